```python
import jax, jax.numpy as jnp
from jax import lax
import numpy as np

D_MODEL = 1024
BATCH = 1
SEQ = 16384
DEPTH = 1

GRID_W = 64
CTX_LEN = 256
RET_HEADS = 4
RET_DK = 256
RET_DV = 512
RET_CHUNK = 128
FOURIER_GROUPS = 4
FOURIER_DG = 256
D_FF = 2816
ROPE_BASE = 10000.0
EPS = 1e-6
N_MOD = 9

RET_QK = RET_HEADS * RET_DK
RET_V = RET_HEADS * RET_DV
FOURIER_W = FOURIER_GROUPS * FOURIER_DG
IN_COLS = 2 * RET_QK + 2 * RET_V + FOURIER_W + 2 * D_MODEL
SPLITS = (RET_QK, 2 * RET_QK, 2 * RET_QK + RET_V, 2 * RET_QK + 2 * RET_V,
          2 * RET_QK + 2 * RET_V + FOURIER_W, 2 * RET_QK + 2 * RET_V + FOURIER_W + D_MODEL)

kernel_name = "hybrid_retention_fnet_macaron_dit_block"


def rmsnorm(x, w):
    xf = x.astype(jnp.float32)
    y = xf * lax.rsqrt(jnp.mean(xf * xf, axis=-1, keepdims=True) + EPS)
    return (y * w.astype(jnp.float32)).astype(x.dtype)


def modulate(h, shift, scale):
    return h * (1 + scale) + shift


def swiglu(h, w13, w2):
    a, b = jnp.split(h @ w13, 2, axis=-1)
    return (jax.nn.silu(a) * b) @ w2


def heads(a, dh):
    B, L, _ = a.shape
    return a.reshape(B, L, -1, dh).transpose(0, 2, 1, 3)


def project(h, w_in):
    u = h @ w_in
    q, k, v, g, f, ga, gf = jnp.split(u, SPLITS, axis=-1)
    q = heads(q, RET_DK).astype(jnp.float32) * (RET_DK ** -0.5)
    k = heads(k, RET_DK).astype(jnp.float32)
    v = heads(v, RET_DV).astype(jnp.float32)
    return q, k, v, g, f, ga, gf


def rope_2d(a, rows):
    row = jnp.repeat(jnp.arange(rows, dtype=jnp.float32), GRID_W)
    col = jnp.tile(jnp.arange(GRID_W, dtype=jnp.float32), rows)
    n_freq = RET_DK // 4
    inv = ROPE_BASE ** (-jnp.arange(n_freq, dtype=jnp.float32) / n_freq)
    ang = jnp.concatenate([row[:, None] * inv, col[:, None] * inv], axis=-1)
    cos, sin = jnp.cos(ang), jnp.sin(ang)
    a1, a2 = a[..., 0::2], a[..., 1::2]
    return jnp.stack([a1 * cos - a2 * sin, a1 * sin + a2 * cos], axis=-1).reshape(a.shape)


def retention_scan(q, k, v, log_gamma, s0):
    B, H, L, dk = q.shape
    dv = v.shape[-1]
    C = RET_CHUNK
    n = L // C
    idx = jnp.arange(C, dtype=jnp.float32)
    lg = log_gamma[:, None]
    diff = idx[:, None] - idx[None, :]
    decay_intra = jnp.where(diff >= 0, jnp.exp(lg[:, :, None] * jnp.maximum(diff, 0.0)), 0.0)
    xi = jnp.exp(lg * (idx + 1.0))
    zeta = jnp.exp(lg * (C - 1.0 - idx))
    g_chunk = jnp.exp(log_gamma * C)

    def to_chunks(a):
        return jnp.moveaxis(a.reshape(B, H, n, C, a.shape[-1]), 2, 0)

    def step(s, qkv):
        qc, kc, vc = qkv
        sc = jnp.einsum('bhid,bhjd->bhij', qc, kc) * decay_intra
        o = (jnp.einsum('bhij,bhjv->bhiv', sc, vc)
             + jnp.einsum('bhid,bhdv->bhiv', qc, s) * xi[None, :, :, None])
        s = s * g_chunk[None, :, None, None] + jnp.einsum('bhjd,bhjv->bhdv', kc * zeta[None, :, :, None], vc)
        return s, o

    s, o = lax.scan(step, s0, (to_chunks(q), to_chunks(k), to_chunks(v)))
    o = jnp.moveaxis(o, 0, 2).reshape(B, H, L, dv)
    return o, s


def bidir_retention(q, k, v, lg_f, lg_b, s_f0, s_b0):
    o_f, s_f = retention_scan(q, k, v, lg_f, s_f0)
    flip = lambda a: jnp.flip(a, axis=2)
    o_b, s_b = retention_scan(flip(q), flip(k), flip(v), lg_b, s_b0)
    return o_f + flip(o_b), s_f, s_b


def retention_out(o, g, gn_w, p_ret):
    mu = jnp.mean(o, axis=-1, keepdims=True)
    var = jnp.mean(jnp.square(o - mu), axis=-1, keepdims=True)
    on = (o - mu) * lax.rsqrt(var + EPS)
    B, H, L, dv = o.shape
    on = on.transpose(0, 2, 1, 3).reshape(B, L, H * dv).astype(g.dtype) * gn_w
    return (jax.nn.silu(g) * on) @ p_ret


def fourier_branch(f, p_four):
    B, L, _ = f.shape
    fg = f.reshape(B, L, FOURIER_GROUPS, FOURIER_DG).astype(jnp.float32)
    mixed = jnp.fft.fft2(fg, axes=(1, 3), norm="ortho").real
    return mixed.reshape(B, L, FOURIER_W).astype(f.dtype) @ p_four


def merge(r, fo, ga, gf, w_out):
    return (jax.nn.sigmoid(ga) * r + jax.nn.sigmoid(gf) * fo) @ w_out


def setup_inputs(seed: int = 0) -> dict:
    key = jax.random.key(seed)
    ks = jax.random.split(key, 24)

    def w(k, shape, fan_in, scale=1.0):
        return jax.random.normal(k, shape, jnp.float32) * (scale * fan_in ** -0.5)

    def gain(k, shape):
        return 1.0 + 0.02 * jax.random.normal(k, shape, jnp.float32)

    decay_base = jnp.asarray(np.log(2.0 ** (5 + np.arange(RET_HEADS)) - 1.0).astype(np.float32))
    return {
        "x": jax.random.normal(ks[0], (BATCH, SEQ, D_MODEL), jnp.float32),
        "c": jax.random.normal(ks[1], (BATCH, D_MODEL), jnp.float32),
        "ctx": jax.random.normal(ks[2], (BATCH, CTX_LEN, D_MODEL), jnp.float32),
        "c_ctx": 0.5 * jax.random.normal(ks[3], (D_MODEL,), jnp.float32),
        "w_ada": w(ks[4], (DEPTH, D_MODEL, N_MOD * D_MODEL), D_MODEL, 0.5),
        "b_ada": 0.02 * jax.random.normal(ks[5], (DEPTH, N_MOD * D_MODEL), jnp.float32),
        "norm_ffn1": gain(ks[6], (DEPTH, D_MODEL)),
        "w13_ffn1": w(ks[7], (DEPTH, D_MODEL, 2 * D_FF), D_MODEL),
        "w2_ffn1": w(ks[8], (DEPTH, D_FF, D_MODEL), D_FF),
        "norm_mix": gain(ks[9], (DEPTH, D_MODEL)),
        "w_in": w(ks[10], (DEPTH, D_MODEL, IN_COLS), D_MODEL),
        "decay_fwd": decay_base[None, :] + 0.05 * jax.random.normal(ks[11], (DEPTH, RET_HEADS), jnp.float32),
        "decay_bwd": decay_base[None, :] + 0.05 * jax.random.normal(ks[12], (DEPTH, RET_HEADS), jnp.float32),
        "ret_gn_w": gain(ks[13], (DEPTH, RET_V)),
        "p_ret": w(ks[14], (DEPTH, RET_V, D_MODEL), RET_V),
        "p_four": w(ks[15], (DEPTH, FOURIER_W, D_MODEL), FOURIER_W),
        "w_out": w(ks[16], (DEPTH, D_MODEL, D_MODEL), D_MODEL),
        "norm_ffn2": gain(ks[17], (DEPTH, D_MODEL)),
        "w13_ffn2": w(ks[18], (DEPTH, D_MODEL, 2 * D_FF), D_MODEL),
        "w2_ffn2": w(ks[19], (DEPTH, D_FF, D_MODEL), D_FF),
        "norm_final": gain(ks[20], (D_MODEL,)),
    }


def reference(x, c, ctx, c_ctx, w_ada, b_ada, norm_ffn1, w13_ffn1, w2_ffn1, norm_mix, w_in,
              decay_fwd, decay_bwd, ret_gn_w, p_ret, p_four, w_out, norm_ffn2, w13_ffn2, w2_ffn2,
              norm_final):
    B, L, _ = x.shape
    rows = L // GRID_W
    for l in range(DEPTH):
        last = l == DEPTH - 1
        mod_x = (jax.nn.silu(c) @ w_ada[l] + b_ada[l])[:, None, :]
        mod_c = (jax.nn.silu(c_ctx) @ w_ada[l] + b_ada[l])[None, None, :]
        sh1x, sc1x, g1x, sh2x, sc2x, g2x, sh3x, sc3x, g3x = jnp.split(mod_x, N_MOD, axis=-1)
        sh1c, sc1c, g1c, sh2c, sc2c, g2c, sh3c, sc3c, g3c = jnp.split(mod_c, N_MOD, axis=-1)

        x = x + 0.5 * g1x * swiglu(modulate(rmsnorm(x, norm_ffn1[l]), sh1x, sc1x), w13_ffn1[l], w2_ffn1[l])
        ctx = ctx + 0.5 * g1c * swiglu(modulate(rmsnorm(ctx, norm_ffn1[l]), sh1c, sc1c), w13_ffn1[l], w2_ffn1[l])

        lg_f = jax.nn.log_sigmoid(decay_fwd[l].astype(jnp.float32))
        lg_b = jax.nn.log_sigmoid(decay_bwd[l].astype(jnp.float32))

        qc, kc, vc, gc, fc, gac, gfc = project(modulate(rmsnorm(ctx, norm_mix[l]), sh2c, sc2c), w_in[l])
        zero = jnp.zeros((ctx.shape[0], RET_HEADS, RET_DK, RET_DV), jnp.float32)
        o_c, s_f, s_b = bidir_retention(qc, kc, vc, lg_f, lg_b, zero, zero)

        qx, kx, vx, gx, fx, gax, gfx = project(modulate(rmsnorm(x, norm_mix[l]), sh2x, sc2x), w_in[l])
        qx = rope_2d(qx, rows)
        kx = rope_2d(kx, rows)
        o_x, _, _ = bidir_retention(qx, kx, vx, lg_f, lg_b, s_f, s_b)
        y_x = merge(retention_out(o_x, gx, ret_gn_w[l], p_ret[l]), fourier_branch(fx, p_four[l]), gax, gfx, w_out[l])
        x = x + g2x * y_x

        x = x + 0.5 * g3x * swiglu(modulate(rmsnorm(x, norm_ffn2[l]), sh3x, sc3x), w13_ffn2[l], w2_ffn2[l])

        if not last:
            y_c = merge(retention_out(o_c, gc, ret_gn_w[l], p_ret[l]), fourier_branch(fc, p_four[l]), gac, gfc, w_out[l])
            ctx = ctx + g2c * y_c
            ctx = ctx + 0.5 * g3c * swiglu(modulate(rmsnorm(ctx, norm_ffn2[l]), sh3c, sc3c), w13_ffn2[l], w2_ffn2[l])
    return rmsnorm(x, norm_final)
```

```python
import functools

import numpy as np
import jax
import jax.numpy as jnp
from jax import lax
from jax.experimental import pallas as pl
from jax.experimental.pallas import tpu as pltpu

D = 1024
L = 16384
GRID_W = 64
CTX = 256
H = 4
DK = 256
DV = 512
QK = H * DK
VW = H * DV
NG = 4
DG = 256
DFF = 2816
ROPE_BASE = 10000.0
EPS = 1e-6
NMOD = 9

F32 = jnp.float32
BF16 = jnp.bfloat16

C = 256
N1 = 128
N2 = 128

VMEM_LIMIT = 56 * 1024 * 1024


def _dot(a, b):
    return jnp.dot(a, b, preferred_element_type=F32)


def _cparams(n_axes):
    return pltpu.CompilerParams(
        dimension_semantics=("arbitrary",) * n_axes, vmem_limit_bytes=VMEM_LIMIT)


def _resident(shape):
    nd = len(shape)
    return pl.BlockSpec(shape, lambda *_: (0,) * nd, pipeline_mode=pl.Buffered(1))


def _mod_spec(j):
    return pl.BlockSpec((8, D), lambda *_: (0, j), pipeline_mode=pl.Buffered(1))


def _norm_mod(x, nw, sc, sh):
    y = x * lax.rsqrt(jnp.mean(x * x, axis=-1, keepdims=True) + EPS)
    return y * (nw * (1.0 + sc)) + sh


def _mods_kernel(cs_ref, w_ref, b_ref, o_ref):
    cs = cs_ref[...]
    s = cs * jax.nn.sigmoid(cs)
    o_ref[...] = jnp.dot(s, w_ref[...], preferred_element_type=F32,
                         precision=lax.Precision.HIGHEST) + b_ref[...]


def _mods(cs, w_ada, b_ada):
    tn = 1024
    return pl.pallas_call(
        _mods_kernel,
        grid=(NMOD * D // tn,),
        in_specs=[
            _resident((8, D)),
            pl.BlockSpec((D, tn), lambda j: (0, j)),
            pl.BlockSpec((1, tn), lambda j: (0, j)),
        ],
        out_specs=pl.BlockSpec((8, tn), lambda j: (0, j)),
        out_shape=jax.ShapeDtypeStruct((8, NMOD * D), F32),
        compiler_params=_cparams(1),
        name="mods",
    )(cs, w_ada, b_ada)


FC = 256


def _ffn_kernel(row, final, x_ref, sh_ref, sc_ref, g_ref, nw_ref, w1_ref, w3_ref, w2_ref, *rest):
    if final:
        nf_ref, o_ref, t_ref = rest
    else:
        o_ref, t_ref = rest
    x = x_ref[...]
    h = _norm_mod(x, nw_ref[...], sc_ref[row:row + 1, :], sh_ref[row:row + 1, :]).astype(BF16)
    for j in range(DFF // FC):
        a = _dot(h, w1_ref[:, j * FC:(j + 1) * FC])
        b = _dot(h, w3_ref[:, j * FC:(j + 1) * FC])
        t_ref[:, j * FC:(j + 1) * FC] = (a * jax.nn.sigmoid(a) * b).astype(BF16)
    y = _dot(t_ref[...], w2_ref[...])
    out = x + (0.5 * g_ref[row:row + 1, :]) * y
    if final:
        out = out * lax.rsqrt(jnp.mean(out * out, axis=-1, keepdims=True) + EPS) * nf_ref[...]
    o_ref[...] = out


def _ffn(x, mods, row, mod0, nw, w1, w3, w2, nf=None, tm=512):
    n = x.shape[0]
    tm = min(tm, n)
    final = nf is not None
    in_specs = [
        pl.BlockSpec((tm, D), lambda i: (i, 0)),
        _mod_spec(mod0), _mod_spec(mod0 + 1), _mod_spec(mod0 + 2),
        _resident((1, D)),
        _resident((D, DFF)), _resident((D, DFF)), _resident((DFF, D)),
    ]
    args = [x, mods, mods, mods, nw, w1, w3, w2]
    if final:
        in_specs.append(_resident((1, D)))
        args.append(nf)
    return pl.pallas_call(
        functools.partial(_ffn_kernel, row, final),
        grid=(n // tm,),
        in_specs=in_specs,
        out_specs=pl.BlockSpec((tm, D), lambda i: (i, 0)),
        out_shape=jax.ShapeDtypeStruct((n, D), F32),
        scratch_shapes=[pltpu.VMEM((tm, DFF), BF16)],
        compiler_params=_cparams(1),
        name="ffn_final" if final else "ffn",
    )(*args)


def _ctx_state_kernel(lg_ref, x_ref, sh_ref, sc_ref, nw_ref, wkT_ref, wv_ref, sf_ref, sb_ref):
    hd = pl.program_id(0)
    h = _norm_mod(x_ref[...], nw_ref[...], sc_ref[1:2, :], sh_ref[1:2, :]).astype(BF16)
    kT = lax.dot_general(wkT_ref[...], h, (((1,), (1,)), ((), ())), preferred_element_type=F32)
    v = _dot(h, wv_ref[...]).astype(BF16)
    j = lax.broadcasted_iota(jnp.int32, (DK, CTX), 1).astype(F32)
    wf = jnp.exp(lg_ref[0, hd] * (CTX - 1.0 - j))
    wb = jnp.exp(lg_ref[1, hd] * j)
    sf_ref[0] = _dot((kT * wf).astype(BF16), v)
    sb_ref[0] = _dot((kT * wb).astype(BF16), v)


def _ctx_state(lg, ctx1, mods, nw, wkT, wv):
    return pl.pallas_call(
        _ctx_state_kernel,
        grid=(H,),
        in_specs=[
            pl.BlockSpec(memory_space=pltpu.SMEM),
            _resident((CTX, D)),
            _mod_spec(3), _mod_spec(4),
            _resident((1, D)),
            pl.BlockSpec((DK, D), lambda h: (h, 0)),
            pl.BlockSpec((D, DV), lambda h: (0, h)),
        ],
        out_specs=[pl.BlockSpec((1, DK, DV), lambda h: (h, 0, 0))] * 2,
        out_shape=[jax.ShapeDtypeStruct((H, DK, DV), F32)] * 2,
        compiler_params=_cparams(1),
        name="ctx_state",
    )(lg, ctx1, mods, mods, nw, wkT, wv)


_OQ, _OV, _OG, _OF, _OGA, _OGF = 0, QK, QK + VW, QK + 2 * VW, QK + 2 * VW + D, QK + 2 * VW + 2 * D
_WCAT = QK + 2 * VW + 3 * D
HALF = DK // 2


def _proj_kernel(x_ref, sh_ref, sc_ref, nw_ref, w_ref, wkT_ref, md_ref, cq_ref, sq_ref, ck_ref, sk_ref,
                 q_ref, kT_ref, v_ref, sg_ref, G_ref, sga_ref, sgf_ref):
    h = _norm_mod(x_ref[...], nw_ref[...], sc_ref[0:1, :], sh_ref[0:1, :]).astype(BF16)

    q = _dot(h, w_ref[:, _OQ:_OQ + QK])
    cq, sq = cq_ref[...], sq_ref[...]
    for hd in range(H):
        a1 = q[:, hd * DK:hd * DK + HALF]
        a2 = q[:, hd * DK + HALF:(hd + 1) * DK]
        q_ref[:, hd * DK:hd * DK + HALF] = (a1 * cq - a2 * sq).astype(BF16)
        q_ref[:, hd * DK + HALF:(hd + 1) * DK] = (a1 * sq + a2 * cq).astype(BF16)

    kT = lax.dot_general(wkT_ref[...], h, (((1,), (1,)), ((), ())), preferred_element_type=F32)
    ck, sk = ck_ref[...], sk_ref[...]
    for hd in range(H):
        a1 = kT[hd * DK:hd * DK + HALF, :]
        a2 = kT[hd * DK + HALF:(hd + 1) * DK, :]
        kT_ref[hd * DK:hd * DK + HALF, :] = (a1 * ck - a2 * sk).astype(BF16)
        kT_ref[hd * DK + HALF:(hd + 1) * DK, :] = (a1 * sk + a2 * ck).astype(BF16)

    v_ref[...] = _dot(h, w_ref[:, _OV:_OV + VW]).astype(BF16)
    g = _dot(h, w_ref[:, _OG:_OG + VW])
    sg_ref[...] = (g * jax.nn.sigmoid(g)).astype(BF16)

    f = _dot(h, w_ref[:, _OF:_OF + D]).astype(BF16)
    for gi in range(NG):
        z = _dot(f[:, gi * DG:(gi + 1) * DG], md_ref[...])
        G_ref[0, :, gi * DG:(gi + 1) * DG] = z[:, :DG].astype(BF16)
        G_ref[1, :, gi * DG:(gi + 1) * DG] = z[:, DG:].astype(BF16)

    sga_ref[...] = jax.nn.sigmoid(_dot(h, w_ref[:, _OGA:_OGA + D])).astype(BF16)
    sgf_ref[...] = jax.nn.sigmoid(_dot(h, w_ref[:, _OGF:_OGF + D])).astype(BF16)


def _proj(x1, mods, nw, wcat, wkT, md, cq, sq, ck, sk, tm=256):
    row = lambda i: (i, 0)
    return pl.pallas_call(
        _proj_kernel,
        grid=(L // tm,),
        in_specs=[
            pl.BlockSpec((tm, D), row),
            _mod_spec(3), _mod_spec(4),
            _resident((1, D)),
            _resident((D, _WCAT)),
            _resident((QK, D)),
            _resident((DG, 2 * DG)),
            pl.BlockSpec((tm, HALF), row), pl.BlockSpec((tm, HALF), row),
            pl.BlockSpec((HALF, tm), lambda i: (0, i)), pl.BlockSpec((HALF, tm), lambda i: (0, i)),
        ],
        out_specs=[
            pl.BlockSpec((tm, QK), row),
            pl.BlockSpec((QK, tm), lambda i: (0, i)),
            pl.BlockSpec((tm, VW), row),
            pl.BlockSpec((tm, VW), row),
            pl.BlockSpec((2, tm, D), lambda i: (0, i, 0)),
            pl.BlockSpec((tm, D), row),
            pl.BlockSpec((tm, D), row),
        ],
        out_shape=[
            jax.ShapeDtypeStruct((L, QK), BF16),
            jax.ShapeDtypeStruct((QK, L), BF16),
            jax.ShapeDtypeStruct((L, VW), BF16),
            jax.ShapeDtypeStruct((L, VW), BF16),
            jax.ShapeDtypeStruct((2, L, D), BF16),
            jax.ShapeDtypeStruct((L, D), BF16),
            jax.ShapeDtypeStruct((L, D), BF16),
        ],
        compiler_params=_cparams(1),
        name="proj",
    )(x1, mods, mods, nw, wcat, wkT, md, cq, sq, ck, sk)


TB = 1024
NCH = TB // C


def _ret_kernel(reverse, lg_ref, q_ref, kT_ref, v_ref, s0_ref, *rest):
    if reverse:
        o_ref, S_ref, D_ref, XI_ref, ZE_ref = rest
    else:
        ob_ref, sg_ref, gnw_ref, o_ref, S_ref, D_ref, XI_ref, ZE_ref = rest
    hd = pl.program_id(0)
    lg = lg_ref[1 if reverse else 0, hd]

    @pl.when(pl.program_id(1) == 0)
    def _():
        S_ref[...] = s0_ref[0]
        r = lax.broadcasted_iota(jnp.int32, (C, C), 0).astype(F32)
        c = lax.broadcasted_iota(jnp.int32, (C, C), 1).astype(F32)
        diff = (c - r) if reverse else (r - c)
        D_ref[...] = jnp.where(diff >= 0, jnp.exp(lg * jnp.maximum(diff, 0.0)), 0.0)
        XI_ref[...] = jnp.exp(lg * ((C - r) if reverse else (r + 1.0)))
        ZE_ref[...] = jnp.exp(lg * (c if reverse else (C - 1.0 - c)))

    g_chunk = jnp.exp(jnp.full((1, DV), lg * C, F32))
    order = range(NCH - 1, -1, -1) if reverse else range(NCH)
    for ci in order:
        rows = slice(ci * C, (ci + 1) * C)
        qc = q_ref[rows, :]
        kTc = kT_ref[:, rows]
        vc = v_ref[rows, :]
        S = S_ref[...]
        p = (_dot(qc, kTc) * D_ref[...]).astype(BF16)
        qs = (qc.astype(F32) * XI_ref[...]).astype(BF16)
        o = _dot(p, vc) + _dot(qs, S.astype(BF16))
        kz = (kTc.astype(F32) * ZE_ref[...]).astype(BF16)
        S_ref[...] = S * g_chunk + _dot(kz, vc)
        if reverse:
            o_ref[rows, :] = o.astype(o_ref.dtype)
        else:
            o = o + ob_ref[rows, :].astype(F32)
            mu = jnp.mean(o, axis=-1, keepdims=True)
            d = o - mu
            var = jnp.mean(d * d, axis=-1, keepdims=True)
            on = d * lax.rsqrt(var + EPS) * gnw_ref[...]
            o_ref[rows, :] = (sg_ref[rows, :].astype(F32) * on).astype(BF16)


def _ret(reverse, lg, q, kT, v, s0, ob=None, sg=None, gnw=None):
    nblk = L // TB
    if reverse:
        tok = lambda h, i: (nblk - 1 - i, h)
        tokT = lambda h, i: (h, nblk - 1 - i)
    else:
        tok = lambda h, i: (i, h)
        tokT = lambda h, i: (h, i)
    in_specs = [
        pl.BlockSpec(memory_space=pltpu.SMEM),
        pl.BlockSpec((TB, DK), tok),
        pl.BlockSpec((DK, TB), tokT),
        pl.BlockSpec((TB, DV), tok),
        pl.BlockSpec((1, DK, DV), lambda h, i: (h, 0, 0)),
    ]
    args = [lg, q, kT, v, s0]
    if not reverse:
        in_specs += [pl.BlockSpec((TB, DV), tok), pl.BlockSpec((TB, DV), tok),
                     pl.BlockSpec((1, DV), lambda h, i: (0, h))]
        args += [ob, sg, gnw]
    return pl.pallas_call(
        functools.partial(_ret_kernel, reverse),
        grid=(H, nblk),
        in_specs=in_specs,
        out_specs=pl.BlockSpec((TB, DV), tok),
        out_shape=jax.ShapeDtypeStruct((L, VW), BF16),
        scratch_shapes=[pltpu.VMEM((DK, DV), F32), pltpu.VMEM((C, C), F32),
                        pltpu.VMEM((C, C), F32), pltpu.VMEM((C, C), F32)],
        compiler_params=_cparams(2),
        name="ret_bwd" if reverse else "ret_fwd",
    )(*args)


FFT_TN = 4096
FFT_KB = 8


def _fft1_kernel(m_ref, z_ref, o_ref):
    o_ref[...] = _dot(m_ref[...], z_ref[...]).astype(BF16)


def _fft1(m1, z):
    ncol = N2 * D
    return pl.pallas_call(
        _fft1_kernel,
        grid=(ncol // FFT_TN,),
        in_specs=[_resident((2 * N1, 2 * N1)), pl.BlockSpec((2 * N1, FFT_TN), lambda j: (0, j))],
        out_specs=pl.BlockSpec((2 * N1, FFT_TN), lambda j: (0, j)),
        out_shape=jax.ShapeDtypeStruct((2 * N1, ncol), BF16),
        compiler_params=_cparams(1),
        name="fft1",
    )(m1, z)


def _fft3_kernel(t_ref, a_ref, pf_ref, o_ref, xr_ref):
    for i in range(FFT_KB):
        xr_ref[i * N2:(i + 1) * N2, :] = _dot(t_ref[i], a_ref[i]).astype(BF16)
    fo = _dot(xr_ref[...], pf_ref[...])
    for i in range(FFT_KB):
        o_ref[:, i * D:(i + 1) * D] = fo[i * N2:(i + 1) * N2, :].astype(BF16)


def _fft3(t3, a, pf):
    return pl.pallas_call(
        _fft3_kernel,
        grid=(N1 // FFT_KB,),
        in_specs=[
            pl.BlockSpec((FFT_KB, N2, 2 * N2), lambda j: (j, 0, 0)),
            pl.BlockSpec((FFT_KB, 2 * N2, D), lambda j: (j, 0, 0)),
            _resident((D, D)),
        ],
        out_specs=pl.BlockSpec((N2, FFT_KB * D), lambda j: (0, j)),
        out_shape=jax.ShapeDtypeStruct((N2, N1 * D), BF16),
        scratch_shapes=[pltpu.VMEM((FFT_KB * N2, D), BF16)],
        compiler_params=_cparams(1),
        name="fft3",
    )(t3, a, pf)


def _merge_kernel(x_ref, g_ref, r_ref, fo_ref, sga_ref, sgf_ref, pr_ref, wo_ref, o_ref):
    r = _dot(r_ref[...], pr_ref[...])
    m = sga_ref[...].astype(F32) * r + sgf_ref[...].astype(F32) * fo_ref[...].astype(F32)
    y = _dot(m.astype(BF16), wo_ref[...])
    o_ref[...] = x_ref[...] + g_ref[0:1, :] * y


def _merge(x1, mods, r_in, fo, sga, sgf, p_ret, w_out, tm=512):
    row = lambda i: (i, 0)
    return pl.pallas_call(
        _merge_kernel,
        grid=(L // tm,),
        in_specs=[
            pl.BlockSpec((tm, D), row),
            _mod_spec(5),
            pl.BlockSpec((tm, VW), row),
            pl.BlockSpec((tm, D), row), pl.BlockSpec((tm, D), row), pl.BlockSpec((tm, D), row),
            _resident((VW, D)), _resident((D, D)),
        ],
        out_specs=pl.BlockSpec((tm, D), row),
        out_shape=jax.ShapeDtypeStruct((L, D), F32),
        compiler_params=_cparams(1),
        name="merge",
    )(x1, mods, r_in, fo, sga, sgf, p_ret, w_out)


def _rope_tables():
    rows = L // GRID_W
    row = jnp.repeat(jnp.arange(rows, dtype=F32), GRID_W)
    col = jnp.tile(jnp.arange(GRID_W, dtype=F32), rows)
    n_freq = DK // 4
    inv = ROPE_BASE ** (-jnp.arange(n_freq, dtype=F32) / n_freq)
    ang = jnp.concatenate([row[:, None] * inv, col[:, None] * inv], axis=-1)
    return jnp.cos(ang), jnp.sin(ang)


def _dft_tables():
    two_pi = 2.0 * np.pi
    cm = (jnp.arange(DG, dtype=jnp.int32)[:, None] * jnp.arange(DG, dtype=jnp.int32)[None, :]) % DG
    th = cm.astype(F32) * (two_pi / DG)
    md = jnp.concatenate([jnp.cos(th), -jnp.sin(th)], axis=1)
    kn = (jnp.arange(N1, dtype=jnp.int32)[:, None] * jnp.arange(N1, dtype=jnp.int32)[None, :]) % N1
    th = kn.astype(F32) * (two_pi / N1)
    c1, s1 = jnp.cos(th), jnp.sin(th)
    m1 = jnp.stack([jnp.concatenate([c1, s1], axis=1), jnp.concatenate([-s1, c1], axis=1)], axis=1)
    m1 = m1.reshape(2 * N1, 2 * N1)
    k = jnp.arange(N1, dtype=jnp.int32)[:, None, None] + N1 * jnp.arange(N2, dtype=jnp.int32)[None, :, None]
    nk = (k * jnp.arange(N2, dtype=jnp.int32)[None, None, :]) % L
    th = nk.astype(F32) * (two_pi / L)
    scale = 1.0 / np.sqrt(float(L) * DG)
    t3 = jnp.concatenate([jnp.cos(th), jnp.sin(th)], axis=-1) * scale
    return md.astype(BF16), m1.astype(BF16), t3.astype(BF16)


def kernel(x, c, ctx, c_ctx, w_ada, b_ada, norm_ffn1, w13_ffn1, w2_ffn1, norm_mix, w_in, decay_fwd, decay_bwd,
           ret_gn_w, p_ret, p_four, w_out, norm_ffn2, w13_ffn2, w2_ffn2, norm_final):
    assert x.shape == (1, L, D) and ctx.shape == (1, CTX, D) and w_ada.shape[0] == 1

    perm = np.concatenate([hd * DK + np.concatenate([np.arange(0, DK, 2), np.arange(1, DK, 2)])
                           for hd in range(H)])
    wi = w_in[0]
    wq = wi[:, :QK][:, perm]
    wkT = wi[:, QK:2 * QK][:, perm].T.astype(BF16)
    wcat = jnp.concatenate([wq, wi[:, 2 * QK:]], axis=1).astype(BF16)
    wv = wi[:, 2 * QK:2 * QK + VW].astype(BF16)
    w1a, w3a = w13_ffn1[0, :, :DFF].astype(BF16), w13_ffn1[0, :, DFF:].astype(BF16)
    w1b, w3b = w13_ffn2[0, :, :DFF].astype(BF16), w13_ffn2[0, :, DFF:].astype(BF16)
    w2a, w2b = w2_ffn1[0].astype(BF16), w2_ffn2[0].astype(BF16)
    pr, pf, wo = p_ret[0].astype(BF16), p_four[0].astype(BF16), w_out[0].astype(BF16)

    lg = jnp.stack([jax.nn.log_sigmoid(decay_fwd[0].astype(F32)),
                    jax.nn.log_sigmoid(decay_bwd[0].astype(F32))])
    cos, sin = _rope_tables()
    qscale = DK ** -0.5
    cq, sq, ck, sk = cos * qscale, sin * qscale, cos.T, sin.T
    md, m1, t3 = _dft_tables()

    cs = jnp.concatenate([c, c_ctx[None, :], jnp.zeros((6, D), F32)], axis=0)
    mods = _mods(cs, w_ada[0], b_ada)

    x0 = x[0]
    x1 = _ffn(x0, mods, 0, 0, norm_ffn1, w1a, w3a, w2a)
    ctx1 = _ffn(ctx[0], mods, 1, 0, norm_ffn1, w1a, w3a, w2a)

    s_f, s_b = _ctx_state(lg, ctx1, mods, norm_mix, wkT, wv)

    q, kT, v, sg, G, sga, sgf = _proj(x1, mods, norm_mix, wcat, wkT, md, cq, sq, ck, sk)

    o_b = _ret(True, lg, q, kT, v, s_b)
    r_in = _ret(False, lg, q, kT, v, s_f, o_b, sg, ret_gn_w)

    a = _fft1(m1, G.reshape(2 * N1, N2 * D))
    fo = _fft3(t3, a.reshape(N1, 2 * N2, D), pf).reshape(L, D)

    x2 = _merge(x1, mods, r_in, fo, sga, sgf, pr, wo)
    out = _ffn(x2, mods, 0, 6, norm_ffn2, w1b, w3b, w2b, nf=norm_final[None, :])
    return out[None]
```

```python
import functools

import numpy as np
import jax
import jax.numpy as jnp
from jax import lax
from jax.experimental import pallas as pl
from jax.experimental.pallas import tpu as pltpu

D = 1024
L = 16384
GRID_W = 64
CTX = 256
H = 4
DK = 256
DV = 512
QK = H * DK
VW = H * DV
NG = 4
DG = 256
DFF = 2816
ROPE_BASE = 10000.0
EPS = 1e-6
NMOD = 9

F32 = jnp.float32
BF16 = jnp.bfloat16

C = 256
N1 = 128
N2 = 128

VMEM_LIMIT = 56 * 1024 * 1024


def _dot(a, b):
    return jnp.dot(a, b, preferred_element_type=F32)


def _cparams(n_axes):
    return pltpu.CompilerParams(
        dimension_semantics=("arbitrary",) * n_axes, vmem_limit_bytes=VMEM_LIMIT)


def _resident(shape):
    nd = len(shape)
    return pl.BlockSpec(shape, lambda *_: (0,) * nd, pipeline_mode=pl.Buffered(1))


def _mod_spec(j):
    return pl.BlockSpec((8, D), lambda *_: (0, j), pipeline_mode=pl.Buffered(1))


def _norm_mod(x, nw, sc, sh):
    y = x * lax.rsqrt(jnp.mean(x * x, axis=-1, keepdims=True) + EPS)
    return y * (nw * (1.0 + sc)) + sh


def _mods_kernel(cs_ref, w_ref, b_ref, o_ref):
    cs = cs_ref[...]
    s = cs * jax.nn.sigmoid(cs)
    o_ref[...] = jnp.dot(s, w_ref[...], preferred_element_type=F32,
                         precision=lax.Precision.HIGHEST) + b_ref[...]


def _mods(cs, w_ada, b_ada):
    tn = 1024
    return pl.pallas_call(
        _mods_kernel,
        grid=(NMOD * D // tn,),
        in_specs=[
            _resident((8, D)),
            pl.BlockSpec((D, tn), lambda j: (0, j)),
            pl.BlockSpec((1, tn), lambda j: (0, j)),
        ],
        out_specs=pl.BlockSpec((8, tn), lambda j: (0, j)),
        out_shape=jax.ShapeDtypeStruct((8, NMOD * D), F32),
        compiler_params=_cparams(1),
        name="mods",
    )(cs, w_ada, b_ada)


FC = 256


def _ffn_kernel(row, final, x_ref, sh_ref, sc_ref, g_ref, nw_ref, w13_ref, w2_ref, *rest):
    if final:
        nf_ref, o_ref, t_ref = rest
    else:
        o_ref, t_ref = rest
    x = x_ref[...]
    h = _norm_mod(x, nw_ref[...], sc_ref[row:row + 1, :], sh_ref[row:row + 1, :]).astype(BF16)
    for j in range(DFF // FC):
        a = _dot(h, w13_ref[:, j * FC:(j + 1) * FC])
        b = _dot(h, w13_ref[:, DFF + j * FC:DFF + (j + 1) * FC])
        t_ref[:, j * FC:(j + 1) * FC] = (a * jax.nn.sigmoid(a) * b).astype(BF16)
    y = _dot(t_ref[...], w2_ref[...])
    out = x + (0.5 * g_ref[row:row + 1, :]) * y
    if final:
        out = out * lax.rsqrt(jnp.mean(out * out, axis=-1, keepdims=True) + EPS) * nf_ref[...]
    o_ref[...] = out


def _ffn(x, mods, row, mod0, nw, w13, w2, nf=None, tm=512):
    n = x.shape[0]
    tm = min(tm, n)
    final = nf is not None
    in_specs = [
        pl.BlockSpec((tm, D), lambda i: (i, 0)),
        _mod_spec(mod0), _mod_spec(mod0 + 1), _mod_spec(mod0 + 2),
        _resident((1, D)),
        _resident((D, 2 * DFF)), _resident((DFF, D)),
    ]
    args = [x, mods, mods, mods, nw, w13, w2]
    if final:
        in_specs.append(_resident((1, D)))
        args.append(nf)
    return pl.pallas_call(
        functools.partial(_ffn_kernel, row, final),
        grid=(n // tm,),
        in_specs=in_specs,
        out_specs=pl.BlockSpec((tm, D), lambda i: (i, 0)),
        out_shape=jax.ShapeDtypeStruct((n, D), F32),
        scratch_shapes=[pltpu.VMEM((tm, DFF), BF16)],
        compiler_params=_cparams(1),
        name="ffn_final" if final else "ffn",
    )(*args)


def _ctx_state_kernel(lg_ref, x_ref, sh_ref, sc_ref, nw_ref, wkT_ref, wv_ref, sf_ref, sb_ref):
    hd = pl.program_id(0)
    h = _norm_mod(x_ref[...], nw_ref[...], sc_ref[1:2, :], sh_ref[1:2, :]).astype(BF16)
    kT = lax.dot_general(wkT_ref[...], h, (((1,), (1,)), ((), ())), preferred_element_type=F32)
    v = _dot(h, wv_ref[...]).astype(BF16)
    j = lax.broadcasted_iota(jnp.int32, (DK, CTX), 1).astype(F32)
    wf = jnp.exp(lg_ref[0, hd] * (CTX - 1.0 - j))
    wb = jnp.exp(lg_ref[1, hd] * j)
    sf_ref[0] = _dot((kT * wf).astype(BF16), v)
    sb_ref[0] = _dot((kT * wb).astype(BF16), v)


def _ctx_state(lg, ctx1, mods, nw, wkT, wall):
    return pl.pallas_call(
        _ctx_state_kernel,
        grid=(H,),
        in_specs=[
            pl.BlockSpec(memory_space=pltpu.SMEM),
            _resident((CTX, D)),
            _mod_spec(3), _mod_spec(4),
            _resident((1, D)),
            pl.BlockSpec((DK, D), lambda h: (h, 0)),
            pl.BlockSpec((D, DV), lambda h: (0, _OV // DV + h)),
        ],
        out_specs=[pl.BlockSpec((1, DK, DV), lambda h: (h, 0, 0))] * 2,
        out_shape=[jax.ShapeDtypeStruct((H, DK, DV), F32)] * 2,
        compiler_params=_cparams(1),
        name="ctx_state",
    )(lg, ctx1, mods, mods, nw, wkT, wall)


_OQ, _OK, _OV, _OG = 0, QK, 2 * QK, 2 * QK + VW
_OF, _OGA, _OGF = 2 * QK + 2 * VW, 2 * QK + 2 * VW + D, 2 * QK + 2 * VW + 2 * D
_WIN = 2 * QK + 2 * VW + 3 * D
HALF = DK // 2


def _proj_kernel(x_ref, sh_ref, sc_ref, nw_ref, w_ref, wkT_ref, md_ref, gnw_ref, cq_ref, sq_ref, ck_ref, sk_ref,
                 q_ref, kT_ref, v_ref, sg_ref, G_ref, sga_ref, sgf_ref):
    h = _norm_mod(x_ref[...], nw_ref[...], sc_ref[0:1, :], sh_ref[0:1, :]).astype(BF16)

    q = _dot(h, w_ref[:, _OQ:_OQ + QK])
    cq, sq = cq_ref[...], sq_ref[...]
    for hd in range(H):
        a1 = q[:, hd * DK:hd * DK + HALF]
        a2 = q[:, hd * DK + HALF:(hd + 1) * DK]
        q_ref[:, hd * DK:hd * DK + HALF] = (a1 * cq - a2 * sq).astype(BF16)
        q_ref[:, hd * DK + HALF:(hd + 1) * DK] = (a1 * sq + a2 * cq).astype(BF16)

    kT = lax.dot_general(wkT_ref[...], h, (((1,), (1,)), ((), ())), preferred_element_type=F32)
    ck, sk = ck_ref[...], sk_ref[...]
    for hd in range(H):
        a1 = kT[hd * DK:hd * DK + HALF, :]
        a2 = kT[hd * DK + HALF:(hd + 1) * DK, :]
        kT_ref[hd * DK:hd * DK + HALF, :] = (a1 * ck - a2 * sk).astype(BF16)
        kT_ref[hd * DK + HALF:(hd + 1) * DK, :] = (a1 * sk + a2 * ck).astype(BF16)

    v_ref[...] = _dot(h, w_ref[:, _OV:_OV + VW]).astype(BF16)
    g = _dot(h, w_ref[:, _OG:_OG + VW])
    sg_ref[...] = (g * jax.nn.sigmoid(g) * gnw_ref[...]).astype(BF16)

    f = _dot(h, w_ref[:, _OF:_OF + D]).astype(BF16)
    for gi in range(NG):
        z = _dot(f[:, gi * DG:(gi + 1) * DG], md_ref[...])
        G_ref[0, :, gi * DG:(gi + 1) * DG] = z[:, :DG].astype(BF16)
        G_ref[1, :, gi * DG:(gi + 1) * DG] = z[:, DG:].astype(BF16)

    sga_ref[...] = jax.nn.sigmoid(_dot(h, w_ref[:, _OGA:_OGA + D])).astype(BF16)
    sgf_ref[...] = jax.nn.sigmoid(_dot(h, w_ref[:, _OGF:_OGF + D])).astype(BF16)


def _proj(x1, mods, nw, wall, wkT, md, gnw, cq, sq, ck, sk, tm=256):
    row = lambda i: (i, 0)
    return pl.pallas_call(
        _proj_kernel,
        grid=(L // tm,),
        in_specs=[
            pl.BlockSpec((tm, D), row),
            _mod_spec(3), _mod_spec(4),
            _resident((1, D)),
            _resident((D, _WIN)),
            _resident((QK, D)),
            _resident((DG, 2 * DG)),
            _resident((1, VW)),
            pl.BlockSpec((tm, HALF), row), pl.BlockSpec((tm, HALF), row),
            pl.BlockSpec((HALF, tm), lambda i: (0, i)), pl.BlockSpec((HALF, tm), lambda i: (0, i)),
        ],
        out_specs=[
            pl.BlockSpec((tm, QK), row),
            pl.BlockSpec((QK, tm), lambda i: (0, i)),
            pl.BlockSpec((tm, VW), row),
            pl.BlockSpec((tm, VW), row),
            pl.BlockSpec((2, tm, D), lambda i: (0, i, 0)),
            pl.BlockSpec((tm, D), row),
            pl.BlockSpec((tm, D), row),
        ],
        out_shape=[
            jax.ShapeDtypeStruct((L, QK), BF16),
            jax.ShapeDtypeStruct((QK, L), BF16),
            jax.ShapeDtypeStruct((L, VW), BF16),
            jax.ShapeDtypeStruct((L, VW), BF16),
            jax.ShapeDtypeStruct((2, L, D), BF16),
            jax.ShapeDtypeStruct((L, D), BF16),
            jax.ShapeDtypeStruct((L, D), BF16),
        ],
        compiler_params=_cparams(1),
        name="proj",
    )(x1, mods, mods, nw, wall, wkT, md, gnw, cq, sq, ck, sk)


TB_B = 4096
TB_F = 2048
NCHUNK = L // C


def _ret_bwd_kernel(lg_ref, kT_ref, v_ref, s0_ref, sb_ref, S_ref, ZE_ref):
    lg = lg_ref[1, pl.program_id(0)]

    @pl.when(pl.program_id(1) == 0)
    def _():
        S_ref[...] = s0_ref[0]
        c = lax.broadcasted_iota(jnp.int32, (C, C), 1).astype(F32)
        ZE_ref[...] = jnp.exp(lg * c)

    g_chunk = jnp.exp(jnp.full((1, DV), lg * C, F32))
    for ci in range(TB_B // C - 1, -1, -1):
        rows = slice(ci * C, (ci + 1) * C)
        S = S_ref[...]
        sb_ref[0, ci] = S.astype(BF16)
        kz = (kT_ref[:, rows].astype(F32) * ZE_ref[...]).astype(BF16)
        S_ref[...] = S * g_chunk + _dot(kz, v_ref[rows, :])


def _ret_bwd(lg, kT, v, s0):
    nblk = L // TB_B
    nch = TB_B // C
    return pl.pallas_call(
        _ret_bwd_kernel,
        grid=(H, nblk),
        in_specs=[
            pl.BlockSpec(memory_space=pltpu.SMEM),
            pl.BlockSpec((DK, TB_B), lambda h, i: (h, nblk - 1 - i)),
            pl.BlockSpec((TB_B, DV), lambda h, i: (nblk - 1 - i, h)),
            pl.BlockSpec((1, DK, DV), lambda h, i: (h, 0, 0)),
        ],
        out_specs=pl.BlockSpec((1, nch, DK, DV), lambda h, i: (h, nblk - 1 - i, 0, 0)),
        out_shape=jax.ShapeDtypeStruct((H, NCHUNK, DK, DV), BF16),
        scratch_shapes=[pltpu.VMEM((DK, DV), F32), pltpu.VMEM((C, C), F32)],
        compiler_params=_cparams(2),
        name="ret_bwd",
    )(lg, kT, v, s0)


def _ret_fwd_kernel(lg_ref, q_ref, kT_ref, v_ref, s0_ref, sb_ref, sgw_ref, o_ref,
                    S_ref, D_ref, XF_ref, XB_ref, ZE_ref):
    hd = pl.program_id(0)
    lgf, lgb = lg_ref[0, hd], lg_ref[1, hd]

    @pl.when(pl.program_id(1) == 0)
    def _():
        S_ref[...] = s0_ref[0]
        r = lax.broadcasted_iota(jnp.int32, (C, C), 0).astype(F32)
        c = lax.broadcasted_iota(jnp.int32, (C, C), 1).astype(F32)
        d = r - c
        D_ref[...] = (jnp.where(d >= 0, jnp.exp(lgf * jnp.maximum(d, 0.0)), 0.0)
                      + jnp.where(d <= 0, jnp.exp(lgb * jnp.maximum(-d, 0.0)), 0.0))
        XF_ref[...] = jnp.exp(lgf * (r + 1.0))
        XB_ref[...] = jnp.exp(lgb * (C - r))
        ZE_ref[...] = jnp.exp(lgf * (C - 1.0 - c))

    g_chunk = jnp.exp(jnp.full((1, DV), lgf * C, F32))
    for ci in range(TB_F // C):
        rows = slice(ci * C, (ci + 1) * C)
        qc = q_ref[rows, :]
        kTc = kT_ref[:, rows]
        vc = v_ref[rows, :]
        S = S_ref[...]
        p = (_dot(qc, kTc) * D_ref[...]).astype(BF16)
        qf = (qc.astype(F32) * XF_ref[...]).astype(BF16)
        qb = (qc.astype(F32) * XB_ref[...]).astype(BF16)
        o = _dot(p, vc) + _dot(qf, S.astype(BF16)) + _dot(qb, sb_ref[0, ci])
        kz = (kTc.astype(F32) * ZE_ref[...]).astype(BF16)
        S_ref[...] = S * g_chunk + _dot(kz, vc)
        mu = jnp.mean(o, axis=-1, keepdims=True)
        d = o - mu
        var = jnp.mean(d * d, axis=-1, keepdims=True)
        on = d * lax.rsqrt(var + EPS)
        o_ref[rows, :] = (sgw_ref[rows, :].astype(F32) * on).astype(BF16)


def _ret_fwd(lg, q, kT, v, s0, sb, sgw):
    nch = TB_F // C
    tok = lambda h, i: (i, h)
    return pl.pallas_call(
        _ret_fwd_kernel,
        grid=(H, L // TB_F),
        in_specs=[
            pl.BlockSpec(memory_space=pltpu.SMEM),
            pl.BlockSpec((TB_F, DK), tok),
            pl.BlockSpec((DK, TB_F), lambda h, i: (h, i)),
            pl.BlockSpec((TB_F, DV), tok),
            pl.BlockSpec((1, DK, DV), lambda h, i: (h, 0, 0)),
            pl.BlockSpec((1, nch, DK, DV), lambda h, i: (h, i, 0, 0)),
            pl.BlockSpec((TB_F, DV), tok),
        ],
        out_specs=pl.BlockSpec((TB_F, DV), tok),
        out_shape=jax.ShapeDtypeStruct((L, VW), BF16),
        scratch_shapes=[pltpu.VMEM((DK, DV), F32)] + [pltpu.VMEM((C, C), F32)] * 4,
        compiler_params=_cparams(2),
        name="ret_fwd",
    )(lg, q, kT, v, s0, sb, sgw)


FFT_NT = 16
FFT_KB = 16


def _fft1_kernel(m_ref, g_ref, o_ref):
    z = jnp.transpose(g_ref[...], (2, 0, 1, 3)).reshape(FFT_NT, 2 * N1, D)
    for t in range(FFT_NT):
        o_ref[t] = _dot(m_ref[...], z[t]).astype(BF16)


def _fft1(m1, g4):
    return pl.pallas_call(
        _fft1_kernel,
        grid=(N2 // FFT_NT,),
        in_specs=[_resident((2 * N1, 2 * N1)),
                  pl.BlockSpec((2, N1, FFT_NT, D), lambda j: (0, 0, j, 0))],
        out_specs=pl.BlockSpec((FFT_NT, 2 * N1, D), lambda j: (j, 0, 0)),
        out_shape=jax.ShapeDtypeStruct((N2, 2 * N1, D), BF16),
        compiler_params=_cparams(1),
        name="fft1",
    )(m1, g4)


def _fft3_kernel(t_ref, pf_ref, a_ref, o_ref, xr_ref):
    z = jnp.transpose(a_ref[...], (2, 1, 0, 3)).reshape(FFT_KB, 2 * N2, D)
    for i in range(FFT_KB):
        xr_ref[i * N2:(i + 1) * N2, :] = _dot(t_ref[i], z[i]).astype(BF16)
    fo = _dot(xr_ref[...], pf_ref[...]).astype(BF16).reshape(FFT_KB, N2, D)
    o_ref[...] = jnp.transpose(fo, (1, 0, 2))


def _fft3(t3, a4, pf):
    return pl.pallas_call(
        _fft3_kernel,
        grid=(N1 // FFT_KB,),
        in_specs=[pl.BlockSpec((FFT_KB, N2, 2 * N2), lambda j: (j, 0, 0)),
                  _resident((D, D)),
                  pl.BlockSpec((N2, 2, FFT_KB, D), lambda j: (0, 0, j, 0))],
        out_specs=pl.BlockSpec((N2, FFT_KB, D), lambda j: (0, j, 0)),
        out_shape=jax.ShapeDtypeStruct((N2, N1, D), BF16),
        scratch_shapes=[pltpu.VMEM((FFT_KB * N2, D), BF16)],
        compiler_params=_cparams(1),
        name="fft3",
    )(t3, pf, a4)


def _merge_kernel(x_ref, g_ref, r_ref, fo_ref, sga_ref, sgf_ref, pr_ref, wo_ref, o_ref):
    r = _dot(r_ref[...], pr_ref[...])
    m = sga_ref[...].astype(F32) * r + sgf_ref[...].astype(F32) * fo_ref[...].astype(F32)
    y = _dot(m.astype(BF16), wo_ref[...])
    o_ref[...] = x_ref[...] + g_ref[0:1, :] * y


def _merge(x1, mods, r_in, fo, sga, sgf, p_ret, w_out, tm=512):
    row = lambda i: (i, 0)
    return pl.pallas_call(
        _merge_kernel,
        grid=(L // tm,),
        in_specs=[
            pl.BlockSpec((tm, D), row),
            _mod_spec(5),
            pl.BlockSpec((tm, VW), row),
            pl.BlockSpec((tm, D), row), pl.BlockSpec((tm, D), row), pl.BlockSpec((tm, D), row),
            _resident((VW, D)), _resident((D, D)),
        ],
        out_specs=pl.BlockSpec((tm, D), row),
        out_shape=jax.ShapeDtypeStruct((L, D), F32),
        compiler_params=_cparams(1),
        name="merge",
    )(x1, mods, r_in, fo, sga, sgf, p_ret, w_out)


def _rope_tables():
    rows = L // GRID_W
    row = np.repeat(np.arange(rows, dtype=np.float64), GRID_W)
    col = np.tile(np.arange(GRID_W, dtype=np.float64), rows)
    n_freq = DK // 4
    inv = ROPE_BASE ** (-np.arange(n_freq, dtype=np.float64) / n_freq)
    ang = np.concatenate([row[:, None] * inv, col[:, None] * inv], axis=-1)
    return np.cos(ang), np.sin(ang)


def _dft_tables():
    two_pi = 2.0 * np.pi
    th = (np.outer(np.arange(DG), np.arange(DG)) % DG) * (two_pi / DG)
    md = np.concatenate([np.cos(th), -np.sin(th)], axis=1)
    th = (np.outer(np.arange(N1), np.arange(N1)) % N1) * (two_pi / N1)
    c1, s1 = np.cos(th), np.sin(th)
    m1 = np.block([[c1, s1], [-s1, c1]])
    k = np.arange(N1)[:, None, None] + N1 * np.arange(N2)[None, :, None]
    th = ((k * np.arange(N2)[None, None, :]) % L) * (two_pi / L)
    t3 = np.concatenate([np.cos(th), np.sin(th)], axis=-1) / np.sqrt(float(L) * DG)
    f32 = lambda a: jnp.asarray(a.astype(np.float32))
    return f32(md).astype(BF16), f32(m1).astype(BF16), f32(t3).astype(BF16)


def kernel(x, c, ctx, c_ctx, w_ada, b_ada, norm_ffn1, w13_ffn1, w2_ffn1, norm_mix, w_in, decay_fwd, decay_bwd,
           ret_gn_w, p_ret, p_four, w_out, norm_ffn2, w13_ffn2, w2_ffn2, norm_final):
    assert x.shape == (1, L, D) and ctx.shape == (1, CTX, D) and w_ada.shape[0] == 1

    perm = np.concatenate([hd * DK + np.concatenate([np.arange(0, DK, 2), np.arange(1, DK, 2)])
                           for hd in range(2 * H)] + [np.arange(2 * QK, _WIN)])
    wall = w_in[0][:, perm].astype(BF16)
    wkT = wall[:, _OK:_OK + QK].T
    w13a, w13b = w13_ffn1[0].astype(BF16), w13_ffn2[0].astype(BF16)
    w2a, w2b = w2_ffn1[0].astype(BF16), w2_ffn2[0].astype(BF16)
    pr, pf, wo = p_ret[0].astype(BF16), p_four[0].astype(BF16), w_out[0].astype(BF16)

    lg = jnp.stack([jax.nn.log_sigmoid(decay_fwd[0].astype(F32)),
                    jax.nn.log_sigmoid(decay_bwd[0].astype(F32))])
    cos, sin = _rope_tables()
    qscale = DK ** -0.5
    f32c = lambda a: jnp.asarray(np.ascontiguousarray(a).astype(np.float32))
    cq, sq, ck, sk = f32c(cos * qscale), f32c(sin * qscale), f32c(cos.T), f32c(sin.T)
    md, m1, t3 = _dft_tables()

    cs = jnp.concatenate([c, c_ctx[None, :], jnp.zeros((6, D), F32)], axis=0)
    mods = _mods(cs, w_ada[0], b_ada)

    x0 = x[0]
    x1 = _ffn(x0, mods, 0, 0, norm_ffn1, w13a, w2a)
    ctx1 = _ffn(ctx[0], mods, 1, 0, norm_ffn1, w13a, w2a)

    s_f, s_b = _ctx_state(lg, ctx1, mods, norm_mix, wkT, wall)

    q, kT, v, sgw, G, sga, sgf = _proj(x1, mods, norm_mix, wall, wkT, md, ret_gn_w, cq, sq, ck, sk)

    sb = _ret_bwd(lg, kT, v, s_b)
    r_in = _ret_fwd(lg, q, kT, v, s_f, sb, sgw)

    a = _fft1(m1, G.reshape(2, N1, N2, D))
    fo = _fft3(t3, a.reshape(N2, 2, N1, D), pf).reshape(L, D)

    x2 = _merge(x1, mods, r_in, fo, sga, sgf, pr, wo)
    out = _ffn(x2, mods, 0, 6, norm_ffn2, w13b, w2b, nf=norm_final[None, :])
    return out[None]
```

```python
import functools

import numpy as np
import jax
import jax.numpy as jnp
from jax import lax
from jax.experimental import pallas as pl
from jax.experimental.pallas import tpu as pltpu

D = 1024
L = 16384
GRID_W = 64
CTX = 256
H = 4
DK = 256
DV = 512
QK = H * DK
VW = H * DV
NG = 4
DG = 256
DFF = 2816
ROPE_BASE = 10000.0
EPS = 1e-6
NMOD = 9

F32 = jnp.float32
BF16 = jnp.bfloat16

C = 256
N1 = 128
N2 = 128

VMEM_LIMIT = 56 * 1024 * 1024


def _dot(a, b):
    return jnp.dot(a, b, preferred_element_type=F32)


def _cparams(n_axes):
    return pltpu.CompilerParams(
        dimension_semantics=("arbitrary",) * n_axes, vmem_limit_bytes=VMEM_LIMIT)


def _resident(shape):
    nd = len(shape)
    return pl.BlockSpec(shape, lambda *_: (0,) * nd, pipeline_mode=pl.Buffered(1))


def _mod_spec(j):
    return pl.BlockSpec((8, D), lambda *_: (0, j), pipeline_mode=pl.Buffered(1))


def _norm_mod(x, nw, sc, sh):
    y = x * lax.rsqrt(jnp.mean(x * x, axis=-1, keepdims=True) + EPS)
    return y * (nw * (1.0 + sc)) + sh


def _mods_kernel(cs_ref, w_ref, b_ref, o_ref):
    cs = cs_ref[...]
    s = cs * jax.nn.sigmoid(cs)
    w = w_ref[...]
    rows = [jnp.sum(w * s[:, r:r + 1], axis=0, keepdims=True) for r in range(2)]
    o_ref[...] = jnp.concatenate(rows + [jnp.zeros((6, w.shape[1]), F32)], axis=0) + b_ref[...]


def _mods(cs, w_ada, b_ada):
    tn = 1024
    return pl.pallas_call(
        _mods_kernel,
        grid=(NMOD * D // tn,),
        in_specs=[
            _resident((D, 8)),
            pl.BlockSpec((D, tn), lambda j: (0, j)),
            pl.BlockSpec((1, tn), lambda j: (0, j)),
        ],
        out_specs=pl.BlockSpec((8, tn), lambda j: (0, j)),
        out_shape=jax.ShapeDtypeStruct((8, NMOD * D), F32),
        compiler_params=_cparams(1),
        name="mods",
    )(cs, w_ada, b_ada)


def _prep_kernel(w_ref, p_ref, pT_ref, o_ref, kT_ref):
    j = pl.program_id(0)
    w = w_ref[...].astype(BF16)

    @pl.when(j >= 2)
    def _():
        o_ref[...] = w

    @pl.when(j < 2)
    def _():
        for hd in range(H):
            o_ref[:, hd * DK:(hd + 1) * DK] = _dot(w[:, hd * DK:(hd + 1) * DK], p_ref[...]).astype(BF16)

    @pl.when(j == 1)
    def _():
        for hd in range(H):
            kT_ref[hd * DK:(hd + 1) * DK, :] = lax.dot_general(
                pT_ref[...], w[:, hd * DK:(hd + 1) * DK], (((1,), (1,)), ((), ())),
                preferred_element_type=F32).astype(BF16)


def _prep(w_in, pm, pmT):
    tn = QK
    return pl.pallas_call(
        _prep_kernel,
        grid=(_WIN // tn,),
        in_specs=[pl.BlockSpec((D, tn), lambda j: (0, j)), _resident((DK, DK)), _resident((DK, DK))],
        out_specs=[pl.BlockSpec((D, tn), lambda j: (0, j)),
                   pl.BlockSpec((QK, D), lambda j: (0, 0))],
        out_shape=[jax.ShapeDtypeStruct((D, _WIN), BF16), jax.ShapeDtypeStruct((QK, D), BF16)],
        compiler_params=_cparams(1),
        name="prep",
    )(w_in, pm, pmT)


FC = 256


def _ffn_kernel(row, final, x_ref, sh_ref, sc_ref, g_ref, nw_ref, w13_ref, w2_ref, *rest):
    if final:
        nf_ref, o_ref, t_ref = rest
    else:
        o_ref, t_ref = rest
    x = x_ref[...]
    h = _norm_mod(x, nw_ref[...], sc_ref[row:row + 1, :], sh_ref[row:row + 1, :]).astype(BF16)
    for j in range(DFF // FC):
        a = _dot(h, w13_ref[:, j * FC:(j + 1) * FC])
        b = _dot(h, w13_ref[:, DFF + j * FC:DFF + (j + 1) * FC])
        t_ref[:, j * FC:(j + 1) * FC] = (a * jax.nn.sigmoid(a) * b).astype(BF16)
    y = _dot(t_ref[...], w2_ref[...])
    out = x + (0.5 * g_ref[row:row + 1, :]) * y
    if final:
        out = out * lax.rsqrt(jnp.mean(out * out, axis=-1, keepdims=True) + EPS) * nf_ref[...]
    o_ref[...] = out


def _ffn(x, mods, row, mod0, nw, w13, w2, nf=None, tm=512):
    n = x.shape[0]
    tm = min(tm, n)
    final = nf is not None
    in_specs = [
        pl.BlockSpec((tm, D), lambda i: (i, 0)),
        _mod_spec(mod0), _mod_spec(mod0 + 1), _mod_spec(mod0 + 2),
        _resident((1, D)),
        _resident((D, 2 * DFF)), _resident((DFF, D)),
    ]
    args = [x, mods, mods, mods, nw, w13, w2]
    if final:
        in_specs.append(_resident((1, D)))
        args.append(nf)
    return pl.pallas_call(
        functools.partial(_ffn_kernel, row, final),
        grid=(n // tm,),
        in_specs=in_specs,
        out_specs=pl.BlockSpec((tm, D), lambda i: (i, 0)),
        out_shape=jax.ShapeDtypeStruct((n, D), F32),
        scratch_shapes=[pltpu.VMEM((tm, DFF), BF16)],
        compiler_params=_cparams(1),
        name="ffn_final" if final else "ffn",
    )(*args)


def _ctx_state_kernel(lg_ref, x_ref, sh_ref, sc_ref, nw_ref, wkT_ref, wv_ref, sf_ref, sb_ref):
    hd = pl.program_id(0)
    h = _norm_mod(x_ref[...], nw_ref[...], sc_ref[1:2, :], sh_ref[1:2, :]).astype(BF16)
    kT = lax.dot_general(wkT_ref[...], h, (((1,), (1,)), ((), ())), preferred_element_type=F32)
    v = _dot(h, wv_ref[...]).astype(BF16)
    j = lax.broadcasted_iota(jnp.int32, (DK, CTX), 1).astype(F32)
    wf = jnp.exp(lg_ref[0, hd] * (CTX - 1.0 - j))
    wb = jnp.exp(lg_ref[1, hd] * j)
    sf_ref[0] = _dot((kT * wf).astype(BF16), v)
    sb_ref[0] = _dot((kT * wb).astype(BF16), v)


def _ctx_state(lg, ctx1, mods, nw, wkT, wall):
    return pl.pallas_call(
        _ctx_state_kernel,
        grid=(H,),
        in_specs=[
            pl.BlockSpec(memory_space=pltpu.SMEM),
            _resident((CTX, D)),
            _mod_spec(3), _mod_spec(4),
            _resident((1, D)),
            pl.BlockSpec((DK, D), lambda h: (h, 0)),
            pl.BlockSpec((D, DV), lambda h: (0, _OV // DV + h)),
        ],
        out_specs=[pl.BlockSpec((1, DK, DV), lambda h: (h, 0, 0))] * 2,
        out_shape=[jax.ShapeDtypeStruct((H, DK, DV), F32)] * 2,
        compiler_params=_cparams(1),
        name="ctx_state",
    )(lg, ctx1, mods, mods, nw, wkT, wall)


_OQ, _OK, _OV, _OG = 0, QK, 2 * QK, 2 * QK + VW
_OF, _OGA, _OGF = 2 * QK + 2 * VW, 2 * QK + 2 * VW + D, 2 * QK + 2 * VW + 2 * D
_WIN = 2 * QK + 2 * VW + 3 * D
HALF = DK // 2


def _proj_kernel(x_ref, sh_ref, sc_ref, nw_ref, w_ref, wkT_ref, md_ref, gnw_ref, cq_ref, sq_ref, ck_ref, sk_ref,
                 q_ref, kT_ref, v_ref, sg_ref, G_ref, sga_ref, sgf_ref):
    h = _norm_mod(x_ref[...], nw_ref[...], sc_ref[0:1, :], sh_ref[0:1, :]).astype(BF16)

    q = _dot(h, w_ref[:, _OQ:_OQ + QK])
    cq, sq = cq_ref[...], sq_ref[...]
    for hd in range(H):
        a1 = q[:, hd * DK:hd * DK + HALF]
        a2 = q[:, hd * DK + HALF:(hd + 1) * DK]
        q_ref[:, hd * DK:hd * DK + HALF] = (a1 * cq - a2 * sq).astype(BF16)
        q_ref[:, hd * DK + HALF:(hd + 1) * DK] = (a1 * sq + a2 * cq).astype(BF16)

    kT = lax.dot_general(wkT_ref[...], h, (((1,), (1,)), ((), ())), preferred_element_type=F32)
    ck, sk = ck_ref[...], sk_ref[...]
    for hd in range(H):
        a1 = kT[hd * DK:hd * DK + HALF, :]
        a2 = kT[hd * DK + HALF:(hd + 1) * DK, :]
        kT_ref[hd * DK:hd * DK + HALF, :] = (a1 * ck - a2 * sk).astype(BF16)
        kT_ref[hd * DK + HALF:(hd + 1) * DK, :] = (a1 * sk + a2 * ck).astype(BF16)

    v_ref[...] = _dot(h, w_ref[:, _OV:_OV + VW]).astype(BF16)
    g = _dot(h, w_ref[:, _OG:_OG + VW])
    sg_ref[...] = (g * jax.nn.sigmoid(g) * gnw_ref[...]).astype(BF16)

    f = _dot(h, w_ref[:, _OF:_OF + D]).astype(BF16)
    for gi in range(NG):
        z = _dot(f[:, gi * DG:(gi + 1) * DG], md_ref[...])
        G_ref[0, :, gi * DG:(gi + 1) * DG] = z[:, :DG].astype(BF16)
        G_ref[1, :, gi * DG:(gi + 1) * DG] = z[:, DG:].astype(BF16)

    sga_ref[...] = jax.nn.sigmoid(_dot(h, w_ref[:, _OGA:_OGA + D])).astype(BF16)
    sgf_ref[...] = jax.nn.sigmoid(_dot(h, w_ref[:, _OGF:_OGF + D])).astype(BF16)


def _proj(x1, mods, nw, wall, wkT, md, gnw, cq, sq, ck, sk, tm=256):
    row = lambda i: (i, 0)
    return pl.pallas_call(
        _proj_kernel,
        grid=(L // tm,),
        in_specs=[
            pl.BlockSpec((tm, D), row),
            _mod_spec(3), _mod_spec(4),
            _resident((1, D)),
            _resident((D, _WIN)),
            _resident((QK, D)),
            _resident((DG, 2 * DG)),
            _resident((1, VW)),
            pl.BlockSpec((tm, HALF), row), pl.BlockSpec((tm, HALF), row),
            pl.BlockSpec((HALF, tm), lambda i: (0, i)), pl.BlockSpec((HALF, tm), lambda i: (0, i)),
        ],
        out_specs=[
            pl.BlockSpec((tm, QK), row),
            pl.BlockSpec((QK, tm), lambda i: (0, i)),
            pl.BlockSpec((tm, VW), row),
            pl.BlockSpec((tm, VW), row),
            pl.BlockSpec((2, tm, D), lambda i: (0, i, 0)),
            pl.BlockSpec((tm, D), row),
            pl.BlockSpec((tm, D), row),
        ],
        out_shape=[
            jax.ShapeDtypeStruct((L, QK), BF16),
            jax.ShapeDtypeStruct((QK, L), BF16),
            jax.ShapeDtypeStruct((L, VW), BF16),
            jax.ShapeDtypeStruct((L, VW), BF16),
            jax.ShapeDtypeStruct((2, L, D), BF16),
            jax.ShapeDtypeStruct((L, D), BF16),
            jax.ShapeDtypeStruct((L, D), BF16),
        ],
        compiler_params=_cparams(1),
        name="proj",
    )(x1, mods, mods, nw, wall, wkT, md, gnw, cq, sq, ck, sk)


TB_B = 4096
TB_F = 2048
NCHUNK = L // C


def _ret_bwd_kernel(lg_ref, kT_ref, v_ref, s0_ref, sb_ref, S_ref, ZE_ref):
    lg = lg_ref[1, pl.program_id(0)]

    @pl.when(pl.program_id(1) == 0)
    def _():
        S_ref[...] = s0_ref[0]
        c = lax.broadcasted_iota(jnp.int32, (C, C), 1).astype(F32)
        ZE_ref[...] = jnp.exp(lg * c)

    g_chunk = jnp.exp(jnp.full((1, DV), lg * C, F32))
    for ci in range(TB_B // C - 1, -1, -1):
        rows = slice(ci * C, (ci + 1) * C)
        S = S_ref[...]
        sb_ref[0, ci] = S.astype(BF16)
        kz = (kT_ref[:, rows].astype(F32) * ZE_ref[...]).astype(BF16)
        S_ref[...] = S * g_chunk + _dot(kz, v_ref[rows, :])


def _ret_bwd(lg, kT, v, s0):
    nblk = L // TB_B
    nch = TB_B // C
    return pl.pallas_call(
        _ret_bwd_kernel,
        grid=(H, nblk),
        in_specs=[
            pl.BlockSpec(memory_space=pltpu.SMEM),
            pl.BlockSpec((DK, TB_B), lambda h, i: (h, nblk - 1 - i)),
            pl.BlockSpec((TB_B, DV), lambda h, i: (nblk - 1 - i, h)),
            pl.BlockSpec((1, DK, DV), lambda h, i: (h, 0, 0)),
        ],
        out_specs=pl.BlockSpec((1, nch, DK, DV), lambda h, i: (h, nblk - 1 - i, 0, 0)),
        out_shape=jax.ShapeDtypeStruct((H, NCHUNK, DK, DV), BF16),
        scratch_shapes=[pltpu.VMEM((DK, DV), F32), pltpu.VMEM((C, C), F32)],
        compiler_params=_cparams(2),
        name="ret_bwd",
    )(lg, kT, v, s0)


def _ret_fwd_kernel(lg_ref, q_ref, kT_ref, v_ref, s0_ref, sb_ref, sgw_ref, o_ref,
                    S_ref, D_ref, XF_ref, XB_ref, ZE_ref):
    hd = pl.program_id(0)
    lgf, lgb = lg_ref[0, hd], lg_ref[1, hd]

    @pl.when(pl.program_id(1) == 0)
    def _():
        S_ref[...] = s0_ref[0]
        r = lax.broadcasted_iota(jnp.int32, (C, C), 0).astype(F32)
        c = lax.broadcasted_iota(jnp.int32, (C, C), 1).astype(F32)
        d = r - c
        D_ref[...] = (jnp.where(d >= 0, jnp.exp(lgf * jnp.maximum(d, 0.0)), 0.0)
                      + jnp.where(d <= 0, jnp.exp(lgb * jnp.maximum(-d, 0.0)), 0.0))
        XF_ref[...] = jnp.exp(lgf * (r + 1.0))
        XB_ref[...] = jnp.exp(lgb * (C - r))
        ZE_ref[...] = jnp.exp(lgf * (C - 1.0 - c))

    g_chunk = jnp.exp(jnp.full((1, DV), lgf * C, F32))
    for ci in range(TB_F // C):
        rows = slice(ci * C, (ci + 1) * C)
        qc = q_ref[rows, :]
        kTc = kT_ref[:, rows]
        vc = v_ref[rows, :]
        S = S_ref[...]
        p = (_dot(qc, kTc) * D_ref[...]).astype(BF16)
        qf = (qc.astype(F32) * XF_ref[...]).astype(BF16)
        qb = (qc.astype(F32) * XB_ref[...]).astype(BF16)
        o = _dot(p, vc) + _dot(qf, S.astype(BF16)) + _dot(qb, sb_ref[0, ci])
        kz = (kTc.astype(F32) * ZE_ref[...]).astype(BF16)
        S_ref[...] = S * g_chunk + _dot(kz, vc)
        mu = jnp.mean(o, axis=-1, keepdims=True)
        d = o - mu
        var = jnp.mean(d * d, axis=-1, keepdims=True)
        on = d * lax.rsqrt(var + EPS)
        o_ref[rows, :] = (sgw_ref[rows, :].astype(F32) * on).astype(BF16)


def _ret_fwd(lg, q, kT, v, s0, sb, sgw):
    nch = TB_F // C
    tok = lambda h, i: (i, h)
    return pl.pallas_call(
        _ret_fwd_kernel,
        grid=(H, L // TB_F),
        in_specs=[
            pl.BlockSpec(memory_space=pltpu.SMEM),
            pl.BlockSpec((TB_F, DK), tok),
            pl.BlockSpec((DK, TB_F), lambda h, i: (h, i)),
            pl.BlockSpec((TB_F, DV), tok),
            pl.BlockSpec((1, DK, DV), lambda h, i: (h, 0, 0)),
            pl.BlockSpec((1, nch, DK, DV), lambda h, i: (h, i, 0, 0)),
            pl.BlockSpec((TB_F, DV), tok),
        ],
        out_specs=pl.BlockSpec((TB_F, DV), tok),
        out_shape=jax.ShapeDtypeStruct((L, VW), BF16),
        scratch_shapes=[pltpu.VMEM((DK, DV), F32)] + [pltpu.VMEM((C, C), F32)] * 4,
        compiler_params=_cparams(2),
        name="ret_fwd",
    )(lg, q, kT, v, s0, sb, sgw)


FFT_NT = 16
FFT_KB = 16


def _fft1_kernel(m_ref, g_ref, o_ref):
    z = jnp.transpose(g_ref[...], (2, 0, 1, 3)).reshape(FFT_NT, 2 * N1, D)
    for t in range(FFT_NT):
        o_ref[t] = _dot(m_ref[...], z[t]).astype(BF16)


def _fft1(m1, g4):
    return pl.pallas_call(
        _fft1_kernel,
        grid=(N2 // FFT_NT,),
        in_specs=[_resident((2 * N1, 2 * N1)),
                  pl.BlockSpec((2, N1, FFT_NT, D), lambda j: (0, 0, j, 0))],
        out_specs=pl.BlockSpec((FFT_NT, 2 * N1, D), lambda j: (j, 0, 0)),
        out_shape=jax.ShapeDtypeStruct((N2, 2 * N1, D), BF16),
        compiler_params=_cparams(1),
        name="fft1",
    )(m1, g4)


def _fft3_kernel(t_ref, pf_ref, a_ref, o_ref, xr_ref):
    z = jnp.transpose(a_ref[...], (2, 1, 0, 3)).reshape(FFT_KB, 2 * N2, D)
    for i in range(FFT_KB):
        xr_ref[i * N2:(i + 1) * N2, :] = _dot(t_ref[i], z[i]).astype(BF16)
    fo = _dot(xr_ref[...], pf_ref[...]).astype(BF16).reshape(FFT_KB, N2, D)
    o_ref[...] = jnp.transpose(fo, (1, 0, 2))


def _fft3(t3, a4, pf):
    return pl.pallas_call(
        _fft3_kernel,
        grid=(N1 // FFT_KB,),
        in_specs=[pl.BlockSpec((FFT_KB, N2, 2 * N2), lambda j: (j, 0, 0)),
                  _resident((D, D)),
                  pl.BlockSpec((N2, 2, FFT_KB, D), lambda j: (0, 0, j, 0))],
        out_specs=pl.BlockSpec((N2, FFT_KB, D), lambda j: (0, j, 0)),
        out_shape=jax.ShapeDtypeStruct((N2, N1, D), BF16),
        scratch_shapes=[pltpu.VMEM((FFT_KB * N2, D), BF16)],
        compiler_params=_cparams(1),
        name="fft3",
    )(t3, pf, a4)


def _merge_kernel(x_ref, g_ref, r_ref, fo_ref, sga_ref, sgf_ref, pr_ref, wo_ref, o_ref):
    r = _dot(r_ref[...], pr_ref[...])
    m = sga_ref[...].astype(F32) * r + sgf_ref[...].astype(F32) * fo_ref[...].astype(F32)
    y = _dot(m.astype(BF16), wo_ref[...])
    o_ref[...] = x_ref[...] + g_ref[0:1, :] * y


def _merge(x1, mods, r_in, fo, sga, sgf, p_ret, w_out, tm=512):
    row = lambda i: (i, 0)
    return pl.pallas_call(
        _merge_kernel,
        grid=(L // tm,),
        in_specs=[
            pl.BlockSpec((tm, D), row),
            _mod_spec(5),
            pl.BlockSpec((tm, VW), row),
            pl.BlockSpec((tm, D), row), pl.BlockSpec((tm, D), row), pl.BlockSpec((tm, D), row),
            _resident((VW, D)), _resident((D, D)),
        ],
        out_specs=pl.BlockSpec((tm, D), row),
        out_shape=jax.ShapeDtypeStruct((L, D), F32),
        compiler_params=_cparams(1),
        name="merge",
    )(x1, mods, r_in, fo, sga, sgf, p_ret, w_out)


def _rope_tables():
    rows = L // GRID_W
    row = np.repeat(np.arange(rows, dtype=np.float64), GRID_W)
    col = np.tile(np.arange(GRID_W, dtype=np.float64), rows)
    n_freq = DK // 4
    inv = ROPE_BASE ** (-np.arange(n_freq, dtype=np.float64) / n_freq)
    ang = np.concatenate([row[:, None] * inv, col[:, None] * inv], axis=-1)
    return np.cos(ang), np.sin(ang)


def _dft_tables():
    two_pi = 2.0 * np.pi
    th = (np.outer(np.arange(DG), np.arange(DG)) % DG) * (two_pi / DG)
    md = np.concatenate([np.cos(th), -np.sin(th)], axis=1)
    th = (np.outer(np.arange(N1), np.arange(N1)) % N1) * (two_pi / N1)
    c1, s1 = np.cos(th), np.sin(th)
    m1 = np.block([[c1, s1], [-s1, c1]])
    k = np.arange(N1)[:, None, None] + N1 * np.arange(N2)[None, :, None]
    th = ((k * np.arange(N2)[None, None, :]) % L) * (two_pi / L)
    t3 = np.concatenate([np.cos(th), np.sin(th)], axis=-1) / np.sqrt(float(L) * DG)
    f32 = lambda a: jnp.asarray(a.astype(np.float32))
    return f32(md).astype(BF16), f32(m1).astype(BF16), f32(t3).astype(BF16)


def kernel(x, c, ctx, c_ctx, w_ada, b_ada, norm_ffn1, w13_ffn1, w2_ffn1, norm_mix, w_in, decay_fwd, decay_bwd,
           ret_gn_w, p_ret, p_four, w_out, norm_ffn2, w13_ffn2, w2_ffn2, norm_final):
    assert x.shape == (1, L, D) and ctx.shape == (1, CTX, D) and w_ada.shape[0] == 1

    pm = np.zeros((DK, DK), np.float32)
    pm[np.concatenate([np.arange(0, DK, 2), np.arange(1, DK, 2)]), np.arange(DK)] = 1.0
    wall, wkT = _prep(w_in[0], jnp.asarray(pm).astype(BF16), jnp.asarray(np.ascontiguousarray(pm.T)).astype(BF16))
    w13a, w13b = w13_ffn1[0].astype(BF16), w13_ffn2[0].astype(BF16)
    w2a, w2b = w2_ffn1[0].astype(BF16), w2_ffn2[0].astype(BF16)
    pr, pf, wo = p_ret[0].astype(BF16), p_four[0].astype(BF16), w_out[0].astype(BF16)

    lg = jnp.stack([jax.nn.log_sigmoid(decay_fwd[0].astype(F32)),
                    jax.nn.log_sigmoid(decay_bwd[0].astype(F32))])
    cos, sin = _rope_tables()
    qscale = DK ** -0.5
    f32c = lambda a: jnp.asarray(np.ascontiguousarray(a).astype(np.float32))
    cq, sq, ck, sk = f32c(cos * qscale), f32c(sin * qscale), f32c(cos.T), f32c(sin.T)
    md, m1, t3 = _dft_tables()

    cs = jnp.concatenate([c, c_ctx[None, :], jnp.zeros((6, D), F32)], axis=0).T
    mods = _mods(cs, w_ada[0], b_ada)

    x0 = x[0]
    x1 = _ffn(x0, mods, 0, 0, norm_ffn1, w13a, w2a)
    ctx1 = _ffn(ctx[0], mods, 1, 0, norm_ffn1, w13a, w2a)

    s_f, s_b = _ctx_state(lg, ctx1, mods, norm_mix, wkT, wall)

    q, kT, v, sgw, G, sga, sgf = _proj(x1, mods, norm_mix, wall, wkT, md, ret_gn_w, cq, sq, ck, sk)

    sb = _ret_bwd(lg, kT, v, s_b)
    r_in = _ret_fwd(lg, q, kT, v, s_f, sb, sgw)

    a = _fft1(m1, G.reshape(2, N1, N2, D))
    fo = _fft3(t3, a.reshape(N2, 2, N1, D), pf).reshape(L, D)

    x2 = _merge(x1, mods, r_in, fo, sga, sgf, pr, wo)
    out = _ffn(x2, mods, 0, 6, norm_ffn2, w13b, w2b, nf=norm_final[None, :])
    return out[None]
```

```python
import functools

import numpy as np
import jax
import jax.numpy as jnp
from jax import lax
from jax.experimental import pallas as pl
from jax.experimental.pallas import tpu as pltpu

D = 1024
L = 16384
GRID_W = 64
CTX = 256
H = 4
DK = 256
DV = 512
QK = H * DK
VW = H * DV
NG = 4
DG = 256
DFF = 2816
ROPE_BASE = 10000.0
EPS = 1e-6
NMOD = 9

F32 = jnp.float32
BF16 = jnp.bfloat16

C = 256
N1 = 128
N2 = 128

VMEM_LIMIT = 56 * 1024 * 1024


def _dot(a, b):
    return jnp.dot(a, b, preferred_element_type=F32)


def _cparams(n_axes):
    return pltpu.CompilerParams(
        dimension_semantics=("arbitrary",) * n_axes, vmem_limit_bytes=VMEM_LIMIT)


def _resident(shape):
    nd = len(shape)
    return pl.BlockSpec(shape, lambda *_: (0,) * nd, pipeline_mode=pl.Buffered(1))


def _mod_spec(j):
    return pl.BlockSpec((8, D), lambda *_: (0, j), pipeline_mode=pl.Buffered(1))


def _norm_mod(x, nw, sc, sh):
    y = x * lax.rsqrt(jnp.mean(x * x, axis=-1, keepdims=True) + EPS)
    return y * (nw * (1.0 + sc)) + sh


def _mods_kernel(cs_ref, w_ref, b_ref, o_ref):
    cs = cs_ref[...]
    s = cs * jax.nn.sigmoid(cs)
    w = w_ref[...]
    rows = [jnp.sum(w * s[:, r:r + 1], axis=0, keepdims=True) for r in range(2)]
    o_ref[...] = jnp.concatenate(rows + [jnp.zeros((6, w.shape[1]), F32)], axis=0) + b_ref[...]


def _mods(cs, w_ada, b_ada):
    tn = 1024
    return pl.pallas_call(
        _mods_kernel,
        grid=(NMOD * D // tn,),
        in_specs=[
            _resident((D, 8)),
            pl.BlockSpec((D, tn), lambda j: (0, j)),
            pl.BlockSpec((1, tn), lambda j: (0, j)),
        ],
        out_specs=pl.BlockSpec((8, tn), lambda j: (0, j)),
        out_shape=jax.ShapeDtypeStruct((8, NMOD * D), F32),
        compiler_params=_cparams(1),
        name="mods",
    )(cs, w_ada, b_ada)


def _prep_kernel(w_ref, p_ref, pT_ref, o_ref, kT_ref):
    j = pl.program_id(0)
    w = w_ref[...].astype(BF16)

    @pl.when(j >= 2)
    def _():
        o_ref[...] = w

    @pl.when(j < 2)
    def _():
        for hd in range(H):
            o_ref[:, hd * DK:(hd + 1) * DK] = _dot(w[:, hd * DK:(hd + 1) * DK], p_ref[...]).astype(BF16)

    @pl.when(j == 1)
    def _():
        for hd in range(H):
            kT_ref[hd * DK:(hd + 1) * DK, :] = lax.dot_general(
                pT_ref[...], w[:, hd * DK:(hd + 1) * DK], (((1,), (1,)), ((), ())),
                preferred_element_type=F32).astype(BF16)


def _prep(w_in, pm, pmT):
    tn = QK
    return pl.pallas_call(
        _prep_kernel,
        grid=(_WIN // tn,),
        in_specs=[pl.BlockSpec((D, tn), lambda j: (0, j)), _resident((DK, DK)), _resident((DK, DK))],
        out_specs=[pl.BlockSpec((D, tn), lambda j: (0, j)),
                   pl.BlockSpec((QK, D), lambda j: (0, 0))],
        out_shape=[jax.ShapeDtypeStruct((D, _WIN), BF16), jax.ShapeDtypeStruct((QK, D), BF16)],
        compiler_params=_cparams(1),
        name="prep",
    )(w_in, pm, pmT)


FC = 256


def _ffn_kernel(row, final, x_ref, sh_ref, sc_ref, g_ref, nw_ref, w13_ref, w2_ref, *rest):
    if final:
        nf_ref, o_ref, t_ref = rest
    else:
        o_ref, t_ref = rest
    x = x_ref[...]
    h = _norm_mod(x, nw_ref[...], sc_ref[row:row + 1, :], sh_ref[row:row + 1, :]).astype(BF16)
    for j in range(DFF // FC):
        a = _dot(h, w13_ref[:, j * FC:(j + 1) * FC])
        b = _dot(h, w13_ref[:, DFF + j * FC:DFF + (j + 1) * FC])
        t_ref[:, j * FC:(j + 1) * FC] = (a * jax.nn.sigmoid(a) * b).astype(BF16)
    y = _dot(t_ref[...], w2_ref[...])
    out = x + (0.5 * g_ref[row:row + 1, :]) * y
    if final:
        out = out * lax.rsqrt(jnp.mean(out * out, axis=-1, keepdims=True) + EPS) * nf_ref[...]
    o_ref[...] = out


def _ffn(x, mods, row, mod0, nw, w13, w2, nf=None, tm=1024):
    n = x.shape[0]
    tm = min(tm, n)
    final = nf is not None
    in_specs = [
        pl.BlockSpec((tm, D), lambda i: (i, 0)),
        _mod_spec(mod0), _mod_spec(mod0 + 1), _mod_spec(mod0 + 2),
        _resident((1, D)),
        _resident((D, 2 * DFF)), _resident((DFF, D)),
    ]
    args = [x, mods, mods, mods, nw, w13, w2]
    if final:
        in_specs.append(_resident((1, D)))
        args.append(nf)
    return pl.pallas_call(
        functools.partial(_ffn_kernel, row, final),
        grid=(n // tm,),
        in_specs=in_specs,
        out_specs=pl.BlockSpec((tm, D), lambda i: (i, 0)),
        out_shape=jax.ShapeDtypeStruct((n, D), F32),
        scratch_shapes=[pltpu.VMEM((tm, DFF), BF16)],
        compiler_params=_cparams(1),
        name="ffn_final" if final else "ffn",
    )(*args)


def _ctx_state_kernel(lg_ref, x_ref, sh_ref, sc_ref, nw_ref, wkT_ref, wv_ref, sf_ref, sb_ref):
    hd = pl.program_id(0)
    h = _norm_mod(x_ref[...], nw_ref[...], sc_ref[1:2, :], sh_ref[1:2, :]).astype(BF16)
    kT = lax.dot_general(wkT_ref[...], h, (((1,), (1,)), ((), ())), preferred_element_type=F32)
    v = _dot(h, wv_ref[...]).astype(BF16)
    j = lax.broadcasted_iota(jnp.int32, (DK, CTX), 1).astype(F32)
    wf = jnp.exp(lg_ref[0, hd] * (CTX - 1.0 - j))
    wb = jnp.exp(lg_ref[1, hd] * j)
    sf_ref[0] = _dot((kT * wf).astype(BF16), v)
    sb_ref[0] = _dot((kT * wb).astype(BF16), v)


def _ctx_state(lg, ctx1, mods, nw, wkT, wall):
    return pl.pallas_call(
        _ctx_state_kernel,
        grid=(H,),
        in_specs=[
            pl.BlockSpec(memory_space=pltpu.SMEM),
            _resident((CTX, D)),
            _mod_spec(3), _mod_spec(4),
            _resident((1, D)),
            pl.BlockSpec((DK, D), lambda h: (h, 0)),
            pl.BlockSpec((D, DV), lambda h: (0, _OV // DV + h)),
        ],
        out_specs=[pl.BlockSpec((1, DK, DV), lambda h: (h, 0, 0))] * 2,
        out_shape=[jax.ShapeDtypeStruct((H, DK, DV), F32)] * 2,
        compiler_params=_cparams(1),
        name="ctx_state",
    )(lg, ctx1, mods, mods, nw, wkT, wall)


_OQ, _OK, _OV, _OG = 0, QK, 2 * QK, 2 * QK + VW
_OF, _OGA, _OGF = 2 * QK + 2 * VW, 2 * QK + 2 * VW + D, 2 * QK + 2 * VW + 2 * D
_WIN = 2 * QK + 2 * VW + 3 * D
HALF = DK // 2


def _proj_kernel(lg_ref, x_ref, sh_ref, sc_ref, nw_ref, w_ref, wkT_ref, md_ref, gnw_ref, cq_ref, sq_ref, ck_ref,
                 sk_ref, s0_ref, q_ref, kT_ref, v_ref, sg_ref, G_ref, sga_ref, sgf_ref, sb_ref, S_ref, ZE_ref):
    @pl.when(pl.program_id(0) == 0)
    def _():
        S_ref[...] = s0_ref[...]
        c = lax.broadcasted_iota(jnp.int32, (DK, C), 1).astype(F32)
        for hd in range(H):
            ZE_ref[hd] = jnp.exp(lg_ref[1, hd] * c)

    h = _norm_mod(x_ref[...], nw_ref[...], sc_ref[0:1, :], sh_ref[0:1, :]).astype(BF16)

    q = _dot(h, w_ref[:, _OQ:_OQ + QK])
    cq, sq = cq_ref[...], sq_ref[...]
    for hd in range(H):
        a1 = q[:, hd * DK:hd * DK + HALF]
        a2 = q[:, hd * DK + HALF:(hd + 1) * DK]
        q_ref[:, hd * DK:hd * DK + HALF] = (a1 * cq - a2 * sq).astype(BF16)
        q_ref[:, hd * DK + HALF:(hd + 1) * DK] = (a1 * sq + a2 * cq).astype(BF16)

    kT = lax.dot_general(wkT_ref[...], h, (((1,), (1,)), ((), ())), preferred_element_type=F32)
    ck, sk = ck_ref[...], sk_ref[...]
    for hd in range(H):
        a1 = kT[hd * DK:hd * DK + HALF, :]
        a2 = kT[hd * DK + HALF:(hd + 1) * DK, :]
        kT_ref[hd * DK:hd * DK + HALF, :] = (a1 * ck - a2 * sk).astype(BF16)
        kT_ref[hd * DK + HALF:(hd + 1) * DK, :] = (a1 * sk + a2 * ck).astype(BF16)

    v = _dot(h, w_ref[:, _OV:_OV + VW]).astype(BF16)
    v_ref[...] = v
    for hd in range(H):
        S = S_ref[hd]
        sb_ref[hd, 0] = S.astype(BF16)
        kz = (kT_ref[hd * DK:(hd + 1) * DK, :].astype(F32) * ZE_ref[hd]).astype(BF16)
        g_chunk = jnp.exp(jnp.full((1, DV), lg_ref[1, hd] * C, F32))
        S_ref[hd] = S * g_chunk + _dot(kz, v[:, hd * DV:(hd + 1) * DV])

    g = _dot(h, w_ref[:, _OG:_OG + VW])
    sg_ref[...] = (g * jax.nn.sigmoid(g) * gnw_ref[...]).astype(BF16)

    f = _dot(h, w_ref[:, _OF:_OF + D]).astype(BF16)
    for gi in range(NG):
        z = _dot(f[:, gi * DG:(gi + 1) * DG], md_ref[...])
        G_ref[0, :, gi * DG:(gi + 1) * DG] = z[:, :DG].astype(BF16)
        G_ref[1, :, gi * DG:(gi + 1) * DG] = z[:, DG:].astype(BF16)

    sga_ref[...] = jax.nn.sigmoid(_dot(h, w_ref[:, _OGA:_OGA + D])).astype(BF16)
    sgf_ref[...] = jax.nn.sigmoid(_dot(h, w_ref[:, _OGF:_OGF + D])).astype(BF16)


def _proj(lg, x1, mods, nw, wall, wkT, md, gnw, cq, sq, ck, sk, s_b0):
    tm = C
    n = L // tm
    row = lambda i: (n - 1 - i, 0)
    col = lambda i: (0, n - 1 - i)
    return pl.pallas_call(
        _proj_kernel,
        grid=(n,),
        in_specs=[
            pl.BlockSpec(memory_space=pltpu.SMEM),
            pl.BlockSpec((tm, D), row),
            _mod_spec(3), _mod_spec(4),
            _resident((1, D)),
            _resident((D, _WIN)),
            _resident((QK, D)),
            _resident((DG, 2 * DG)),
            _resident((1, VW)),
            pl.BlockSpec((tm, HALF), row), pl.BlockSpec((tm, HALF), row),
            pl.BlockSpec((HALF, tm), col), pl.BlockSpec((HALF, tm), col),
            _resident((H, DK, DV)),
        ],
        out_specs=[
            pl.BlockSpec((tm, QK), row),
            pl.BlockSpec((QK, tm), col),
            pl.BlockSpec((tm, VW), row),
            pl.BlockSpec((tm, VW), row),
            pl.BlockSpec((2, tm, D), lambda i: (0, n - 1 - i, 0)),
            pl.BlockSpec((tm, D), row),
            pl.BlockSpec((tm, D), row),
            pl.BlockSpec((H, 1, DK, DV), lambda i: (0, n - 1 - i, 0, 0)),
        ],
        out_shape=[
            jax.ShapeDtypeStruct((L, QK), BF16),
            jax.ShapeDtypeStruct((QK, L), BF16),
            jax.ShapeDtypeStruct((L, VW), BF16),
            jax.ShapeDtypeStruct((L, VW), BF16),
            jax.ShapeDtypeStruct((2, L, D), BF16),
            jax.ShapeDtypeStruct((L, D), BF16),
            jax.ShapeDtypeStruct((L, D), BF16),
            jax.ShapeDtypeStruct((H, L // C, DK, DV), BF16),
        ],
        scratch_shapes=[pltpu.VMEM((H, DK, DV), F32), pltpu.VMEM((H, DK, C), F32)],
        compiler_params=_cparams(1),
        name="proj",
    )(lg, x1, mods, mods, nw, wall, wkT, md, gnw, cq, sq, ck, sk, s_b0)


TB_F = 4096


def _ret_fwd_kernel(lg_ref, q_ref, kT_ref, v_ref, s0_ref, sb_ref, sgw_ref, o_ref,
                    S_ref, D_ref, XF_ref, XB_ref, ZE_ref):
    hd = pl.program_id(0)
    lgf, lgb = lg_ref[0, hd], lg_ref[1, hd]

    @pl.when(pl.program_id(1) == 0)
    def _():
        S_ref[...] = s0_ref[0]
        r = lax.broadcasted_iota(jnp.int32, (C, C), 0).astype(F32)
        c = lax.broadcasted_iota(jnp.int32, (C, C), 1).astype(F32)
        d = r - c
        D_ref[...] = (jnp.where(d >= 0, jnp.exp(lgf * jnp.maximum(d, 0.0)), 0.0)
                      + jnp.where(d <= 0, jnp.exp(lgb * jnp.maximum(-d, 0.0)), 0.0))
        XF_ref[...] = jnp.exp(lgf * (r + 1.0))
        XB_ref[...] = jnp.exp(lgb * (C - r))
        ZE_ref[...] = jnp.exp(lgf * (C - 1.0 - c))

    g_chunk = jnp.exp(jnp.full((1, DV), lgf * C, F32))
    for ci in range(TB_F // C):
        rows = slice(ci * C, (ci + 1) * C)
        qc = q_ref[rows, :]
        kTc = kT_ref[:, rows]
        vc = v_ref[rows, :]
        S = S_ref[...]
        p = (_dot(qc, kTc) * D_ref[...]).astype(BF16)
        qf = (qc.astype(F32) * XF_ref[...]).astype(BF16)
        qb = (qc.astype(F32) * XB_ref[...]).astype(BF16)
        o = _dot(p, vc) + _dot(qf, S.astype(BF16)) + _dot(qb, sb_ref[0, ci])
        kz = (kTc.astype(F32) * ZE_ref[...]).astype(BF16)
        S_ref[...] = S * g_chunk + _dot(kz, vc)
        mu = jnp.mean(o, axis=-1, keepdims=True)
        d = o - mu
        var = jnp.mean(d * d, axis=-1, keepdims=True)
        on = d * lax.rsqrt(var + EPS)
        o_ref[rows, :] = (sgw_ref[rows, :].astype(F32) * on).astype(BF16)


def _ret_fwd(lg, q, kT, v, s0, sb, sgw):
    nch = TB_F // C
    tok = lambda h, i: (i, h)
    return pl.pallas_call(
        _ret_fwd_kernel,
        grid=(H, L // TB_F),
        in_specs=[
            pl.BlockSpec(memory_space=pltpu.SMEM),
            pl.BlockSpec((TB_F, DK), tok),
            pl.BlockSpec((DK, TB_F), lambda h, i: (h, i)),
            pl.BlockSpec((TB_F, DV), tok),
            pl.BlockSpec((1, DK, DV), lambda h, i: (h, 0, 0)),
            pl.BlockSpec((1, nch, DK, DV), lambda h, i: (h, i, 0, 0)),
            pl.BlockSpec((TB_F, DV), tok),
        ],
        out_specs=pl.BlockSpec((TB_F, DV), tok),
        out_shape=jax.ShapeDtypeStruct((L, VW), BF16),
        scratch_shapes=[pltpu.VMEM((DK, DV), F32)] + [pltpu.VMEM((C, C), F32)] * 4,
        compiler_params=_cparams(2),
        name="ret_fwd",
    )(lg, q, kT, v, s0, sb, sgw)


FFT_NT = 16
FFT_KB = 16


def _fft1_kernel(m_ref, g_ref, o_ref):
    z = jnp.transpose(g_ref[...], (2, 0, 1, 3)).reshape(FFT_NT, 2 * N1, D)
    for t in range(FFT_NT):
        o_ref[t] = _dot(m_ref[...], z[t]).astype(BF16)


def _fft1(m1, g4):
    return pl.pallas_call(
        _fft1_kernel,
        grid=(N2 // FFT_NT,),
        in_specs=[_resident((2 * N1, 2 * N1)),
                  pl.BlockSpec((2, N1, FFT_NT, D), lambda j: (0, 0, j, 0))],
        out_specs=pl.BlockSpec((FFT_NT, 2 * N1, D), lambda j: (j, 0, 0)),
        out_shape=jax.ShapeDtypeStruct((N2, 2 * N1, D), BF16),
        compiler_params=_cparams(1),
        name="fft1",
    )(m1, g4)


def _fft3_kernel(t_ref, pf_ref, a_ref, o_ref, xr_ref):
    z = jnp.transpose(a_ref[...], (2, 1, 0, 3)).reshape(FFT_KB, 2 * N2, D)
    for i in range(FFT_KB):
        xr_ref[i * N2:(i + 1) * N2, :] = _dot(t_ref[i], z[i]).astype(BF16)
    fo = _dot(xr_ref[...], pf_ref[...]).astype(BF16).reshape(FFT_KB, N2, D)
    o_ref[...] = jnp.transpose(fo, (1, 0, 2))


def _fft3(t3, a4, pf):
    return pl.pallas_call(
        _fft3_kernel,
        grid=(N1 // FFT_KB,),
        in_specs=[pl.BlockSpec((FFT_KB, N2, 2 * N2), lambda j: (j, 0, 0)),
                  _resident((D, D)),
                  pl.BlockSpec((N2, 2, FFT_KB, D), lambda j: (0, 0, j, 0))],
        out_specs=pl.BlockSpec((N2, FFT_KB, D), lambda j: (0, j, 0)),
        out_shape=jax.ShapeDtypeStruct((N2, N1, D), BF16),
        scratch_shapes=[pltpu.VMEM((FFT_KB * N2, D), BF16)],
        compiler_params=_cparams(1),
        name="fft3",
    )(t3, pf, a4)


def _merge_kernel(x_ref, g_ref, r_ref, fo_ref, sga_ref, sgf_ref, pr_ref, wo_ref, o_ref):
    r = _dot(r_ref[...], pr_ref[...])
    m = sga_ref[...].astype(F32) * r + sgf_ref[...].astype(F32) * fo_ref[...].astype(F32)
    y = _dot(m.astype(BF16), wo_ref[...])
    o_ref[...] = x_ref[...] + g_ref[0:1, :] * y


def _merge(x1, mods, r_in, fo, sga, sgf, p_ret, w_out, tm=1024):
    row = lambda i: (i, 0)
    return pl.pallas_call(
        _merge_kernel,
        grid=(L // tm,),
        in_specs=[
            pl.BlockSpec((tm, D), row),
            _mod_spec(5),
            pl.BlockSpec((tm, VW), row),
            pl.BlockSpec((tm, D), row), pl.BlockSpec((tm, D), row), pl.BlockSpec((tm, D), row),
            _resident((VW, D)), _resident((D, D)),
        ],
        out_specs=pl.BlockSpec((tm, D), row),
        out_shape=jax.ShapeDtypeStruct((L, D), F32),
        compiler_params=_cparams(1),
        name="merge",
    )(x1, mods, r_in, fo, sga, sgf, p_ret, w_out)


def _rope_tables():
    rows = L // GRID_W
    row = np.repeat(np.arange(rows, dtype=np.float64), GRID_W)
    col = np.tile(np.arange(GRID_W, dtype=np.float64), rows)
    n_freq = DK // 4
    inv = ROPE_BASE ** (-np.arange(n_freq, dtype=np.float64) / n_freq)
    ang = np.concatenate([row[:, None] * inv, col[:, None] * inv], axis=-1)
    return np.cos(ang), np.sin(ang)


def _dft_tables():
    two_pi = 2.0 * np.pi
    th = (np.outer(np.arange(DG), np.arange(DG)) % DG) * (two_pi / DG)
    md = np.concatenate([np.cos(th), -np.sin(th)], axis=1)
    th = (np.outer(np.arange(N1), np.arange(N1)) % N1) * (two_pi / N1)
    c1, s1 = np.cos(th), np.sin(th)
    m1 = np.block([[c1, s1], [-s1, c1]])
    k = np.arange(N1)[:, None, None] + N1 * np.arange(N2)[None, :, None]
    th = ((k * np.arange(N2)[None, None, :]) % L) * (two_pi / L)
    t3 = np.concatenate([np.cos(th), np.sin(th)], axis=-1) / np.sqrt(float(L) * DG)
    f32 = lambda a: jnp.asarray(a.astype(np.float32))
    return f32(md).astype(BF16), f32(m1).astype(BF16), f32(t3).astype(BF16)


def kernel(x, c, ctx, c_ctx, w_ada, b_ada, norm_ffn1, w13_ffn1, w2_ffn1, norm_mix, w_in, decay_fwd, decay_bwd,
           ret_gn_w, p_ret, p_four, w_out, norm_ffn2, w13_ffn2, w2_ffn2, norm_final):
    assert x.shape == (1, L, D) and ctx.shape == (1, CTX, D) and w_ada.shape[0] == 1

    pm = np.zeros((DK, DK), np.float32)
    pm[np.concatenate([np.arange(0, DK, 2), np.arange(1, DK, 2)]), np.arange(DK)] = 1.0
    wall, wkT = _prep(w_in[0], jnp.asarray(pm).astype(BF16), jnp.asarray(np.ascontiguousarray(pm.T)).astype(BF16))
    w13a, w13b = w13_ffn1[0].astype(BF16), w13_ffn2[0].astype(BF16)
    w2a, w2b = w2_ffn1[0].astype(BF16), w2_ffn2[0].astype(BF16)
    pr, pf, wo = p_ret[0].astype(BF16), p_four[0].astype(BF16), w_out[0].astype(BF16)

    lg = jnp.stack([jax.nn.log_sigmoid(decay_fwd[0].astype(F32)),
                    jax.nn.log_sigmoid(decay_bwd[0].astype(F32))])
    cos, sin = _rope_tables()
    qscale = DK ** -0.5
    f32c = lambda a: jnp.asarray(np.ascontiguousarray(a).astype(np.float32))
    cq, sq, ck, sk = f32c(cos * qscale), f32c(sin * qscale), f32c(cos.T), f32c(sin.T)
    md, m1, t3 = _dft_tables()

    cs = jnp.concatenate([c, c_ctx[None, :], jnp.zeros((6, D), F32)], axis=0).T
    mods = _mods(cs, w_ada[0], b_ada)

    x0 = x[0]
    x1 = _ffn(x0, mods, 0, 0, norm_ffn1, w13a, w2a)
    ctx1 = _ffn(ctx[0], mods, 1, 0, norm_ffn1, w13a, w2a)

    s_f, s_b = _ctx_state(lg, ctx1, mods, norm_mix, wkT, wall)

    q, kT, v, sgw, G, sga, sgf, sb = _proj(lg, x1, mods, norm_mix, wall, wkT, md, ret_gn_w, cq, sq, ck, sk, s_b)
    r_in = _ret_fwd(lg, q, kT, v, s_f, sb, sgw)

    a = _fft1(m1, G.reshape(2, N1, N2, D))
    fo = _fft3(t3, a.reshape(N2, 2, N1, D), pf).reshape(L, D)

    x2 = _merge(x1, mods, r_in, fo, sga, sgf, pr, wo)
    out = _ffn(x2, mods, 0, 6, norm_ffn2, w13b, w2b, nf=norm_final[None, :])
    return out[None]
```

```python
import functools

import numpy as np
import jax
import jax.numpy as jnp
from jax import lax
from jax.experimental import pallas as pl
from jax.experimental.pallas import tpu as pltpu

D = 1024
L = 16384
GRID_W = 64
CTX = 256
H = 4
DK = 256
DV = 512
QK = H * DK
VW = H * DV
NG = 4
DG = 256
DFF = 2816
ROPE_BASE = 10000.0
EPS = 1e-6
NMOD = 9

F32 = jnp.float32
BF16 = jnp.bfloat16

C = 256
N1 = 128
N2 = 128

VMEM_LIMIT = 56 * 1024 * 1024


def _dot(a, b):
    return jnp.dot(a, b, preferred_element_type=F32)


def _cparams(n_axes):
    return pltpu.CompilerParams(
        dimension_semantics=("arbitrary",) * n_axes, vmem_limit_bytes=VMEM_LIMIT)


def _resident(shape):
    nd = len(shape)
    return pl.BlockSpec(shape, lambda *_: (0,) * nd, pipeline_mode=pl.Buffered(1))


def _mod_spec(j):
    return pl.BlockSpec((8, D), lambda *_: (0, j), pipeline_mode=pl.Buffered(1))


def _norm_mod(x, nw, sc, sh):
    y = x * lax.rsqrt(jnp.mean(x * x, axis=-1, keepdims=True) + EPS)
    return y * (nw * (1.0 + sc)) + sh


def _mods_kernel(cs_ref, w_ref, b_ref, o_ref):
    cs = cs_ref[...]
    s = cs * jax.nn.sigmoid(cs)
    w = w_ref[...]
    rows = [jnp.sum(w * s[:, r:r + 1], axis=0, keepdims=True) for r in range(2)]
    o_ref[...] = jnp.concatenate(rows + [jnp.zeros((6, w.shape[1]), F32)], axis=0) + b_ref[...]


def _mods(cs, w_ada, b_ada):
    tn = 1024
    return pl.pallas_call(
        _mods_kernel,
        grid=(NMOD * D // tn,),
        in_specs=[
            _resident((D, 8)),
            pl.BlockSpec((D, tn), lambda j: (0, j)),
            pl.BlockSpec((1, tn), lambda j: (0, j)),
        ],
        out_specs=pl.BlockSpec((8, tn), lambda j: (0, j)),
        out_shape=jax.ShapeDtypeStruct((8, NMOD * D), F32),
        compiler_params=_cparams(1),
        name="mods",
    )(cs, w_ada, b_ada)


def _prep_kernel(w_ref, p_ref, pT_ref, o_ref, kT_ref):
    j = pl.program_id(0)
    w = w_ref[...].astype(BF16)

    @pl.when(j >= 2)
    def _():
        o_ref[...] = w

    @pl.when(j < 2)
    def _():
        for hd in range(H):
            o_ref[:, hd * DK:(hd + 1) * DK] = _dot(w[:, hd * DK:(hd + 1) * DK], p_ref[...]).astype(BF16)

    @pl.when(j == 1)
    def _():
        for hd in range(H):
            kT_ref[hd * DK:(hd + 1) * DK, :] = lax.dot_general(
                pT_ref[...], w[:, hd * DK:(hd + 1) * DK], (((1,), (1,)), ((), ())),
                preferred_element_type=F32).astype(BF16)


def _prep(w_in, pm, pmT):
    tn = QK
    return pl.pallas_call(
        _prep_kernel,
        grid=(_WIN // tn,),
        in_specs=[pl.BlockSpec((D, tn), lambda j: (0, j)), _resident((DK, DK)), _resident((DK, DK))],
        out_specs=[pl.BlockSpec((D, tn), lambda j: (0, j)),
                   pl.BlockSpec((QK, D), lambda j: (0, 0))],
        out_shape=[jax.ShapeDtypeStruct((D, _WIN), BF16), jax.ShapeDtypeStruct((QK, D), BF16)],
        compiler_params=_cparams(1),
        name="prep",
    )(w_in, pm, pmT)


FC = 256


def _ffn_kernel(row, final, x_ref, sh_ref, sc_ref, g_ref, nw_ref, w13_ref, w2_ref, *rest):
    if final:
        nf_ref, o_ref, t_ref = rest
    else:
        o_ref, t_ref = rest
    x = x_ref[...]
    h = _norm_mod(x, nw_ref[...], sc_ref[row:row + 1, :], sh_ref[row:row + 1, :]).astype(BF16)
    for j in range(DFF // FC):
        a = _dot(h, w13_ref[:, j * FC:(j + 1) * FC])
        b = _dot(h, w13_ref[:, DFF + j * FC:DFF + (j + 1) * FC])
        t_ref[:, j * FC:(j + 1) * FC] = (a * jax.nn.sigmoid(a) * b).astype(BF16)
    y = _dot(t_ref[...], w2_ref[...])
    out = x + (0.5 * g_ref[row:row + 1, :]) * y
    if final:
        out = out * lax.rsqrt(jnp.mean(out * out, axis=-1, keepdims=True) + EPS) * nf_ref[...]
    o_ref[...] = out


def _ffn(x, mods, row, mod0, nw, w13, w2, nf=None, tm=1024):
    n = x.shape[0]
    tm = min(tm, n)
    final = nf is not None
    in_specs = [
        pl.BlockSpec((tm, D), lambda i: (i, 0)),
        _mod_spec(mod0), _mod_spec(mod0 + 1), _mod_spec(mod0 + 2),
        _resident((1, D)),
        _resident((D, 2 * DFF)), _resident((DFF, D)),
    ]
    args = [x, mods, mods, mods, nw, w13, w2]
    if final:
        in_specs.append(_resident((1, D)))
        args.append(nf)
    return pl.pallas_call(
        functools.partial(_ffn_kernel, row, final),
        grid=(n // tm,),
        in_specs=in_specs,
        out_specs=pl.BlockSpec((tm, D), lambda i: (i, 0)),
        out_shape=jax.ShapeDtypeStruct((n, D), F32),
        scratch_shapes=[pltpu.VMEM((tm, DFF), BF16)],
        compiler_params=_cparams(1),
        name="ffn_final" if final else "ffn",
    )(*args)


def _ctx_state_kernel(lg_ref, x_ref, sh_ref, sc_ref, nw_ref, wkT_ref, wv_ref, sf_ref, sb_ref):
    hd = pl.program_id(0)
    h = _norm_mod(x_ref[...], nw_ref[...], sc_ref[1:2, :], sh_ref[1:2, :]).astype(BF16)
    kT = lax.dot_general(wkT_ref[...], h, (((1,), (1,)), ((), ())), preferred_element_type=F32)
    v = _dot(h, wv_ref[...]).astype(BF16)
    j = lax.broadcasted_iota(jnp.int32, (DK, CTX), 1).astype(F32)
    wf = jnp.exp(lg_ref[0, hd] * (CTX - 1.0 - j))
    wb = jnp.exp(lg_ref[1, hd] * j)
    sf_ref[0] = _dot((kT * wf).astype(BF16), v)
    sb_ref[0] = _dot((kT * wb).astype(BF16), v)


def _ctx_state(lg, ctx1, mods, nw, wkT, wall):
    return pl.pallas_call(
        _ctx_state_kernel,
        grid=(H,),
        in_specs=[
            pl.BlockSpec(memory_space=pltpu.SMEM),
            _resident((CTX, D)),
            _mod_spec(3), _mod_spec(4),
            _resident((1, D)),
            pl.BlockSpec((DK, D), lambda h: (h, 0)),
            pl.BlockSpec((D, DV), lambda h: (0, _OV // DV + h)),
        ],
        out_specs=[pl.BlockSpec((1, DK, DV), lambda h: (h, 0, 0))] * 2,
        out_shape=[jax.ShapeDtypeStruct((H, DK, DV), F32)] * 2,
        compiler_params=_cparams(1),
        name="ctx_state",
    )(lg, ctx1, mods, mods, nw, wkT, wall)


_OQ, _OK, _OV, _OG = 0, QK, 2 * QK, 2 * QK + VW
_OF, _OGA, _OGF = 2 * QK + 2 * VW, 2 * QK + 2 * VW + D, 2 * QK + 2 * VW + 2 * D
_WIN = 2 * QK + 2 * VW + 3 * D
HALF = DK // 2


def _proj_kernel(lg_ref, x_ref, sh_ref, sc_ref, nw_ref, w_ref, wkT_ref, md_ref, gnw_ref, cq_ref, sq_ref, ck_ref,
                 sk_ref, s0_ref, q_ref, kT_ref, v_ref, sg_ref, G_ref, sga_ref, sgf_ref, sb_ref, S_ref, ZE_ref):
    @pl.when(pl.program_id(0) == 0)
    def _():
        S_ref[...] = s0_ref[...]
        c = lax.broadcasted_iota(jnp.int32, (DK, C), 1).astype(F32)
        for hd in range(H):
            ZE_ref[hd] = jnp.exp(lg_ref[1, hd] * c)

    h = _norm_mod(x_ref[...], nw_ref[...], sc_ref[0:1, :], sh_ref[0:1, :]).astype(BF16)

    q = _dot(h, w_ref[:, _OQ:_OQ + QK])
    cq, sq = cq_ref[...], sq_ref[...]
    for hd in range(H):
        a1 = q[:, hd * DK:hd * DK + HALF]
        a2 = q[:, hd * DK + HALF:(hd + 1) * DK]
        q_ref[:, hd * DK:hd * DK + HALF] = (a1 * cq - a2 * sq).astype(BF16)
        q_ref[:, hd * DK + HALF:(hd + 1) * DK] = (a1 * sq + a2 * cq).astype(BF16)

    kT = lax.dot_general(wkT_ref[...], h, (((1,), (1,)), ((), ())), preferred_element_type=F32)
    ck, sk = ck_ref[...], sk_ref[...]
    for hd in range(H):
        a1 = kT[hd * DK:hd * DK + HALF, :]
        a2 = kT[hd * DK + HALF:(hd + 1) * DK, :]
        kT_ref[hd * DK:hd * DK + HALF, :] = (a1 * ck - a2 * sk).astype(BF16)
        kT_ref[hd * DK + HALF:(hd + 1) * DK, :] = (a1 * sk + a2 * ck).astype(BF16)

    v = _dot(h, w_ref[:, _OV:_OV + VW]).astype(BF16)
    v_ref[...] = v
    for hd in range(H):
        S = S_ref[hd]
        sb_ref[hd, 0] = S.astype(BF16)
        kz = (kT_ref[hd * DK:(hd + 1) * DK, :].astype(F32) * ZE_ref[hd]).astype(BF16)
        g_chunk = jnp.exp(jnp.full((1, DV), lg_ref[1, hd] * C, F32))
        S_ref[hd] = S * g_chunk + _dot(kz, v[:, hd * DV:(hd + 1) * DV])

    g = _dot(h, w_ref[:, _OG:_OG + VW])
    sg_ref[...] = (g * jax.nn.sigmoid(g) * gnw_ref[...]).astype(BF16)

    f = _dot(h, w_ref[:, _OF:_OF + D]).astype(BF16)
    for gi in range(NG):
        z = _dot(f[:, gi * DG:(gi + 1) * DG], md_ref[...])
        G_ref[0, :, gi * DG:(gi + 1) * DG] = z[:, :DG].astype(BF16)
        G_ref[1, :, gi * DG:(gi + 1) * DG] = z[:, DG:].astype(BF16)

    sga_ref[...] = jax.nn.sigmoid(_dot(h, w_ref[:, _OGA:_OGA + D])).astype(BF16)
    sgf_ref[...] = jax.nn.sigmoid(_dot(h, w_ref[:, _OGF:_OGF + D])).astype(BF16)


def _proj(lg, x1, mods, nw, wall, wkT, md, gnw, cq, sq, ck, sk, s_b0):
    tm = C
    n = L // tm
    row = lambda i: (n - 1 - i, 0)
    col = lambda i: (0, n - 1 - i)
    return pl.pallas_call(
        _proj_kernel,
        grid=(n,),
        in_specs=[
            pl.BlockSpec(memory_space=pltpu.SMEM),
            pl.BlockSpec((tm, D), row),
            _mod_spec(3), _mod_spec(4),
            _resident((1, D)),
            _resident((D, _WIN)),
            _resident((QK, D)),
            _resident((DG, 2 * DG)),
            _resident((1, VW)),
            pl.BlockSpec((tm, HALF), row), pl.BlockSpec((tm, HALF), row),
            pl.BlockSpec((HALF, tm), col), pl.BlockSpec((HALF, tm), col),
            _resident((H, DK, DV)),
        ],
        out_specs=[
            pl.BlockSpec((tm, QK), row),
            pl.BlockSpec((QK, tm), col),
            pl.BlockSpec((tm, VW), row),
            pl.BlockSpec((tm, VW), row),
            pl.BlockSpec((2, tm, D), lambda i: (0, n - 1 - i, 0)),
            pl.BlockSpec((tm, D), row),
            pl.BlockSpec((tm, D), row),
            pl.BlockSpec((H, 1, DK, DV), lambda i: (0, n - 1 - i, 0, 0)),
        ],
        out_shape=[
            jax.ShapeDtypeStruct((L, QK), BF16),
            jax.ShapeDtypeStruct((QK, L), BF16),
            jax.ShapeDtypeStruct((L, VW), BF16),
            jax.ShapeDtypeStruct((L, VW), BF16),
            jax.ShapeDtypeStruct((2, L, D), BF16),
            jax.ShapeDtypeStruct((L, D), BF16),
            jax.ShapeDtypeStruct((L, D), BF16),
            jax.ShapeDtypeStruct((H, L // C, DK, DV), BF16),
        ],
        scratch_shapes=[pltpu.VMEM((H, DK, DV), F32), pltpu.VMEM((H, DK, C), F32)],
        compiler_params=_cparams(1),
        name="proj",
    )(lg, x1, mods, mods, nw, wall, wkT, md, gnw, cq, sq, ck, sk, s_b0)


FFT_NT = 16
FFT_KB = 16


def _fft1_kernel(m_ref, g_ref, o_ref):
    z = jnp.transpose(g_ref[...], (2, 0, 1, 3)).reshape(FFT_NT, 2 * N1, D)
    for t in range(FFT_NT):
        o_ref[t] = _dot(m_ref[...], z[t]).astype(BF16)


def _fft1(m1, g4):
    return pl.pallas_call(
        _fft1_kernel,
        grid=(N2 // FFT_NT,),
        in_specs=[_resident((2 * N1, 2 * N1)),
                  pl.BlockSpec((2, N1, FFT_NT, D), lambda j: (0, 0, j, 0))],
        out_specs=pl.BlockSpec((FFT_NT, 2 * N1, D), lambda j: (j, 0, 0)),
        out_shape=jax.ShapeDtypeStruct((N2, 2 * N1, D), BF16),
        compiler_params=_cparams(1),
        name="fft1",
    )(m1, g4)


def _fft3_kernel(t_ref, pf_ref, a_ref, o_ref, xr_ref):
    z = jnp.transpose(a_ref[...], (2, 1, 0, 3)).reshape(FFT_KB, 2 * N2, D)
    for i in range(FFT_KB):
        xr_ref[i * N2:(i + 1) * N2, :] = _dot(t_ref[i], z[i]).astype(BF16)
    fo = _dot(xr_ref[...], pf_ref[...]).astype(BF16).reshape(FFT_KB, N2, D)
    o_ref[...] = jnp.transpose(fo, (1, 0, 2))


def _fft3(t3, a4, pf):
    return pl.pallas_call(
        _fft3_kernel,
        grid=(N1 // FFT_KB,),
        in_specs=[pl.BlockSpec((FFT_KB, N2, 2 * N2), lambda j: (j, 0, 0)),
                  _resident((D, D)),
                  pl.BlockSpec((N2, 2, FFT_KB, D), lambda j: (0, 0, j, 0))],
        out_specs=pl.BlockSpec((N2, FFT_KB, D), lambda j: (0, j, 0)),
        out_shape=jax.ShapeDtypeStruct((N2, N1, D), BF16),
        scratch_shapes=[pltpu.VMEM((FFT_KB * N2, D), BF16)],
        compiler_params=_cparams(1),
        name="fft3",
    )(t3, pf, a4)


TB_R = 512


def _retmix_kernel(lg_ref, q_ref, kT_ref, v_ref, s0_ref, sb_ref, sgw_ref, x_ref, g_ref, fo_ref, sga_ref, sgf_ref,
                   pr_ref, wo_ref, o_ref, S_ref, D_ref, XF_ref, XB_ref, ZE_ref, rin_ref):
    @pl.when(pl.program_id(0) == 0)
    def _():
        S_ref[...] = s0_ref[...]
        r = lax.broadcasted_iota(jnp.int32, (C, C), 0).astype(F32)
        c = lax.broadcasted_iota(jnp.int32, (C, C), 1).astype(F32)
        d = r - c
        for hd in range(H):
            lgf, lgb = lg_ref[0, hd], lg_ref[1, hd]
            D_ref[hd] = (jnp.where(d >= 0, jnp.exp(lgf * jnp.maximum(d, 0.0)), 0.0)
                         + jnp.where(d <= 0, jnp.exp(lgb * jnp.maximum(-d, 0.0)), 0.0))
            XF_ref[hd] = jnp.exp(lgf * (r + 1.0))
            XB_ref[hd] = jnp.exp(lgb * (C - r))
            ZE_ref[hd] = jnp.exp(lgf * (C - 1.0 - c))

    for hd in range(H):
        g_chunk = jnp.exp(jnp.full((1, DV), lg_ref[0, hd] * C, F32))
        for ci in range(TB_R // C):
            rows = slice(ci * C, (ci + 1) * C)
            qc = q_ref[rows, hd * DK:(hd + 1) * DK]
            kTc = kT_ref[hd * DK:(hd + 1) * DK, rows]
            vc = v_ref[rows, hd * DV:(hd + 1) * DV]
            S = S_ref[hd]
            p = (_dot(qc, kTc) * D_ref[hd]).astype(BF16)
            qf = (qc.astype(F32) * XF_ref[hd]).astype(BF16)
            qb = (qc.astype(F32) * XB_ref[hd]).astype(BF16)
            o = _dot(p, vc) + _dot(qf, S.astype(BF16)) + _dot(qb, sb_ref[hd, ci])
            kz = (kTc.astype(F32) * ZE_ref[hd]).astype(BF16)
            S_ref[hd] = S * g_chunk + _dot(kz, vc)
            mu = jnp.mean(o, axis=-1, keepdims=True)
            d = o - mu
            var = jnp.mean(d * d, axis=-1, keepdims=True)
            on = d * lax.rsqrt(var + EPS)
            rin_ref[rows, hd * DV:(hd + 1) * DV] = (sgw_ref[rows, hd * DV:(hd + 1) * DV].astype(F32) * on).astype(BF16)

    r = _dot(rin_ref[...], pr_ref[...])
    m = sga_ref[...].astype(F32) * r + sgf_ref[...].astype(F32) * fo_ref[...].astype(F32)
    y = _dot(m.astype(BF16), wo_ref[...])
    o_ref[...] = x_ref[...] + g_ref[0:1, :] * y


def _retmix(lg, q, kT, v, s0, sb, sgw, x1, mods, fo, sga, sgf, p_ret, w_out):
    tb = TB_R
    row = lambda i: (i, 0)
    return pl.pallas_call(
        _retmix_kernel,
        grid=(L // tb,),
        in_specs=[
            pl.BlockSpec(memory_space=pltpu.SMEM),
            pl.BlockSpec((tb, QK), row),
            pl.BlockSpec((QK, tb), lambda i: (0, i)),
            pl.BlockSpec((tb, VW), row),
            _resident((H, DK, DV)),
            pl.BlockSpec((H, tb // C, DK, DV), lambda i: (0, i, 0, 0)),
            pl.BlockSpec((tb, VW), row),
            pl.BlockSpec((tb, D), row),
            _mod_spec(5),
            pl.BlockSpec((tb, D), row), pl.BlockSpec((tb, D), row), pl.BlockSpec((tb, D), row),
            _resident((VW, D)), _resident((D, D)),
        ],
        out_specs=pl.BlockSpec((tb, D), row),
        out_shape=jax.ShapeDtypeStruct((L, D), F32),
        scratch_shapes=[pltpu.VMEM((H, DK, DV), F32)] + [pltpu.VMEM((H, C, C), F32)] * 4
                       + [pltpu.VMEM((tb, VW), BF16)],
        compiler_params=_cparams(1),
        name="retmix",
    )(lg, q, kT, v, s0, sb, sgw, x1, mods, fo, sga, sgf, p_ret, w_out)


def _rope_tables():
    rows = L // GRID_W
    row = np.repeat(np.arange(rows, dtype=np.float64), GRID_W)
    col = np.tile(np.arange(GRID_W, dtype=np.float64), rows)
    n_freq = DK // 4
    inv = ROPE_BASE ** (-np.arange(n_freq, dtype=np.float64) / n_freq)
    ang = np.concatenate([row[:, None] * inv, col[:, None] * inv], axis=-1)
    return np.cos(ang), np.sin(ang)


def _dft_tables():
    two_pi = 2.0 * np.pi
    th = (np.outer(np.arange(DG), np.arange(DG)) % DG) * (two_pi / DG)
    md = np.concatenate([np.cos(th), -np.sin(th)], axis=1)
    th = (np.outer(np.arange(N1), np.arange(N1)) % N1) * (two_pi / N1)
    c1, s1 = np.cos(th), np.sin(th)
    m1 = np.block([[c1, s1], [-s1, c1]])
    k = np.arange(N1)[:, None, None] + N1 * np.arange(N2)[None, :, None]
    th = ((k * np.arange(N2)[None, None, :]) % L) * (two_pi / L)
    t3 = np.concatenate([np.cos(th), np.sin(th)], axis=-1) / np.sqrt(float(L) * DG)
    f32 = lambda a: jnp.asarray(a.astype(np.float32))
    return f32(md).astype(BF16), f32(m1).astype(BF16), f32(t3).astype(BF16)


def kernel(x, c, ctx, c_ctx, w_ada, b_ada, norm_ffn1, w13_ffn1, w2_ffn1, norm_mix, w_in, decay_fwd, decay_bwd,
           ret_gn_w, p_ret, p_four, w_out, norm_ffn2, w13_ffn2, w2_ffn2, norm_final):
    assert x.shape == (1, L, D) and ctx.shape == (1, CTX, D) and w_ada.shape[0] == 1

    pm = np.zeros((DK, DK), np.float32)
    pm[np.concatenate([np.arange(0, DK, 2), np.arange(1, DK, 2)]), np.arange(DK)] = 1.0
    wall, wkT = _prep(w_in[0], jnp.asarray(pm).astype(BF16), jnp.asarray(np.ascontiguousarray(pm.T)).astype(BF16))
    w13a, w13b = w13_ffn1[0].astype(BF16), w13_ffn2[0].astype(BF16)
    w2a, w2b = w2_ffn1[0].astype(BF16), w2_ffn2[0].astype(BF16)
    pr, pf, wo = p_ret[0].astype(BF16), p_four[0].astype(BF16), w_out[0].astype(BF16)

    lg = jnp.stack([jax.nn.log_sigmoid(decay_fwd[0].astype(F32)),
                    jax.nn.log_sigmoid(decay_bwd[0].astype(F32))])
    cos, sin = _rope_tables()
    qscale = DK ** -0.5
    f32c = lambda a: jnp.asarray(np.ascontiguousarray(a).astype(np.float32))
    cq, sq, ck, sk = f32c(cos * qscale), f32c(sin * qscale), f32c(cos.T), f32c(sin.T)
    md, m1, t3 = _dft_tables()

    cs = jnp.concatenate([c, c_ctx[None, :], jnp.zeros((6, D), F32)], axis=0).T
    mods = _mods(cs, w_ada[0], b_ada)

    x0 = x[0]
    x1 = _ffn(x0, mods, 0, 0, norm_ffn1, w13a, w2a)
    ctx1 = _ffn(ctx[0], mods, 1, 0, norm_ffn1, w13a, w2a)

    s_f, s_b = _ctx_state(lg, ctx1, mods, norm_mix, wkT, wall)

    q, kT, v, sgw, G, sga, sgf, sb = _proj(lg, x1, mods, norm_mix, wall, wkT, md, ret_gn_w, cq, sq, ck, sk, s_b)
    a = _fft1(m1, G.reshape(2, N1, N2, D))
    fo = _fft3(t3, a.reshape(N2, 2, N1, D), pf).reshape(L, D)

    x2 = _retmix(lg, q, kT, v, s_f, sb, sgw, x1, mods, fo, sga, sgf, pr, wo)
    out = _ffn(x2, mods, 0, 6, norm_ffn2, w13b, w2b, nf=norm_final[None, :])
    return out[None]
```

```python
import functools

import numpy as np
import jax
import jax.numpy as jnp
from jax import lax
from jax.experimental import pallas as pl
from jax.experimental.pallas import tpu as pltpu

D = 1024
L = 16384
GRID_W = 64
CTX = 256
H = 4
DK = 256
DV = 512
QK = H * DK
VW = H * DV
NG = 4
DG = 256
DFF = 2816
ROPE_BASE = 10000.0
EPS = 1e-6
NMOD = 9

F32 = jnp.float32
BF16 = jnp.bfloat16

C = 256
N1 = 128
N2 = 128

VMEM_LIMIT = 56 * 1024 * 1024


def _dot(a, b):
    return jnp.dot(a, b, preferred_element_type=F32)


def _cparams(n_axes):
    return pltpu.CompilerParams(
        dimension_semantics=("arbitrary",) * n_axes, vmem_limit_bytes=VMEM_LIMIT)


def _resident(shape):
    nd = len(shape)
    return pl.BlockSpec(shape, lambda *_: (0,) * nd, pipeline_mode=pl.Buffered(1))


def _mod_spec(j):
    return pl.BlockSpec((8, D), lambda *_: (0, j), pipeline_mode=pl.Buffered(1))


def _norm_mod(x, nw, sc, sh):
    y = x * lax.rsqrt(jnp.mean(x * x, axis=-1, keepdims=True) + EPS)
    return y * (nw * (1.0 + sc)) + sh


def _mods_kernel(cs_ref, w_ref, b_ref, o_ref):
    cs = cs_ref[...]
    s = cs * jax.nn.sigmoid(cs)
    w = w_ref[...]
    rows = [jnp.sum(w * s[:, r:r + 1], axis=0, keepdims=True) for r in range(2)]
    o_ref[...] = jnp.concatenate(rows + [jnp.zeros((6, w.shape[1]), F32)], axis=0) + b_ref[...]


def _mods(cs, w_ada, b_ada):
    tn = 1024
    return pl.pallas_call(
        _mods_kernel,
        grid=(NMOD * D // tn,),
        in_specs=[
            _resident((D, 8)),
            pl.BlockSpec((D, tn), lambda j: (0, j)),
            pl.BlockSpec((1, tn), lambda j: (0, j)),
        ],
        out_specs=pl.BlockSpec((8, tn), lambda j: (0, j)),
        out_shape=jax.ShapeDtypeStruct((8, NMOD * D), F32),
        compiler_params=_cparams(1),
        name="mods",
    )(cs, w_ada, b_ada)


def _prep_kernel(w_ref, p_ref, pT_ref, o_ref, kT_ref):
    j = pl.program_id(0)
    w = w_ref[...].astype(BF16)

    @pl.when(j >= 2)
    def _():
        o_ref[...] = w

    @pl.when(j < 2)
    def _():
        for hd in range(H):
            o_ref[:, hd * DK:(hd + 1) * DK] = _dot(w[:, hd * DK:(hd + 1) * DK], p_ref[...]).astype(BF16)

    @pl.when(j == 1)
    def _():
        for hd in range(H):
            kT_ref[hd * DK:(hd + 1) * DK, :] = lax.dot_general(
                pT_ref[...], w[:, hd * DK:(hd + 1) * DK], (((1,), (1,)), ((), ())),
                preferred_element_type=F32).astype(BF16)


def _prep(w_in, pm, pmT):
    tn = QK
    return pl.pallas_call(
        _prep_kernel,
        grid=(_WIN // tn,),
        in_specs=[pl.BlockSpec((D, tn), lambda j: (0, j)), _resident((DK, DK)), _resident((DK, DK))],
        out_specs=[pl.BlockSpec((D, tn), lambda j: (0, j)),
                   pl.BlockSpec((QK, D), lambda j: (0, 0))],
        out_shape=[jax.ShapeDtypeStruct((D, _WIN), BF16), jax.ShapeDtypeStruct((QK, D), BF16)],
        compiler_params=_cparams(1),
        name="prep",
    )(w_in, pm, pmT)


FC = 256


def _ffn_kernel(row, final, x_ref, sh_ref, sc_ref, g_ref, nw_ref, w13_ref, w2_ref, *rest):
    if final:
        y_ref, g2_ref, nf_ref, o_ref, t_ref = rest
        x = x_ref[...] + g2_ref[row:row + 1, :] * y_ref[...].astype(F32)
    else:
        o_ref, t_ref = rest
        x = x_ref[...]
    h = _norm_mod(x, nw_ref[...], sc_ref[row:row + 1, :], sh_ref[row:row + 1, :]).astype(BF16)
    for j in range(DFF // FC):
        a = _dot(h, w13_ref[:, j * FC:(j + 1) * FC])
        b = _dot(h, w13_ref[:, DFF + j * FC:DFF + (j + 1) * FC])
        t_ref[:, j * FC:(j + 1) * FC] = (a * jax.nn.sigmoid(a) * b).astype(BF16)
    y = _dot(t_ref[...], w2_ref[...])
    out = x + (0.5 * g_ref[row:row + 1, :]) * y
    if final:
        out = out * lax.rsqrt(jnp.mean(out * out, axis=-1, keepdims=True) + EPS) * nf_ref[...]
    o_ref[...] = out


def _ffn(x, mods, row, mod0, nw, w13, w2, y=None, nf=None, tm=1024):
    n = x.shape[0]
    tm = min(tm, n)
    final = nf is not None
    in_specs = [
        pl.BlockSpec((tm, D), lambda i: (i, 0)),
        _mod_spec(mod0), _mod_spec(mod0 + 1), _mod_spec(mod0 + 2),
        _resident((1, D)),
        _resident((D, 2 * DFF)), _resident((DFF, D)),
    ]
    args = [x, mods, mods, mods, nw, w13, w2]
    if final:
        in_specs += [pl.BlockSpec((tm, D), lambda i: (i, 0)), _mod_spec(mod0 - 1), _resident((1, D))]
        args += [y, mods, nf]
    return pl.pallas_call(
        functools.partial(_ffn_kernel, row, final),
        grid=(n // tm,),
        in_specs=in_specs,
        out_specs=pl.BlockSpec((tm, D), lambda i: (i, 0)),
        out_shape=jax.ShapeDtypeStruct((n, D), F32),
        scratch_shapes=[pltpu.VMEM((tm, DFF), BF16)],
        compiler_params=_cparams(1),
        name="ffn_final" if final else "ffn",
    )(*args)


def _ctx_state_kernel(lg_ref, x_ref, sh_ref, sc_ref, nw_ref, wkT_ref, wv_ref, sf_ref, sb_ref):
    hd = pl.program_id(0)
    h = _norm_mod(x_ref[...], nw_ref[...], sc_ref[1:2, :], sh_ref[1:2, :]).astype(BF16)
    kT = lax.dot_general(wkT_ref[...], h, (((1,), (1,)), ((), ())), preferred_element_type=F32)
    v = _dot(h, wv_ref[...]).astype(BF16)
    j = lax.broadcasted_iota(jnp.int32, (DK, CTX), 1).astype(F32)
    wf = jnp.exp(lg_ref[0, hd] * (CTX - 1.0 - j))
    wb = jnp.exp(lg_ref[1, hd] * j)
    sf_ref[0] = _dot((kT * wf).astype(BF16), v)
    sb_ref[0] = _dot((kT * wb).astype(BF16), v)


def _ctx_state(lg, ctx1, mods, nw, wkT, wall):
    return pl.pallas_call(
        _ctx_state_kernel,
        grid=(H,),
        in_specs=[
            pl.BlockSpec(memory_space=pltpu.SMEM),
            _resident((CTX, D)),
            _mod_spec(3), _mod_spec(4),
            _resident((1, D)),
            pl.BlockSpec((DK, D), lambda h: (h, 0)),
            pl.BlockSpec((D, DV), lambda h: (0, _OV // DV + h)),
        ],
        out_specs=[pl.BlockSpec((1, DK, DV), lambda h: (h, 0, 0))] * 2,
        out_shape=[jax.ShapeDtypeStruct((H, DK, DV), F32)] * 2,
        compiler_params=_cparams(1),
        name="ctx_state",
    )(lg, ctx1, mods, mods, nw, wkT, wall)


_OQ, _OK, _OV, _OG = 0, QK, 2 * QK, 2 * QK + VW
_OF, _OGA, _OGF = 2 * QK + 2 * VW, 2 * QK + 2 * VW + D, 2 * QK + 2 * VW + 2 * D
_WIN = 2 * QK + 2 * VW + 3 * D
HALF = DK // 2


def _proj_kernel(lg_ref, x_ref, sh_ref, sc_ref, nw_ref, w_ref, wkT_ref, md_ref, gnw_ref, cq_ref, sq_ref, ck_ref,
                 sk_ref, s0_ref, q_ref, kT_ref, v_ref, sg_ref, G_ref, sga_ref, sgf_ref, sb_ref, S_ref, ZE_ref):
    @pl.when(pl.program_id(0) == 0)
    def _():
        S_ref[...] = s0_ref[...]
        c = lax.broadcasted_iota(jnp.int32, (DK, C), 1).astype(F32)
        for hd in range(H):
            ZE_ref[hd] = jnp.exp(lg_ref[1, hd] * c)

    h = _norm_mod(x_ref[...], nw_ref[...], sc_ref[0:1, :], sh_ref[0:1, :]).astype(BF16)

    q = _dot(h, w_ref[:, _OQ:_OQ + QK])
    cq, sq = cq_ref[...], sq_ref[...]
    for hd in range(H):
        a1 = q[:, hd * DK:hd * DK + HALF]
        a2 = q[:, hd * DK + HALF:(hd + 1) * DK]
        q_ref[:, hd * DK:hd * DK + HALF] = (a1 * cq - a2 * sq).astype(BF16)
        q_ref[:, hd * DK + HALF:(hd + 1) * DK] = (a1 * sq + a2 * cq).astype(BF16)

    kT = lax.dot_general(wkT_ref[...], h, (((1,), (1,)), ((), ())), preferred_element_type=F32)
    ck, sk = ck_ref[...], sk_ref[...]
    for hd in range(H):
        a1 = kT[hd * DK:hd * DK + HALF, :]
        a2 = kT[hd * DK + HALF:(hd + 1) * DK, :]
        kT_ref[hd * DK:hd * DK + HALF, :] = (a1 * ck - a2 * sk).astype(BF16)
        kT_ref[hd * DK + HALF:(hd + 1) * DK, :] = (a1 * sk + a2 * ck).astype(BF16)

    v = _dot(h, w_ref[:, _OV:_OV + VW]).astype(BF16)
    v_ref[...] = v
    for hd in range(H):
        S = S_ref[hd]
        sb_ref[hd, 0] = S.astype(BF16)
        kz = (kT_ref[hd * DK:(hd + 1) * DK, :].astype(F32) * ZE_ref[hd]).astype(BF16)
        g_chunk = jnp.exp(jnp.full((1, DV), lg_ref[1, hd] * C, F32))
        S_ref[hd] = S * g_chunk + _dot(kz, v[:, hd * DV:(hd + 1) * DV])

    g = _dot(h, w_ref[:, _OG:_OG + VW])
    sg_ref[...] = (g * jax.nn.sigmoid(g) * gnw_ref[...]).astype(BF16)

    f = _dot(h, w_ref[:, _OF:_OF + D]).astype(BF16)
    for gi in range(NG):
        z = _dot(f[:, gi * DG:(gi + 1) * DG], md_ref[...])
        G_ref[0, :, gi * DG:(gi + 1) * DG] = z[:, :DG].astype(BF16)
        G_ref[1, :, gi * DG:(gi + 1) * DG] = z[:, DG:].astype(BF16)

    sga_ref[...] = jax.nn.sigmoid(_dot(h, w_ref[:, _OGA:_OGA + D])).astype(BF16)
    sgf_ref[...] = jax.nn.sigmoid(_dot(h, w_ref[:, _OGF:_OGF + D])).astype(BF16)


def _proj(lg, x1, mods, nw, wall, wkT, md, gnw, cq, sq, ck, sk, s_b0):
    tm = C
    n = L // tm
    row = lambda i: (n - 1 - i, 0)
    col = lambda i: (0, n - 1 - i)
    return pl.pallas_call(
        _proj_kernel,
        grid=(n,),
        in_specs=[
            pl.BlockSpec(memory_space=pltpu.SMEM),
            pl.BlockSpec((tm, D), row),
            _mod_spec(3), _mod_spec(4),
            _resident((1, D)),
            _resident((D, _WIN)),
            _resident((QK, D)),
            _resident((DG, 2 * DG)),
            _resident((1, VW)),
            pl.BlockSpec((tm, HALF), row), pl.BlockSpec((tm, HALF), row),
            pl.BlockSpec((HALF, tm), col), pl.BlockSpec((HALF, tm), col),
            _resident((H, DK, DV)),
        ],
        out_specs=[
            pl.BlockSpec((tm, QK), row),
            pl.BlockSpec((QK, tm), col),
            pl.BlockSpec((tm, VW), row),
            pl.BlockSpec((tm, VW), row),
            pl.BlockSpec((2, tm, D), lambda i: (0, n - 1 - i, 0)),
            pl.BlockSpec((tm, D), row),
            pl.BlockSpec((tm, D), row),
            pl.BlockSpec((H, 1, DK, DV), lambda i: (0, n - 1 - i, 0, 0)),
        ],
        out_shape=[
            jax.ShapeDtypeStruct((L, QK), BF16),
            jax.ShapeDtypeStruct((QK, L), BF16),
            jax.ShapeDtypeStruct((L, VW), BF16),
            jax.ShapeDtypeStruct((L, VW), BF16),
            jax.ShapeDtypeStruct((2, L, D), BF16),
            jax.ShapeDtypeStruct((L, D), BF16),
            jax.ShapeDtypeStruct((L, D), BF16),
            jax.ShapeDtypeStruct((H, L // C, DK, DV), BF16),
        ],
        scratch_shapes=[pltpu.VMEM((H, DK, DV), F32), pltpu.VMEM((H, DK, C), F32)],
        compiler_params=_cparams(1),
        name="proj",
    )(lg, x1, mods, mods, nw, wall, wkT, md, gnw, cq, sq, ck, sk, s_b0)


FFT_NT = 16
FFT_KB = 16


def _fft1_kernel(m_ref, g_ref, o_ref):
    z = jnp.transpose(g_ref[...], (2, 0, 1, 3)).reshape(FFT_NT, 2 * N1, D)
    for t in range(FFT_NT):
        o_ref[t] = _dot(m_ref[...], z[t]).astype(BF16)


def _fft1(m1, g4):
    return pl.pallas_call(
        _fft1_kernel,
        grid=(N2 // FFT_NT,),
        in_specs=[_resident((2 * N1, 2 * N1)),
                  pl.BlockSpec((2, N1, FFT_NT, D), lambda j: (0, 0, j, 0))],
        out_specs=pl.BlockSpec((FFT_NT, 2 * N1, D), lambda j: (j, 0, 0)),
        out_shape=jax.ShapeDtypeStruct((N2, 2 * N1, D), BF16),
        compiler_params=_cparams(1),
        name="fft1",
    )(m1, g4)


def _fft3_kernel(t_ref, pf_ref, a_ref, o_ref, xr_ref):
    z = jnp.transpose(a_ref[...], (2, 1, 0, 3)).reshape(FFT_KB, 2 * N2, D)
    for i in range(FFT_KB):
        xr_ref[i * N2:(i + 1) * N2, :] = _dot(t_ref[i], z[i]).astype(BF16)
    fo = _dot(xr_ref[...], pf_ref[...]).astype(BF16).reshape(FFT_KB, N2, D)
    o_ref[...] = jnp.transpose(fo, (1, 0, 2))


def _fft3(t3, a4, pf):
    return pl.pallas_call(
        _fft3_kernel,
        grid=(N1 // FFT_KB,),
        in_specs=[pl.BlockSpec((FFT_KB, N2, 2 * N2), lambda j: (j, 0, 0)),
                  _resident((D, D)),
                  pl.BlockSpec((N2, 2, FFT_KB, D), lambda j: (0, 0, j, 0))],
        out_specs=pl.BlockSpec((N2, FFT_KB, D), lambda j: (0, j, 0)),
        out_shape=jax.ShapeDtypeStruct((N2, N1, D), BF16),
        scratch_shapes=[pltpu.VMEM((FFT_KB * N2, D), BF16)],
        compiler_params=_cparams(1),
        name="fft3",
    )(t3, pf, a4)


TB_R = 512


def _retmix_kernel(lg_ref, q_ref, kT_ref, v_ref, s0_ref, sb_ref, sgw_ref, fo_ref, sga_ref, sgf_ref,
                   pr_ref, wo_ref, o_ref, S_ref, D_ref, XF_ref, XB_ref, ZE_ref, rin_ref):
    @pl.when(pl.program_id(0) == 0)
    def _():
        S_ref[...] = s0_ref[...]
        r = lax.broadcasted_iota(jnp.int32, (C, C), 0).astype(F32)
        c = lax.broadcasted_iota(jnp.int32, (C, C), 1).astype(F32)
        d = r - c
        for hd in range(H):
            lgf, lgb = lg_ref[0, hd], lg_ref[1, hd]
            D_ref[hd] = (jnp.where(d >= 0, jnp.exp(lgf * jnp.maximum(d, 0.0)), 0.0)
                         + jnp.where(d <= 0, jnp.exp(lgb * jnp.maximum(-d, 0.0)), 0.0))
            XF_ref[hd] = jnp.exp(lgf * (r + 1.0))
            XB_ref[hd] = jnp.exp(lgb * (C - r))
            ZE_ref[hd] = jnp.exp(lgf * (C - 1.0 - c))

    for hd in range(H):
        g_chunk = jnp.exp(jnp.full((1, DV), lg_ref[0, hd] * C, F32))
        for ci in range(TB_R // C):
            rows = slice(ci * C, (ci + 1) * C)
            qc = q_ref[rows, hd * DK:(hd + 1) * DK]
            kTc = kT_ref[hd * DK:(hd + 1) * DK, rows]
            vc = v_ref[rows, hd * DV:(hd + 1) * DV]
            S = S_ref[hd]
            p = (_dot(qc, kTc) * D_ref[hd]).astype(BF16)
            qf = (qc.astype(F32) * XF_ref[hd]).astype(BF16)
            qb = (qc.astype(F32) * XB_ref[hd]).astype(BF16)
            o = _dot(p, vc) + _dot(qf, S.astype(BF16)) + _dot(qb, sb_ref[hd, ci])
            kz = (kTc.astype(F32) * ZE_ref[hd]).astype(BF16)
            S_ref[hd] = S * g_chunk + _dot(kz, vc)
            mu = jnp.mean(o, axis=-1, keepdims=True)
            d = o - mu
            var = jnp.mean(d * d, axis=-1, keepdims=True)
            on = d * lax.rsqrt(var + EPS)
            rin_ref[rows, hd * DV:(hd + 1) * DV] = (sgw_ref[rows, hd * DV:(hd + 1) * DV].astype(F32) * on).astype(BF16)

    r = _dot(rin_ref[...], pr_ref[...])
    m = sga_ref[...].astype(F32) * r + sgf_ref[...].astype(F32) * fo_ref[...].astype(F32)
    o_ref[...] = _dot(m.astype(BF16), wo_ref[...]).astype(BF16)


def _retmix(lg, q, kT, v, s0, sb, sgw, fo, sga, sgf, p_ret, w_out):
    tb = TB_R
    row = lambda i: (i, 0)
    return pl.pallas_call(
        _retmix_kernel,
        grid=(L // tb,),
        in_specs=[
            pl.BlockSpec(memory_space=pltpu.SMEM),
            pl.BlockSpec((tb, QK), row),
            pl.BlockSpec((QK, tb), lambda i: (0, i)),
            pl.BlockSpec((tb, VW), row),
            _resident((H, DK, DV)),
            pl.BlockSpec((H, tb // C, DK, DV), lambda i: (0, i, 0, 0)),
            pl.BlockSpec((tb, VW), row),
            pl.BlockSpec((tb, D), row), pl.BlockSpec((tb, D), row), pl.BlockSpec((tb, D), row),
            _resident((VW, D)), _resident((D, D)),
        ],
        out_specs=pl.BlockSpec((tb, D), row),
        out_shape=jax.ShapeDtypeStruct((L, D), BF16),
        scratch_shapes=[pltpu.VMEM((H, DK, DV), F32)] + [pltpu.VMEM((H, C, C), F32)] * 4
                       + [pltpu.VMEM((tb, VW), BF16)],
        compiler_params=_cparams(1),
        name="retmix",
    )(lg, q, kT, v, s0, sb, sgw, fo, sga, sgf, p_ret, w_out)


def _rope_tables():
    rows = L // GRID_W
    row = np.repeat(np.arange(rows, dtype=np.float64), GRID_W)
    col = np.tile(np.arange(GRID_W, dtype=np.float64), rows)
    n_freq = DK // 4
    inv = ROPE_BASE ** (-np.arange(n_freq, dtype=np.float64) / n_freq)
    ang = np.concatenate([row[:, None] * inv, col[:, None] * inv], axis=-1)
    return np.cos(ang), np.sin(ang)


def _dft_tables():
    two_pi = 2.0 * np.pi
    th = (np.outer(np.arange(DG), np.arange(DG)) % DG) * (two_pi / DG)
    md = np.concatenate([np.cos(th), -np.sin(th)], axis=1)
    th = (np.outer(np.arange(N1), np.arange(N1)) % N1) * (two_pi / N1)
    c1, s1 = np.cos(th), np.sin(th)
    m1 = np.block([[c1, s1], [-s1, c1]])
    k = np.arange(N1)[:, None, None] + N1 * np.arange(N2)[None, :, None]
    th = ((k * np.arange(N2)[None, None, :]) % L) * (two_pi / L)
    t3 = np.concatenate([np.cos(th), np.sin(th)], axis=-1) / np.sqrt(float(L) * DG)
    f32 = lambda a: jnp.asarray(a.astype(np.float32))
    return f32(md).astype(BF16), f32(m1).astype(BF16), f32(t3).astype(BF16)


def kernel(x, c, ctx, c_ctx, w_ada, b_ada, norm_ffn1, w13_ffn1, w2_ffn1, norm_mix, w_in, decay_fwd, decay_bwd,
           ret_gn_w, p_ret, p_four, w_out, norm_ffn2, w13_ffn2, w2_ffn2, norm_final):
    assert x.shape == (1, L, D) and ctx.shape == (1, CTX, D) and w_ada.shape[0] == 1

    pm = np.zeros((DK, DK), np.float32)
    pm[np.concatenate([np.arange(0, DK, 2), np.arange(1, DK, 2)]), np.arange(DK)] = 1.0
    wall, wkT = _prep(w_in[0], jnp.asarray(pm).astype(BF16), jnp.asarray(np.ascontiguousarray(pm.T)).astype(BF16))
    w13a, w13b = w13_ffn1[0].astype(BF16), w13_ffn2[0].astype(BF16)
    w2a, w2b = w2_ffn1[0].astype(BF16), w2_ffn2[0].astype(BF16)
    pr, pf, wo = p_ret[0].astype(BF16), p_four[0].astype(BF16), w_out[0].astype(BF16)

    lg = jnp.stack([jax.nn.log_sigmoid(decay_fwd[0].astype(F32)),
                    jax.nn.log_sigmoid(decay_bwd[0].astype(F32))])
    cos, sin = _rope_tables()
    qscale = DK ** -0.5
    f32c = lambda a: jnp.asarray(np.ascontiguousarray(a).astype(np.float32))
    cq, sq, ck, sk = f32c(cos * qscale), f32c(sin * qscale), f32c(cos.T), f32c(sin.T)
    md, m1, t3 = _dft_tables()

    cs = jnp.concatenate([c, c_ctx[None, :], jnp.zeros((6, D), F32)], axis=0).T
    mods = _mods(cs, w_ada[0], b_ada)

    x0 = x[0]
    x1 = _ffn(x0, mods, 0, 0, norm_ffn1, w13a, w2a)
    ctx1 = _ffn(ctx[0], mods, 1, 0, norm_ffn1, w13a, w2a)

    s_f, s_b = _ctx_state(lg, ctx1, mods, norm_mix, wkT, wall)

    q, kT, v, sgw, G, sga, sgf, sb = _proj(lg, x1, mods, norm_mix, wall, wkT, md, ret_gn_w, cq, sq, ck, sk, s_b)
    a = _fft1(m1, G.reshape(2, N1, N2, D))
    fo = _fft3(t3, a.reshape(N2, 2, N1, D), pf).reshape(L, D)

    y = _retmix(lg, q, kT, v, s_f, sb, sgw, fo, sga, sgf, pr, wo)
    out = _ffn(x1, mods, 0, 6, norm_ffn2, w13b, w2b, y=y, nf=norm_final[None, :])
    return out[None]
```

```python
import functools

import numpy as np
import jax
import jax.numpy as jnp
from jax import lax
from jax.experimental import pallas as pl
from jax.experimental.pallas import tpu as pltpu

D = 1024
L = 16384
GRID_W = 64
CTX = 256
H = 4
DK = 256
DV = 512
QK = H * DK
VW = H * DV
NG = 4
DG = 256
DFF = 2816
ROPE_BASE = 10000.0
EPS = 1e-6
NMOD = 9

F32 = jnp.float32
BF16 = jnp.bfloat16

C = 256
N1 = 128
N2 = 128

VMEM_LIMIT = 56 * 1024 * 1024


def _dot(a, b):
    return jnp.dot(a, b, preferred_element_type=F32)


def _cparams(n_axes):
    return pltpu.CompilerParams(
        dimension_semantics=("arbitrary",) * n_axes, vmem_limit_bytes=VMEM_LIMIT)


def _resident(shape):
    nd = len(shape)
    return pl.BlockSpec(shape, lambda *_: (0,) * nd, pipeline_mode=pl.Buffered(1))


def _mod_spec(j):
    return pl.BlockSpec((8, D), lambda *_: (0, j), pipeline_mode=pl.Buffered(1))


def _norm_mod(x, nw, sc, sh):
    y = x * lax.rsqrt(jnp.mean(x * x, axis=-1, keepdims=True) + EPS)
    return y * (nw * (1.0 + sc)) + sh


def _mods_kernel(cs_ref, w_ref, b_ref, o_ref):
    cs = cs_ref[...]
    s = cs * jax.nn.sigmoid(cs)
    w = w_ref[...]
    rows = [jnp.sum(w * s[:, r:r + 1], axis=0, keepdims=True) for r in range(2)]
    o_ref[...] = jnp.concatenate(rows + [jnp.zeros((6, w.shape[1]), F32)], axis=0) + b_ref[...]


def _mods(cs, w_ada, b_ada):
    tn = 2304
    return pl.pallas_call(
        _mods_kernel,
        grid=(NMOD * D // tn,),
        in_specs=[
            _resident((D, 8)),
            pl.BlockSpec((D, tn), lambda j: (0, j)),
            pl.BlockSpec((1, tn), lambda j: (0, j)),
        ],
        out_specs=pl.BlockSpec((8, tn), lambda j: (0, j)),
        out_shape=jax.ShapeDtypeStruct((8, NMOD * D), F32),
        compiler_params=_cparams(1),
        name="mods",
    )(cs, w_ada, b_ada)


def _prep_kernel(w_ref, p_ref, pT_ref, o_ref, kT_ref):
    j = pl.program_id(0)
    w = w_ref[...].astype(BF16)

    @pl.when(j >= 2)
    def _():
        o_ref[...] = w

    @pl.when(j < 2)
    def _():
        for hd in range(H):
            o_ref[:, hd * DK:(hd + 1) * DK] = _dot(w[:, hd * DK:(hd + 1) * DK], p_ref[...]).astype(BF16)

    @pl.when(j == 1)
    def _():
        for hd in range(H):
            kT_ref[hd * DK:(hd + 1) * DK, :] = lax.dot_general(
                pT_ref[...], w[:, hd * DK:(hd + 1) * DK], (((1,), (1,)), ((), ())),
                preferred_element_type=F32).astype(BF16)


def _prep(w_in, pm, pmT):
    tn = QK
    return pl.pallas_call(
        _prep_kernel,
        grid=(_WIN // tn,),
        in_specs=[pl.BlockSpec((D, tn), lambda j: (0, j)), _resident((DK, DK)), _resident((DK, DK))],
        out_specs=[pl.BlockSpec((D, tn), lambda j: (0, j)),
                   pl.BlockSpec((QK, D), lambda j: (0, 0))],
        out_shape=[jax.ShapeDtypeStruct((D, _WIN), BF16), jax.ShapeDtypeStruct((QK, D), BF16)],
        compiler_params=_cparams(1),
        name="prep",
    )(w_in, pm, pmT)


FC = 256


def _cast_blocks(src_refs, dst_refs):
    for s, d in zip(src_refs, dst_refs):
        d[...] = s[...].astype(BF16)


def _cast_specs(arrays, steps):
    specs = [pl.BlockSpec((a.shape[0] // steps, a.shape[1]), lambda i: (i, 0)) for a in arrays]
    shapes = [jax.ShapeDtypeStruct(a.shape, BF16) for a in arrays]
    return specs, shapes


def _ffn_kernel(row, final, ncast, x_ref, sh_ref, sc_ref, g_ref, nw_ref, w13_ref, w2_ref, *rest):
    if final:
        nf_ref, rest = rest[0], rest[1:]
    o_ref, t_ref = rest[ncast], rest[-1]
    _cast_blocks(rest[:ncast], rest[ncast + 1:-1])
    x = x_ref[...]
    h = _norm_mod(x, nw_ref[...], sc_ref[row:row + 1, :], sh_ref[row:row + 1, :]).astype(BF16)
    for j in range(DFF // FC):
        a = _dot(h, w13_ref[:, j * FC:(j + 1) * FC])
        b = _dot(h, w13_ref[:, DFF + j * FC:DFF + (j + 1) * FC])
        t_ref[:, j * FC:(j + 1) * FC] = (a * jax.nn.sigmoid(a) * b).astype(BF16)
    y = _dot(t_ref[...], w2_ref[...])
    out = x + (0.5 * g_ref[row:row + 1, :]) * y
    if final:
        out = out * lax.rsqrt(jnp.mean(out * out, axis=-1, keepdims=True) + EPS) * nf_ref[...]
    o_ref[...] = out


def _ffn(x, mods, row, mod0, nw, w13, w2, nf=None, cast=(), tm=1024):
    n = x.shape[0]
    tm = min(tm, n)
    final = nf is not None
    cast_specs, cast_shapes = _cast_specs(cast, n // tm)
    in_specs = [
        pl.BlockSpec((tm, D), lambda i: (i, 0)),
        _mod_spec(mod0), _mod_spec(mod0 + 1), _mod_spec(mod0 + 2),
        _resident((1, D)),
        _resident((D, 2 * DFF)), _resident((DFF, D)),
    ]
    args = [x, mods, mods, mods, nw, w13, w2]
    if final:
        in_specs.append(_resident((1, D)))
        args.append(nf)
    outs = pl.pallas_call(
        functools.partial(_ffn_kernel, row, final, len(cast)),
        grid=(n // tm,),
        in_specs=in_specs + cast_specs,
        out_specs=[pl.BlockSpec((tm, D), lambda i: (i, 0))] + cast_specs,
        out_shape=[jax.ShapeDtypeStruct((n, D), F32)] + cast_shapes,
        scratch_shapes=[pltpu.VMEM((tm, DFF), BF16)],
        compiler_params=_cparams(1),
        name="ffn_final" if final else "ffn",
    )(*args, *cast)
    return outs if cast else outs[0]


def _ctx_state_kernel(lg_ref, x_ref, sh_ref, sc_ref, nw_ref, wkT_ref, wv_ref, sf_ref, sb_ref):
    hd = pl.program_id(0)
    h = _norm_mod(x_ref[...], nw_ref[...], sc_ref[1:2, :], sh_ref[1:2, :]).astype(BF16)
    kT = lax.dot_general(wkT_ref[...], h, (((1,), (1,)), ((), ())), preferred_element_type=F32)
    v = _dot(h, wv_ref[...]).astype(BF16)
    j = lax.broadcasted_iota(jnp.int32, (DK, CTX), 1).astype(F32)
    wf = jnp.exp(lg_ref[0, hd] * (CTX - 1.0 - j))
    wb = jnp.exp(lg_ref[1, hd] * j)
    sf_ref[0] = _dot((kT * wf).astype(BF16), v)
    sb_ref[0] = _dot((kT * wb).astype(BF16), v)


def _ctx_state(lg, ctx1, mods, nw, wkT, wall):
    return pl.pallas_call(
        _ctx_state_kernel,
        grid=(H,),
        in_specs=[
            pl.BlockSpec(memory_space=pltpu.SMEM),
            _resident((CTX, D)),
            _mod_spec(3), _mod_spec(4),
            _resident((1, D)),
            pl.BlockSpec((DK, D), lambda h: (h, 0)),
            pl.BlockSpec((D, DV), lambda h: (0, _OV // DV + h)),
        ],
        out_specs=[pl.BlockSpec((1, DK, DV), lambda h: (h, 0, 0))] * 2,
        out_shape=[jax.ShapeDtypeStruct((H, DK, DV), F32)] * 2,
        compiler_params=_cparams(1),
        name="ctx_state",
    )(lg, ctx1, mods, mods, nw, wkT, wall)


_OQ, _OK, _OV, _OG = 0, QK, 2 * QK, 2 * QK + VW
_OF, _OGA, _OGF = 2 * QK + 2 * VW, 2 * QK + 2 * VW + D, 2 * QK + 2 * VW + 2 * D
_WIN = 2 * QK + 2 * VW + 3 * D
HALF = DK // 2


def _proj_kernel(lg_ref, x_ref, sh_ref, sc_ref, nw_ref, w_ref, wkT_ref, md_ref, gnw_ref, cq_ref, sq_ref, ck_ref,
                 sk_ref, s0_ref, pr_ref, pf_ref, wo_ref, q_ref, kT_ref, v_ref, sg_ref, G_ref, sga_ref, sgf_ref, sb_ref,
                 prb_ref, pfb_ref, wob_ref, S_ref, ZE_ref):
    _cast_blocks((pr_ref, pf_ref, wo_ref), (prb_ref, pfb_ref, wob_ref))

    @pl.when(pl.program_id(0) == 0)
    def _():
        S_ref[...] = s0_ref[...]
        c = lax.broadcasted_iota(jnp.int32, (DK, C), 1).astype(F32)
        for hd in range(H):
            ZE_ref[hd] = jnp.exp(lg_ref[1, hd] * c)

    h = _norm_mod(x_ref[...], nw_ref[...], sc_ref[0:1, :], sh_ref[0:1, :]).astype(BF16)

    q = _dot(h, w_ref[:, _OQ:_OQ + QK])
    cq, sq = cq_ref[...], sq_ref[...]
    for hd in range(H):
        a1 = q[:, hd * DK:hd * DK + HALF]
        a2 = q[:, hd * DK + HALF:(hd + 1) * DK]
        q_ref[:, hd * DK:hd * DK + HALF] = (a1 * cq - a2 * sq).astype(BF16)
        q_ref[:, hd * DK + HALF:(hd + 1) * DK] = (a1 * sq + a2 * cq).astype(BF16)

    kT = lax.dot_general(wkT_ref[...], h, (((1,), (1,)), ((), ())), preferred_element_type=F32)
    ck, sk = ck_ref[...], sk_ref[...]
    for hd in range(H):
        a1 = kT[hd * DK:hd * DK + HALF, :]
        a2 = kT[hd * DK + HALF:(hd + 1) * DK, :]
        kT_ref[hd * DK:hd * DK + HALF, :] = (a1 * ck - a2 * sk).astype(BF16)
        kT_ref[hd * DK + HALF:(hd + 1) * DK, :] = (a1 * sk + a2 * ck).astype(BF16)

    v = _dot(h, w_ref[:, _OV:_OV + VW]).astype(BF16)
    v_ref[...] = v
    for hd in range(H):
        S = S_ref[hd]
        sb_ref[hd, 0] = S.astype(BF16)
        kz = (kT_ref[hd * DK:(hd + 1) * DK, :].astype(F32) * ZE_ref[hd]).astype(BF16)
        g_chunk = jnp.exp(jnp.full((1, DV), lg_ref[1, hd] * C, F32))
        S_ref[hd] = S * g_chunk + _dot(kz, v[:, hd * DV:(hd + 1) * DV])

    g = _dot(h, w_ref[:, _OG:_OG + VW])
    sg_ref[...] = (g * jax.nn.sigmoid(g) * gnw_ref[...]).astype(BF16)

    f = _dot(h, w_ref[:, _OF:_OF + D]).astype(BF16)
    for gi in range(NG):
        z = _dot(f[:, gi * DG:(gi + 1) * DG], md_ref[...])
        G_ref[0, :, gi * DG:(gi + 1) * DG] = z[:, :DG].astype(BF16)
        G_ref[1, :, gi * DG:(gi + 1) * DG] = z[:, DG:].astype(BF16)

    sga_ref[...] = jax.nn.sigmoid(_dot(h, w_ref[:, _OGA:_OGA + D])).astype(BF16)
    sgf_ref[...] = jax.nn.sigmoid(_dot(h, w_ref[:, _OGF:_OGF + D])).astype(BF16)


def _proj(lg, x1, mods, nw, wall, wkT, md, gnw, cq, sq, ck, sk, s_b0, cast):
    tm = C
    n = L // tm
    cast_specs, cast_shapes = _cast_specs(cast, n)
    row = lambda i: (n - 1 - i, 0)
    col = lambda i: (0, n - 1 - i)
    return pl.pallas_call(
        _proj_kernel,
        grid=(n,),
        in_specs=[
            pl.BlockSpec(memory_space=pltpu.SMEM),
            pl.BlockSpec((tm, D), row),
            _mod_spec(3), _mod_spec(4),
            _resident((1, D)),
            _resident((D, _WIN)),
            _resident((QK, D)),
            _resident((DG, 2 * DG)),
            _resident((1, VW)),
            pl.BlockSpec((tm, HALF), row), pl.BlockSpec((tm, HALF), row),
            pl.BlockSpec((HALF, tm), col), pl.BlockSpec((HALF, tm), col),
            _resident((H, DK, DV)),
        ] + cast_specs,
        out_specs=[
            pl.BlockSpec((tm, QK), row),
            pl.BlockSpec((QK, tm), col),
            pl.BlockSpec((tm, VW), row),
            pl.BlockSpec((tm, VW), row),
            pl.BlockSpec((2, tm, D), lambda i: (0, n - 1 - i, 0)),
            pl.BlockSpec((tm, D), row),
            pl.BlockSpec((tm, D), row),
            pl.BlockSpec((H, 1, DK, DV), lambda i: (0, n - 1 - i, 0, 0)),
        ] + cast_specs,
        out_shape=[
            jax.ShapeDtypeStruct((L, QK), BF16),
            jax.ShapeDtypeStruct((QK, L), BF16),
            jax.ShapeDtypeStruct((L, VW), BF16),
            jax.ShapeDtypeStruct((L, VW), BF16),
            jax.ShapeDtypeStruct((2, L, D), BF16),
            jax.ShapeDtypeStruct((L, D), BF16),
            jax.ShapeDtypeStruct((L, D), BF16),
            jax.ShapeDtypeStruct((H, L // C, DK, DV), BF16),
        ] + cast_shapes,
        scratch_shapes=[pltpu.VMEM((H, DK, DV), F32), pltpu.VMEM((H, DK, C), F32)],
        compiler_params=_cparams(1),
        name="proj",
    )(lg, x1, mods, mods, nw, wall, wkT, md, gnw, cq, sq, ck, sk, s_b0, *cast)


FFT_NT = 16
FFT_KB = 16


def _fft1_kernel(m_ref, g_ref, o_ref):
    z = jnp.transpose(g_ref[...], (2, 0, 1, 3)).reshape(FFT_NT, 2 * N1, D)
    for t in range(FFT_NT):
        o_ref[t] = _dot(m_ref[...], z[t]).astype(BF16)


def _fft1(m1, g4):
    return pl.pallas_call(
        _fft1_kernel,
        grid=(N2 // FFT_NT,),
        in_specs=[_resident((2 * N1, 2 * N1)),
                  pl.BlockSpec((2, N1, FFT_NT, D), lambda j: (0, 0, j, 0))],
        out_specs=pl.BlockSpec((FFT_NT, 2 * N1, D), lambda j: (j, 0, 0)),
        out_shape=jax.ShapeDtypeStruct((N2, 2 * N1, D), BF16),
        compiler_params=_cparams(1),
        name="fft1",
    )(m1, g4)


def _fft3_kernel(t_ref, pf_ref, a_ref, o_ref, xr_ref):
    z = jnp.transpose(a_ref[...], (2, 1, 0, 3)).reshape(FFT_KB, 2 * N2, D)
    for i in range(FFT_KB):
        xr_ref[i * N2:(i + 1) * N2, :] = _dot(t_ref[i], z[i]).astype(BF16)
    fo = _dot(xr_ref[...], pf_ref[...]).astype(BF16).reshape(FFT_KB, N2, D)
    o_ref[...] = jnp.transpose(fo, (1, 0, 2))


def _fft3(t3, a4, pf):
    return pl.pallas_call(
        _fft3_kernel,
        grid=(N1 // FFT_KB,),
        in_specs=[pl.BlockSpec((FFT_KB, N2, 2 * N2), lambda j: (j, 0, 0)),
                  _resident((D, D)),
                  pl.BlockSpec((N2, 2, FFT_KB, D), lambda j: (0, 0, j, 0))],
        out_specs=pl.BlockSpec((N2, FFT_KB, D), lambda j: (0, j, 0)),
        out_shape=jax.ShapeDtypeStruct((N2, N1, D), BF16),
        scratch_shapes=[pltpu.VMEM((FFT_KB * N2, D), BF16)],
        compiler_params=_cparams(1),
        name="fft3",
    )(t3, pf, a4)


TB_R = 512


def _retmix_kernel(lg_ref, q_ref, kT_ref, v_ref, s0_ref, sb_ref, sgw_ref, x_ref, g_ref, fo_ref, sga_ref, sgf_ref,
                   pr_ref, wo_ref, o_ref, S_ref, D_ref, XF_ref, XB_ref, ZE_ref, rin_ref):
    @pl.when(pl.program_id(0) == 0)
    def _():
        S_ref[...] = s0_ref[...]
        r = lax.broadcasted_iota(jnp.int32, (C, C), 0).astype(F32)
        c = lax.broadcasted_iota(jnp.int32, (C, C), 1).astype(F32)
        d = r - c
        for hd in range(H):
            lgf, lgb = lg_ref[0, hd], lg_ref[1, hd]
            D_ref[hd] = (jnp.where(d >= 0, jnp.exp(lgf * jnp.maximum(d, 0.0)), 0.0)
                         + jnp.where(d <= 0, jnp.exp(lgb * jnp.maximum(-d, 0.0)), 0.0))
            XF_ref[hd] = jnp.exp(lgf * (r + 1.0))
            XB_ref[hd] = jnp.exp(lgb * (C - r))
            ZE_ref[hd] = jnp.exp(lgf * (C - 1.0 - c))

    for hd in range(H):
        g_chunk = jnp.exp(jnp.full((1, DV), lg_ref[0, hd] * C, F32))
        for ci in range(TB_R // C):
            rows = slice(ci * C, (ci + 1) * C)
            qc = q_ref[rows, hd * DK:(hd + 1) * DK]
            kTc = kT_ref[hd * DK:(hd + 1) * DK, rows]
            vc = v_ref[rows, hd * DV:(hd + 1) * DV]
            S = S_ref[hd]
            p = (_dot(qc, kTc) * D_ref[hd]).astype(BF16)
            qf = (qc.astype(F32) * XF_ref[hd]).astype(BF16)
            qb = (qc.astype(F32) * XB_ref[hd]).astype(BF16)
            o = _dot(p, vc) + _dot(qf, S.astype(BF16)) + _dot(qb, sb_ref[hd, ci])
            kz = (kTc.astype(F32) * ZE_ref[hd]).astype(BF16)
            S_ref[hd] = S * g_chunk + _dot(kz, vc)
            mu = jnp.mean(o, axis=-1, keepdims=True)
            d = o - mu
            var = jnp.mean(d * d, axis=-1, keepdims=True)
            on = d * lax.rsqrt(var + EPS)
            rin_ref[rows, hd * DV:(hd + 1) * DV] = (sgw_ref[rows, hd * DV:(hd + 1) * DV].astype(F32) * on).astype(BF16)

    r = _dot(rin_ref[...], pr_ref[...])
    m = sga_ref[...].astype(F32) * r + sgf_ref[...].astype(F32) * fo_ref[...].astype(F32)
    y = _dot(m.astype(BF16), wo_ref[...])
    o_ref[...] = x_ref[...] + g_ref[0:1, :] * y


def _retmix(lg, q, kT, v, s0, sb, sgw, x1, mods, fo, sga, sgf, p_ret, w_out):
    tb = TB_R
    row = lambda i: (i, 0)
    return pl.pallas_call(
        _retmix_kernel,
        grid=(L // tb,),
        in_specs=[
            pl.BlockSpec(memory_space=pltpu.SMEM),
            pl.BlockSpec((tb, QK), row),
            pl.BlockSpec((QK, tb), lambda i: (0, i)),
            pl.BlockSpec((tb, VW), row),
            _resident((H, DK, DV)),
            pl.BlockSpec((H, tb // C, DK, DV), lambda i: (0, i, 0, 0)),
            pl.BlockSpec((tb, VW), row),
            pl.BlockSpec((tb, D), row),
            _mod_spec(5),
            pl.BlockSpec((tb, D), row), pl.BlockSpec((tb, D), row), pl.BlockSpec((tb, D), row),
            _resident((VW, D)), _resident((D, D)),
        ],
        out_specs=pl.BlockSpec((tb, D), row),
        out_shape=jax.ShapeDtypeStruct((L, D), F32),
        scratch_shapes=[pltpu.VMEM((H, DK, DV), F32)] + [pltpu.VMEM((H, C, C), F32)] * 4
                       + [pltpu.VMEM((tb, VW), BF16)],
        compiler_params=_cparams(1),
        name="retmix",
    )(lg, q, kT, v, s0, sb, sgw, x1, mods, fo, sga, sgf, p_ret, w_out)


def _rope_tables():
    rows = L // GRID_W
    row = np.repeat(np.arange(rows, dtype=np.float64), GRID_W)
    col = np.tile(np.arange(GRID_W, dtype=np.float64), rows)
    n_freq = DK // 4
    inv = ROPE_BASE ** (-np.arange(n_freq, dtype=np.float64) / n_freq)
    ang = np.concatenate([row[:, None] * inv, col[:, None] * inv], axis=-1)
    return np.cos(ang), np.sin(ang)


def _dft_tables():
    two_pi = 2.0 * np.pi
    th = (np.outer(np.arange(DG), np.arange(DG)) % DG) * (two_pi / DG)
    md = np.concatenate([np.cos(th), -np.sin(th)], axis=1)
    th = (np.outer(np.arange(N1), np.arange(N1)) % N1) * (two_pi / N1)
    c1, s1 = np.cos(th), np.sin(th)
    m1 = np.block([[c1, s1], [-s1, c1]])
    k = np.arange(N1)[:, None, None] + N1 * np.arange(N2)[None, :, None]
    th = ((k * np.arange(N2)[None, None, :]) % L) * (two_pi / L)
    t3 = np.concatenate([np.cos(th), np.sin(th)], axis=-1) / np.sqrt(float(L) * DG)
    f32 = lambda a: jnp.asarray(a.astype(np.float32))
    return f32(md).astype(BF16), f32(m1).astype(BF16), f32(t3).astype(BF16)


def kernel(x, c, ctx, c_ctx, w_ada, b_ada, norm_ffn1, w13_ffn1, w2_ffn1, norm_mix, w_in, decay_fwd, decay_bwd,
           ret_gn_w, p_ret, p_four, w_out, norm_ffn2, w13_ffn2, w2_ffn2, norm_final):
    assert x.shape == (1, L, D) and ctx.shape == (1, CTX, D) and w_ada.shape[0] == 1

    pm = np.zeros((DK, DK), np.float32)
    pm[np.concatenate([np.arange(0, DK, 2), np.arange(1, DK, 2)]), np.arange(DK)] = 1.0
    wall, wkT = _prep(w_in[0], jnp.asarray(pm).astype(BF16), jnp.asarray(np.ascontiguousarray(pm.T)).astype(BF16))
    w13a, w2a = w13_ffn1[0].astype(BF16), w2_ffn1[0].astype(BF16)

    lg = jnp.stack([jax.nn.log_sigmoid(decay_fwd[0].astype(F32)),
                    jax.nn.log_sigmoid(decay_bwd[0].astype(F32))])
    cos, sin = _rope_tables()
    qscale = DK ** -0.5
    f32c = lambda a: jnp.asarray(np.ascontiguousarray(a).astype(np.float32))
    cq, sq, ck, sk = f32c(cos * qscale), f32c(sin * qscale), f32c(cos.T), f32c(sin.T)
    md, m1, t3 = _dft_tables()

    cs = jnp.concatenate([c, c_ctx[None, :], jnp.zeros((6, D), F32)], axis=0).T
    mods = _mods(cs, w_ada[0], b_ada)

    x0 = x[0]
    x1, w13b, w2b = _ffn(x0, mods, 0, 0, norm_ffn1, w13a, w2a, cast=(w13_ffn2[0], w2_ffn2[0]))
    ctx1 = _ffn(ctx[0], mods, 1, 0, norm_ffn1, w13a, w2a)

    s_f, s_b = _ctx_state(lg, ctx1, mods, norm_mix, wkT, wall)

    q, kT, v, sgw, G, sga, sgf, sb, pr, pf, wo = _proj(lg, x1, mods, norm_mix, wall, wkT, md, ret_gn_w, cq, sq, ck, sk,
                                                     s_b, (p_ret[0], p_four[0], w_out[0]))
    a = _fft1(m1, G.reshape(2, N1, N2, D))
    fo = _fft3(t3, a.reshape(N2, 2, N1, D), pf).reshape(L, D)

    x2 = _retmix(lg, q, kT, v, s_f, sb, sgw, x1, mods, fo, sga, sgf, pr, wo)
    out = _ffn(x2, mods, 0, 6, norm_ffn2, w13b, w2b, nf=norm_final[None, :])
    return out[None]
```

```python
import functools

import numpy as np
import jax
import jax.numpy as jnp
from jax import lax
from jax.experimental import pallas as pl
from jax.experimental.pallas import tpu as pltpu

D = 1024
L = 16384
GRID_W = 64
CTX = 256
H = 4
DK = 256
DV = 512
QK = H * DK
VW = H * DV
NG = 4
DG = 256
DFF = 2816
ROPE_BASE = 10000.0
EPS = 1e-6
NMOD = 9

F32 = jnp.float32
BF16 = jnp.bfloat16

C = 256
N1 = 128
N2 = 128

VMEM_LIMIT = 56 * 1024 * 1024


def _dot(a, b):
    return jnp.dot(a, b, preferred_element_type=F32)


def _cparams(n_axes, vmem_limit=VMEM_LIMIT):
    return pltpu.CompilerParams(
        dimension_semantics=("arbitrary",) * n_axes, vmem_limit_bytes=vmem_limit)


def _resident(shape):
    nd = len(shape)
    return pl.BlockSpec(shape, lambda *_: (0,) * nd, pipeline_mode=pl.Buffered(1))


def _mod_spec(j):
    return pl.BlockSpec((8, D), lambda *_: (0, j), pipeline_mode=pl.Buffered(1))


def _norm_mod(x, nw, sc, sh):
    y = x * lax.rsqrt(jnp.mean(x * x, axis=-1, keepdims=True) + EPS)
    return y * (nw * (1.0 + sc)) + sh


def _mods_kernel(cs_ref, w_ref, b_ref, o_ref):
    cs = cs_ref[...]
    s = cs * jax.nn.sigmoid(cs)
    w = w_ref[...]
    rows = [jnp.sum(w * s[:, r:r + 1], axis=0, keepdims=True) for r in range(2)]
    o_ref[...] = jnp.concatenate(rows + [jnp.zeros((6, w.shape[1]), F32)], axis=0) + b_ref[...]


def _mods(cs, w_ada, b_ada):
    tn = 2304
    return pl.pallas_call(
        _mods_kernel,
        grid=(NMOD * D // tn,),
        in_specs=[
            _resident((D, 8)),
            pl.BlockSpec((D, tn), lambda j: (0, j)),
            pl.BlockSpec((1, tn), lambda j: (0, j)),
        ],
        out_specs=pl.BlockSpec((8, tn), lambda j: (0, j)),
        out_shape=jax.ShapeDtypeStruct((8, NMOD * D), F32),
        compiler_params=_cparams(1),
        name="mods",
    )(cs, w_ada, b_ada)


def _prep_kernel(w_ref, p_ref, pT_ref, o_ref, kT_ref):
    j = pl.program_id(0)
    w = w_ref[...].astype(BF16)

    @pl.when(j >= 2)
    def _():
        o_ref[...] = w

    @pl.when(j < 2)
    def _():
        for hd in range(H):
            o_ref[:, hd * DK:(hd + 1) * DK] = _dot(w[:, hd * DK:(hd + 1) * DK], p_ref[...]).astype(BF16)

    @pl.when(j == 1)
    def _():
        for hd in range(H):
            kT_ref[hd * DK:(hd + 1) * DK, :] = lax.dot_general(
                pT_ref[...], w[:, hd * DK:(hd + 1) * DK], (((1,), (1,)), ((), ())),
                preferred_element_type=F32).astype(BF16)


def _prep(w_in, pm, pmT):
    tn = QK
    return pl.pallas_call(
        _prep_kernel,
        grid=(_WIN // tn,),
        in_specs=[pl.BlockSpec((D, tn), lambda j: (0, j)), _resident((DK, DK)), _resident((DK, DK))],
        out_specs=[pl.BlockSpec((D, tn), lambda j: (0, jnp.where(j == 1, _OK // tn, jnp.maximum(j - 1, 0)))),
                   pl.BlockSpec((QK, D), lambda j: (0, 0))],
        out_shape=[jax.ShapeDtypeStruct((D, _WIN), BF16), jax.ShapeDtypeStruct((QK, D), BF16)],
        compiler_params=_cparams(1),
        name="prep",
    )(w_in, pm, pmT)


FC = 256


def _cast_blocks(src_refs, dst_refs):
    for s, d in zip(src_refs, dst_refs):
        d[...] = s[...].astype(BF16)


def _cast_specs(arrays, steps):
    specs = [pl.BlockSpec((a.shape[0] // steps, a.shape[1]), lambda i: (i, 0)) for a in arrays]
    shapes = [jax.ShapeDtypeStruct(a.shape, BF16) for a in arrays]
    return specs, shapes


def _ffn_kernel(row, final, ncast, x_ref, sh_ref, sc_ref, g_ref, nw_ref, w13_ref, w2_ref, *rest):
    if final:
        nf_ref, rest = rest[0], rest[1:]
    o_ref, t_ref = rest[ncast], rest[-1]
    _cast_blocks(rest[:ncast], rest[ncast + 1:-1])
    x = x_ref[...]
    h = _norm_mod(x, nw_ref[...], sc_ref[row:row + 1, :], sh_ref[row:row + 1, :]).astype(BF16)
    for j in range(DFF // FC):
        a = _dot(h, w13_ref[:, j * FC:(j + 1) * FC])
        b = _dot(h, w13_ref[:, DFF + j * FC:DFF + (j + 1) * FC])
        t_ref[:, j * FC:(j + 1) * FC] = (a * jax.nn.sigmoid(a) * b).astype(BF16)
    y = _dot(t_ref[...], w2_ref[...])
    out = x + (0.5 * g_ref[row:row + 1, :]) * y
    if final:
        out = out * lax.rsqrt(jnp.mean(out * out, axis=-1, keepdims=True) + EPS) * nf_ref[...]
    o_ref[...] = out


def _ffn(x, mods, row, mod0, nw, w13, w2, nf=None, cast=(), tm=1024):
    n = x.shape[0]
    tm = min(tm, n)
    final = nf is not None
    cast_specs, cast_shapes = _cast_specs(cast, n // tm)
    in_specs = [
        pl.BlockSpec((tm, D), lambda i: (i, 0)),
        _mod_spec(mod0), _mod_spec(mod0 + 1), _mod_spec(mod0 + 2),
        _resident((1, D)),
        _resident((D, 2 * DFF)), _resident((DFF, D)),
    ]
    args = [x, mods, mods, mods, nw, w13, w2]
    if final:
        in_specs.append(_resident((1, D)))
        args.append(nf)
    outs = pl.pallas_call(
        functools.partial(_ffn_kernel, row, final, len(cast)),
        grid=(n // tm,),
        in_specs=in_specs + cast_specs,
        out_specs=[pl.BlockSpec((tm, D), lambda i: (i, 0))] + cast_specs,
        out_shape=[jax.ShapeDtypeStruct((n, D), F32)] + cast_shapes,
        scratch_shapes=[pltpu.VMEM((tm, DFF), BF16)],
        compiler_params=_cparams(1),
        name="ffn_final" if final else "ffn",
    )(*args, *cast)
    return outs if cast else outs[0]


def _ctx_state_kernel(lg_ref, x_ref, sh_ref, sc_ref, nw_ref, wkT_ref, wv_ref, sf_ref, sb_ref):
    hd = pl.program_id(0)
    h = _norm_mod(x_ref[...], nw_ref[...], sc_ref[1:2, :], sh_ref[1:2, :]).astype(BF16)
    kT = lax.dot_general(wkT_ref[...], h, (((1,), (1,)), ((), ())), preferred_element_type=F32)
    v = _dot(h, wv_ref[...]).astype(BF16)
    j = lax.broadcasted_iota(jnp.int32, (DK, CTX), 1).astype(F32)
    wf = jnp.exp(lg_ref[0, hd] * (CTX - 1.0 - j))
    wb = jnp.exp(lg_ref[1, hd] * j)
    sf_ref[0] = _dot((kT * wf).astype(BF16), v)
    sb_ref[0] = _dot((kT * wb).astype(BF16), v)


def _ctx_state(lg, ctx1, mods, nw, wkT, wall):
    return pl.pallas_call(
        _ctx_state_kernel,
        grid=(H,),
        in_specs=[
            pl.BlockSpec(memory_space=pltpu.SMEM),
            _resident((CTX, D)),
            _mod_spec(3), _mod_spec(4),
            _resident((1, D)),
            pl.BlockSpec((DK, D), lambda h: (h, 0)),
            pl.BlockSpec((D, DV), lambda h: (0, _OV // DV + h)),
        ],
        out_specs=[pl.BlockSpec((1, DK, DV), lambda h: (h, 0, 0))] * 2,
        out_shape=[jax.ShapeDtypeStruct((H, DK, DV), F32)] * 2,
        compiler_params=_cparams(1),
        name="ctx_state",
    )(lg, ctx1, mods, mods, nw, wkT, wall)


_OQ, _OV, _OG = 0, QK, QK + VW
_OF, _OGA, _OGF = QK + 2 * VW, QK + 2 * VW + D, QK + 2 * VW + 2 * D
_WIN = 2 * QK + 2 * VW + 3 * D
_OK = _WIN - QK
HALF = DK // 2

PT = 2
PROJ_VMEM_LIMIT = 58 * 1024 * 1024


def _proj_chunk(rows, ci, lg_ref, x_ref, sh_ref, sc_ref, nw_ref, w_ref, wkT_ref, md_ref, gnw_ref, cq_ref, sq_ref,
                ck_ref, sk_ref, q_ref, kT_ref, v_ref, sg_ref, G_ref, sga_ref, sgf_ref, sb_ref, S_ref, ZE_ref):
    h = _norm_mod(x_ref[rows, :], nw_ref[...], sc_ref[0:1, :], sh_ref[0:1, :]).astype(BF16)

    q = _dot(h, w_ref[:, _OQ:_OQ + QK])
    cq, sq = cq_ref[rows, :], sq_ref[rows, :]
    for hd in range(H):
        a1 = q[:, hd * DK:hd * DK + HALF]
        a2 = q[:, hd * DK + HALF:(hd + 1) * DK]
        q_ref[rows, hd * DK:hd * DK + HALF] = (a1 * cq - a2 * sq).astype(BF16)
        q_ref[rows, hd * DK + HALF:(hd + 1) * DK] = (a1 * sq + a2 * cq).astype(BF16)

    kT = lax.dot_general(wkT_ref[...], h, (((1,), (1,)), ((), ())), preferred_element_type=F32)
    ck, sk = ck_ref[:, rows], sk_ref[:, rows]
    for hd in range(H):
        a1 = kT[hd * DK:hd * DK + HALF, :]
        a2 = kT[hd * DK + HALF:(hd + 1) * DK, :]
        kT_ref[hd * DK:hd * DK + HALF, rows] = (a1 * ck - a2 * sk).astype(BF16)
        kT_ref[hd * DK + HALF:(hd + 1) * DK, rows] = (a1 * sk + a2 * ck).astype(BF16)

    v = _dot(h, w_ref[:, _OV:_OV + VW]).astype(BF16)
    v_ref[rows, :] = v
    for hd in range(H):
        S = S_ref[hd]
        sb_ref[hd, ci] = S.astype(BF16)
        kz = (kT_ref[hd * DK:(hd + 1) * DK, rows].astype(F32) * ZE_ref[hd]).astype(BF16)
        g_chunk = jnp.exp(jnp.full((1, DV), lg_ref[1, hd] * C, F32))
        S_ref[hd] = S * g_chunk + _dot(kz, v[:, hd * DV:(hd + 1) * DV])

    g = _dot(h, w_ref[:, _OG:_OG + VW])
    sg_ref[rows, :] = (g * jax.nn.sigmoid(g) * gnw_ref[...]).astype(BF16)

    f = _dot(h, w_ref[:, _OF:_OF + D]).astype(BF16)
    for gi in range(NG):
        z = _dot(f[:, gi * DG:(gi + 1) * DG], md_ref[...])
        G_ref[0, rows, gi * DG:(gi + 1) * DG] = z[:, :DG].astype(BF16)
        G_ref[1, rows, gi * DG:(gi + 1) * DG] = z[:, DG:].astype(BF16)

    sga_ref[rows, :] = jax.nn.sigmoid(_dot(h, w_ref[:, _OGA:_OGA + D])).astype(BF16)
    sgf_ref[rows, :] = jax.nn.sigmoid(_dot(h, w_ref[:, _OGF:_OGF + D])).astype(BF16)


def _proj_kernel(lg_ref, x_ref, sh_ref, sc_ref, nw_ref, w_ref, wkT_ref, md_ref, gnw_ref, cq_ref, sq_ref, ck_ref,
                 sk_ref, s0_ref, pr_ref, pf_ref, wo_ref, q_ref, kT_ref, v_ref, sg_ref, G_ref, sga_ref, sgf_ref, sb_ref,
                 prb_ref, pfb_ref, wob_ref, S_ref, ZE_ref):
    _cast_blocks((pr_ref, pf_ref, wo_ref), (prb_ref, pfb_ref, wob_ref))

    @pl.when(pl.program_id(0) == 0)
    def _():
        S_ref[...] = s0_ref[...]
        c = lax.broadcasted_iota(jnp.int32, (DK, C), 1).astype(F32)
        for hd in range(H):
            ZE_ref[hd] = jnp.exp(lg_ref[1, hd] * c)

    for ci in range(PT - 1, -1, -1):
        _proj_chunk(slice(ci * C, (ci + 1) * C), ci, lg_ref, x_ref, sh_ref, sc_ref, nw_ref, w_ref, wkT_ref, md_ref,
                    gnw_ref, cq_ref, sq_ref, ck_ref, sk_ref, q_ref, kT_ref, v_ref, sg_ref, G_ref, sga_ref, sgf_ref,
                    sb_ref, S_ref, ZE_ref)


def _proj(lg, x1, mods, nw, wall, wkT, md, gnw, cq, sq, ck, sk, s_b0, cast):
    tm = PT * C
    n = L // tm
    cast_specs, cast_shapes = _cast_specs(cast, n)
    row = lambda i: (n - 1 - i, 0)
    col = lambda i: (0, n - 1 - i)
    return pl.pallas_call(
        _proj_kernel,
        grid=(n,),
        in_specs=[
            pl.BlockSpec(memory_space=pltpu.SMEM),
            pl.BlockSpec((tm, D), row),
            _mod_spec(3), _mod_spec(4),
            _resident((1, D)),
            _resident((D, _OK)),
            _resident((QK, D)),
            _resident((DG, 2 * DG)),
            _resident((1, VW)),
            pl.BlockSpec((tm, HALF), row), pl.BlockSpec((tm, HALF), row),
            pl.BlockSpec((HALF, tm), col), pl.BlockSpec((HALF, tm), col),
            _resident((H, DK, DV)),
        ] + cast_specs,
        out_specs=[
            pl.BlockSpec((tm, QK), row),
            pl.BlockSpec((QK, tm), col),
            pl.BlockSpec((tm, VW), row),
            pl.BlockSpec((tm, VW), row),
            pl.BlockSpec((2, tm, D), lambda i: (0, n - 1 - i, 0)),
            pl.BlockSpec((tm, D), row),
            pl.BlockSpec((tm, D), row),
            pl.BlockSpec((H, PT, DK, DV), lambda i: (0, n - 1 - i, 0, 0)),
        ] + cast_specs,
        out_shape=[
            jax.ShapeDtypeStruct((L, QK), BF16),
            jax.ShapeDtypeStruct((QK, L), BF16),
            jax.ShapeDtypeStruct((L, VW), BF16),
            jax.ShapeDtypeStruct((L, VW), BF16),
            jax.ShapeDtypeStruct((2, L, D), BF16),
            jax.ShapeDtypeStruct((L, D), BF16),
            jax.ShapeDtypeStruct((L, D), BF16),
            jax.ShapeDtypeStruct((H, L // C, DK, DV), BF16),
        ] + cast_shapes,
        scratch_shapes=[pltpu.VMEM((H, DK, DV), F32), pltpu.VMEM((H, DK, C), F32)],
        compiler_params=_cparams(1, PROJ_VMEM_LIMIT),
        name="proj",
    )(lg, x1, mods, mods, nw, wall, wkT, md, gnw, cq, sq, ck, sk, s_b0, *cast)


FFT_NT = 16
FFT_KB = 16


def _fft1_kernel(m_ref, g_ref, o_ref):
    z = jnp.transpose(g_ref[...], (2, 0, 1, 3)).reshape(FFT_NT, 2 * N1, D)
    for t in range(FFT_NT):
        o_ref[t] = _dot(m_ref[...], z[t]).astype(BF16)


def _fft1(m1, g4):
    return pl.pallas_call(
        _fft1_kernel,
        grid=(N2 // FFT_NT,),
        in_specs=[_resident((2 * N1, 2 * N1)),
                  pl.BlockSpec((2, N1, FFT_NT, D), lambda j: (0, 0, j, 0))],
        out_specs=pl.BlockSpec((FFT_NT, 2 * N1, D), lambda j: (j, 0, 0)),
        out_shape=jax.ShapeDtypeStruct((N2, 2 * N1, D), BF16),
        compiler_params=_cparams(1),
        name="fft1",
    )(m1, g4)


def _fft3_kernel(t_ref, pf_ref, a_ref, o_ref, xr_ref):
    z = jnp.transpose(a_ref[...], (2, 1, 0, 3)).reshape(FFT_KB, 2 * N2, D)
    for i in range(FFT_KB):
        xr_ref[i * N2:(i + 1) * N2, :] = _dot(t_ref[i], z[i]).astype(BF16)
    fo = _dot(xr_ref[...], pf_ref[...]).astype(BF16).reshape(FFT_KB, N2, D)
    o_ref[...] = jnp.transpose(fo, (1, 0, 2))


def _fft3(t3, a4, pf):
    return pl.pallas_call(
        _fft3_kernel,
        grid=(N1 // FFT_KB,),
        in_specs=[pl.BlockSpec((FFT_KB, N2, 2 * N2), lambda j: (j, 0, 0)),
                  _resident((D, D)),
                  pl.BlockSpec((N2, 2, FFT_KB, D), lambda j: (0, 0, j, 0))],
        out_specs=pl.BlockSpec((N2, FFT_KB, D), lambda j: (0, j, 0)),
        out_shape=jax.ShapeDtypeStruct((N2, N1, D), BF16),
        scratch_shapes=[pltpu.VMEM((FFT_KB * N2, D), BF16)],
        compiler_params=_cparams(1),
        name="fft3",
    )(t3, pf, a4)


TB_R = 512


def _retmix_kernel(lg_ref, q_ref, kT_ref, v_ref, s0_ref, sb_ref, sgw_ref, x_ref, g_ref, fo_ref, sga_ref, sgf_ref,
                   pr_ref, wo_ref, o_ref, S_ref, D_ref, XF_ref, XB_ref, ZE_ref, rin_ref):
    @pl.when(pl.program_id(0) == 0)
    def _():
        S_ref[...] = s0_ref[...]
        r = lax.broadcasted_iota(jnp.int32, (C, C), 0).astype(F32)
        c = lax.broadcasted_iota(jnp.int32, (C, C), 1).astype(F32)
        d = r - c
        for hd in range(H):
            lgf, lgb = lg_ref[0, hd], lg_ref[1, hd]
            D_ref[hd] = (jnp.where(d >= 0, jnp.exp(lgf * jnp.maximum(d, 0.0)), 0.0)
                         + jnp.where(d <= 0, jnp.exp(lgb * jnp.maximum(-d, 0.0)), 0.0))
            XF_ref[hd] = jnp.exp(lgf * (r + 1.0))
            XB_ref[hd] = jnp.exp(lgb * (C - r))
            ZE_ref[hd] = jnp.exp(lgf * (C - 1.0 - c))

    for hd in range(H):
        g_chunk = jnp.exp(jnp.full((1, DV), lg_ref[0, hd] * C, F32))
        for ci in range(TB_R // C):
            rows = slice(ci * C, (ci + 1) * C)
            qc = q_ref[rows, hd * DK:(hd + 1) * DK]
            kTc = kT_ref[hd * DK:(hd + 1) * DK, rows]
            vc = v_ref[rows, hd * DV:(hd + 1) * DV]
            S = S_ref[hd]
            p = (_dot(qc, kTc) * D_ref[hd]).astype(BF16)
            qf = (qc.astype(F32) * XF_ref[hd]).astype(BF16)
            qb = (qc.astype(F32) * XB_ref[hd]).astype(BF16)
            o = _dot(p, vc) + _dot(qf, S.astype(BF16)) + _dot(qb, sb_ref[hd, ci])
            kz = (kTc.astype(F32) * ZE_ref[hd]).astype(BF16)
            S_ref[hd] = S * g_chunk + _dot(kz, vc)
            mu = jnp.mean(o, axis=-1, keepdims=True)
            d = o - mu
            var = jnp.mean(d * d, axis=-1, keepdims=True)
            on = d * lax.rsqrt(var + EPS)
            rin_ref[rows, hd * DV:(hd + 1) * DV] = (sgw_ref[rows, hd * DV:(hd + 1) * DV].astype(F32) * on).astype(BF16)

    r = _dot(rin_ref[...], pr_ref[...])
    m = sga_ref[...].astype(F32) * r + sgf_ref[...].astype(F32) * fo_ref[...].astype(F32)
    y = _dot(m.astype(BF16), wo_ref[...])
    o_ref[...] = x_ref[...] + g_ref[0:1, :] * y


def _retmix(lg, q, kT, v, s0, sb, sgw, x1, mods, fo, sga, sgf, p_ret, w_out):
    tb = TB_R
    row = lambda i: (i, 0)
    return pl.pallas_call(
        _retmix_kernel,
        grid=(L // tb,),
        in_specs=[
            pl.BlockSpec(memory_space=pltpu.SMEM),
            pl.BlockSpec((tb, QK), row),
            pl.BlockSpec((QK, tb), lambda i: (0, i)),
            pl.BlockSpec((tb, VW), row),
            _resident((H, DK, DV)),
            pl.BlockSpec((H, tb // C, DK, DV), lambda i: (0, i, 0, 0)),
            pl.BlockSpec((tb, VW), row),
            pl.BlockSpec((tb, D), row),
            _mod_spec(5),
            pl.BlockSpec((tb, D), row), pl.BlockSpec((tb, D), row), pl.BlockSpec((tb, D), row),
            _resident((VW, D)), _resident((D, D)),
        ],
        out_specs=pl.BlockSpec((tb, D), row),
        out_shape=jax.ShapeDtypeStruct((L, D), F32),
        scratch_shapes=[pltpu.VMEM((H, DK, DV), F32)] + [pltpu.VMEM((H, C, C), F32)] * 4
                       + [pltpu.VMEM((tb, VW), BF16)],
        compiler_params=_cparams(1),
        name="retmix",
    )(lg, q, kT, v, s0, sb, sgw, x1, mods, fo, sga, sgf, p_ret, w_out)


def _rope_tables():
    rows = L // GRID_W
    row = np.repeat(np.arange(rows, dtype=np.float64), GRID_W)
    col = np.tile(np.arange(GRID_W, dtype=np.float64), rows)
    n_freq = DK // 4
    inv = ROPE_BASE ** (-np.arange(n_freq, dtype=np.float64) / n_freq)
    ang = np.concatenate([row[:, None] * inv, col[:, None] * inv], axis=-1)
    return np.cos(ang), np.sin(ang)


def _dft_tables():
    two_pi = 2.0 * np.pi
    th = (np.outer(np.arange(DG), np.arange(DG)) % DG) * (two_pi / DG)
    md = np.concatenate([np.cos(th), -np.sin(th)], axis=1)
    th = (np.outer(np.arange(N1), np.arange(N1)) % N1) * (two_pi / N1)
    c1, s1 = np.cos(th), np.sin(th)
    m1 = np.block([[c1, s1], [-s1, c1]])
    k = np.arange(N1)[:, None, None] + N1 * np.arange(N2)[None, :, None]
    th = ((k * np.arange(N2)[None, None, :]) % L) * (two_pi / L)
    t3 = np.concatenate([np.cos(th), np.sin(th)], axis=-1) / np.sqrt(float(L) * DG)
    f32 = lambda a: jnp.asarray(a.astype(np.float32))
    return f32(md).astype(BF16), f32(m1).astype(BF16), f32(t3).astype(BF16)


def kernel(x, c, ctx, c_ctx, w_ada, b_ada, norm_ffn1, w13_ffn1, w2_ffn1, norm_mix, w_in, decay_fwd, decay_bwd,
           ret_gn_w, p_ret, p_four, w_out, norm_ffn2, w13_ffn2, w2_ffn2, norm_final):
    assert x.shape == (1, L, D) and ctx.shape == (1, CTX, D) and w_ada.shape[0] == 1

    pm = np.zeros((DK, DK), np.float32)
    pm[np.concatenate([np.arange(0, DK, 2), np.arange(1, DK, 2)]), np.arange(DK)] = 1.0
    wall, wkT = _prep(w_in[0], jnp.asarray(pm).astype(BF16), jnp.asarray(np.ascontiguousarray(pm.T)).astype(BF16))
    w13a, w2a = w13_ffn1[0].astype(BF16), w2_ffn1[0].astype(BF16)

    lg = jnp.stack([jax.nn.log_sigmoid(decay_fwd[0].astype(F32)),
                    jax.nn.log_sigmoid(decay_bwd[0].astype(F32))])
    cos, sin = _rope_tables()
    qscale = DK ** -0.5
    f32c = lambda a: jnp.asarray(np.ascontiguousarray(a).astype(np.float32))
    cq, sq, ck, sk = f32c(cos * qscale), f32c(sin * qscale), f32c(cos.T), f32c(sin.T)
    md, m1, t3 = _dft_tables()

    cs = jnp.concatenate([c, c_ctx[None, :], jnp.zeros((6, D), F32)], axis=0).T
    mods = _mods(cs, w_ada[0], b_ada)

    x0 = x[0]
    x1, w13b, w2b = _ffn(x0, mods, 0, 0, norm_ffn1, w13a, w2a, cast=(w13_ffn2[0], w2_ffn2[0]))
    ctx1 = _ffn(ctx[0], mods, 1, 0, norm_ffn1, w13a, w2a)

    s_f, s_b = _ctx_state(lg, ctx1, mods, norm_mix, wkT, wall)

    q, kT, v, sgw, G, sga, sgf, sb, pr, pf, wo = _proj(lg, x1, mods, norm_mix, wall, wkT, md, ret_gn_w, cq, sq, ck, sk,
                                                     s_b, (p_ret[0], p_four[0], w_out[0]))
    a = _fft1(m1, G.reshape(2, N1, N2, D))
    fo = _fft3(t3, a.reshape(N2, 2, N1, D), pf).reshape(L, D)

    x2 = _retmix(lg, q, kT, v, s_f, sb, sgw, x1, mods, fo, sga, sgf, pr, wo)
    out = _ffn(x2, mods, 0, 6, norm_ffn2, w13b, w2b, nf=norm_final[None, :])
    return out[None]
```

```python
import functools

import numpy as np
import jax
import jax.numpy as jnp
from jax import lax
from jax.experimental import pallas as pl
from jax.experimental.pallas import tpu as pltpu

D = 1024
L = 16384
GRID_W = 64
CTX = 256
H = 4
DK = 256
DV = 512
QK = H * DK
VW = H * DV
NG = 4
DG = 256
DFF = 2816
ROPE_BASE = 10000.0
EPS = 1e-6
NMOD = 9

F32 = jnp.float32
BF16 = jnp.bfloat16

C = 256
N1 = 128
N2 = 128

VMEM_LIMIT = 56 * 1024 * 1024


def _dot(a, b):
    return jnp.dot(a, b, preferred_element_type=F32)


def _cparams(n_axes, vmem_limit=VMEM_LIMIT):
    return pltpu.CompilerParams(
        dimension_semantics=("arbitrary",) * n_axes, vmem_limit_bytes=vmem_limit)


def _resident(shape):
    nd = len(shape)
    return pl.BlockSpec(shape, lambda *_: (0,) * nd, pipeline_mode=pl.Buffered(1))


def _mod_spec(j):
    return pl.BlockSpec((8, D), lambda *_: (0, j), pipeline_mode=pl.Buffered(1))


NMOD_A = 3


def _mod(tables, j):
    return (tables[0], _mod_spec(j)) if j < NMOD_A else (tables[1], _mod_spec(j - NMOD_A))


def _norm_mod(x, nw, sc, sh):
    y = x * lax.rsqrt(jnp.mean(x * x, axis=-1, keepdims=True) + EPS)
    return y * (nw * (1.0 + sc)) + sh


def _mods_block(w_ref, cs_ref, b_ref, o_ref):
    cs = cs_ref[...]
    s = cs * jax.nn.sigmoid(cs)
    w = w_ref[...]
    rows = [jnp.sum(w * s[:, r:r + 1], axis=0, keepdims=True) for r in range(2)]
    o_ref[...] = jnp.concatenate(rows + [jnp.zeros((6, w.shape[1]), F32)], axis=0) + b_ref[...]


def _mods_kernel(cs_ref, w_ref, b_ref, o_ref):
    _mods_block(w_ref, cs_ref, b_ref, o_ref)


def _mods(cs, w_ada, b_ada):
    tn = NMOD_A * D // 2
    return pl.pallas_call(
        _mods_kernel,
        grid=(2,),
        in_specs=[
            _resident((D, 8)),
            pl.BlockSpec((D, tn), lambda j: (0, j)),
            pl.BlockSpec((1, tn), lambda j: (0, j)),
        ],
        out_specs=pl.BlockSpec((8, tn), lambda j: (0, j)),
        out_shape=jax.ShapeDtypeStruct((8, NMOD_A * D), F32),
        compiler_params=_cparams(1),
        name="mods",
    )(cs, w_ada, b_ada)


def _cast_blocks(src_refs, dst_refs):
    for s, d in zip(src_refs, dst_refs):
        d[...] = s[...].astype(BF16)


def _cast_specs(arrays, steps):
    specs = [pl.BlockSpec((a.shape[0] // steps, a.shape[1]), lambda i: (i, 0)) for a in arrays]
    shapes = [jax.ShapeDtypeStruct(a.shape, BF16) for a in arrays]
    return specs, shapes


def _prep_block(win_ref, p_ref, wall_ref):
    w = win_ref[...].astype(BF16)
    rb = w.shape[0]
    heads = jnp.concatenate([w[:, hd * DK:(hd + 1) * DK] for hd in range(2 * H)], axis=0)
    perm = _dot(heads, p_ref[...]).astype(BF16)
    for hd in range(H):
        wall_ref[:, _OQ + hd * DK:_OQ + (hd + 1) * DK] = perm[hd * rb:(hd + 1) * rb]
        wall_ref[:, _OK + hd * DK:_OK + (hd + 1) * DK] = perm[(H + hd) * rb:(H + hd + 1) * rb]
    wall_ref[:, QK:_OK] = w[:, 2 * QK:]


MODS_TN = 256


def _kT_kernel(eye_ref, w_ref, o_ref):
    o_ref[...] = lax.dot_general(eye_ref[...], w_ref[...], (((1,), (1,)), ((), ())),
                                 preferred_element_type=F32).astype(BF16)


def _kT(wall, eye):
    return pl.pallas_call(
        _kT_kernel,
        grid=(H,),
        in_specs=[_resident((DK, DK)), pl.BlockSpec((D, DK), lambda h: (0, _OK // DK + h))],
        out_specs=pl.BlockSpec((DK, D), lambda h: (h, 0)),
        out_shape=jax.ShapeDtypeStruct((QK, D), BF16),
        compiler_params=_cparams(1),
        name="kT",
    )(eye, wall)


FC = 256


def _ffn_kernel(row, final, ncast, side, x_ref, sh_ref, sc_ref, g_ref, nw_ref, w13_ref, w2_ref, *rest):
    if final:
        nf_ref, rest = rest[0], rest[1:]
    cast_in, rest = rest[:ncast], rest[ncast:]
    if side:
        (win_ref, p_ref, wada_ref, cs_ref, b_ref), rest = rest[:5], rest[5:]
    o_ref, rest = rest[0], rest[1:]
    cast_out, rest = rest[:ncast], rest[ncast:]
    t_ref = rest[-1]

    x = x_ref[...]
    h = _norm_mod(x, nw_ref[...], sc_ref[row:row + 1, :], sh_ref[row:row + 1, :]).astype(BF16)
    for j in range(DFF // FC):
        a = _dot(h, w13_ref[:, j * FC:(j + 1) * FC])
        b = _dot(h, w13_ref[:, DFF + j * FC:DFF + (j + 1) * FC])
        t_ref[:, j * FC:(j + 1) * FC] = (a * jax.nn.sigmoid(a) * b).astype(BF16)
    _cast_blocks(cast_in, cast_out)
    if side:
        _prep_block(win_ref, p_ref, rest[0])
        _mods_block(wada_ref, cs_ref, b_ref, rest[1])
    y = _dot(t_ref[...], w2_ref[...])
    out = x + (0.5 * g_ref[row:row + 1, :]) * y
    if final:
        out = out * lax.rsqrt(jnp.mean(out * out, axis=-1, keepdims=True) + EPS) * nf_ref[...]
    o_ref[...] = out


def _ffn(x, tables, row, mod0, nw, w13, w2, nf=None, cast=(), side=None, tm=1024):
    n = x.shape[0]
    tm = min(tm if side is None else tm // 2, n)
    steps = n // tm
    final = nf is not None
    cast_specs, cast_shapes = _cast_specs(cast, steps)
    mod_arrays, mod_specs = zip(*[_mod(tables, mod0 + k) for k in range(3)])
    in_specs = [pl.BlockSpec((tm, D), lambda i: (i, 0)), *mod_specs,
                _resident((1, D)), _resident((D, 2 * DFF)), _resident((DFF, D))]
    args = [x, *mod_arrays, nw, w13, w2]
    if final:
        in_specs.append(_resident((1, D)))
        args.append(nf)
    in_specs += cast_specs
    args += list(cast)
    out_specs = [pl.BlockSpec((tm, D), lambda i: (i, 0))] + cast_specs
    out_shape = [jax.ShapeDtypeStruct((n, D), F32)] + cast_shapes
    if side is not None:
        w_in, perm, w_ada, cs, b_ada = side
        rb = D // steps
        nlate = (NMOD - NMOD_A) * D
        first, last = NMOD_A * D // MODS_TN, NMOD * D // MODS_TN - 1
        late_in = lambda i: (0, jnp.minimum(first + i, last))
        late_out = lambda i: (0, jnp.minimum(i, last - first))
        in_specs += [pl.BlockSpec((rb, _WIN), lambda i: (i, 0)), _resident((DK, DK)),
                     pl.BlockSpec((D, MODS_TN), late_in), _resident((D, 8)), pl.BlockSpec((1, MODS_TN), late_in)]
        args += [w_in, perm, w_ada, cs, b_ada]
        out_specs += [pl.BlockSpec((rb, _WIN), lambda i: (i, 0)), pl.BlockSpec((8, MODS_TN), late_out)]
        out_shape += [jax.ShapeDtypeStruct((D, _WIN), BF16), jax.ShapeDtypeStruct((8, nlate), F32)]
    outs = pl.pallas_call(
        functools.partial(_ffn_kernel, row, final, len(cast), side is not None),
        grid=(steps,),
        in_specs=in_specs,
        out_specs=out_specs,
        out_shape=out_shape,
        scratch_shapes=[pltpu.VMEM((tm, DFF), BF16)],
        compiler_params=_cparams(1),
        name="ffn_final" if final else "ffn",
    )(*args)
    return outs if len(outs) > 1 else outs[0]


def _ctx_state_kernel(lg_ref, x_ref, sh_ref, sc_ref, nw_ref, wkT_ref, wv_ref, sf_ref, sb_ref):
    hd = pl.program_id(0)
    h = _norm_mod(x_ref[...], nw_ref[...], sc_ref[1:2, :], sh_ref[1:2, :]).astype(BF16)
    kT = lax.dot_general(wkT_ref[...], h, (((1,), (1,)), ((), ())), preferred_element_type=F32)
    v = _dot(h, wv_ref[...]).astype(BF16)
    j = lax.broadcasted_iota(jnp.int32, (DK, CTX), 1).astype(F32)
    wf = jnp.exp(lg_ref[0, hd] * (CTX - 1.0 - j))
    wb = jnp.exp(lg_ref[1, hd] * j)
    sf_ref[0] = _dot((kT * wf).astype(BF16), v)
    sb_ref[0] = _dot((kT * wb).astype(BF16), v)


def _ctx_state(lg, ctx1, tables, nw, wkT, wall):
    return pl.pallas_call(
        _ctx_state_kernel,
        grid=(H,),
        in_specs=[
            pl.BlockSpec(memory_space=pltpu.SMEM),
            _resident((CTX, D)),
            _mod(tables, 3)[1], _mod(tables, 4)[1],
            _resident((1, D)),
            pl.BlockSpec((DK, D), lambda h: (h, 0)),
            pl.BlockSpec((D, DV), lambda h: (0, _OV // DV + h)),
        ],
        out_specs=[pl.BlockSpec((1, DK, DV), lambda h: (h, 0, 0))] * 2,
        out_shape=[jax.ShapeDtypeStruct((H, DK, DV), F32)] * 2,
        compiler_params=_cparams(1),
        name="ctx_state",
    )(lg, ctx1, tables[1], tables[1], nw, wkT, wall)


_OQ, _OV, _OG = 0, QK, QK + VW
_OF, _OGA, _OGF = QK + 2 * VW, QK + 2 * VW + D, QK + 2 * VW + 2 * D
_WIN = 2 * QK + 2 * VW + 3 * D
_OK = _WIN - QK
HALF = DK // 2

PT = 2
PROJ_VMEM_LIMIT = 58 * 1024 * 1024


def _proj_chunk(rows, ci, lg_ref, x_ref, sh_ref, sc_ref, nw_ref, w_ref, wkT_ref, md_ref, gnw_ref, cq_ref, sq_ref,
                ck_ref, sk_ref, q_ref, kT_ref, v_ref, sg_ref, G_ref, sga_ref, sgf_ref, sb_ref, S_ref, ZE_ref):
    h = _norm_mod(x_ref[rows, :], nw_ref[...], sc_ref[0:1, :], sh_ref[0:1, :]).astype(BF16)

    q = _dot(h, w_ref[:, _OQ:_OQ + QK])
    cq, sq = cq_ref[rows, :], sq_ref[rows, :]
    for hd in range(H):
        a1 = q[:, hd * DK:hd * DK + HALF]
        a2 = q[:, hd * DK + HALF:(hd + 1) * DK]
        q_ref[rows, hd * DK:hd * DK + HALF] = (a1 * cq - a2 * sq).astype(BF16)
        q_ref[rows, hd * DK + HALF:(hd + 1) * DK] = (a1 * sq + a2 * cq).astype(BF16)

    kT = lax.dot_general(wkT_ref[...], h, (((1,), (1,)), ((), ())), preferred_element_type=F32)
    ck, sk = ck_ref[:, rows], sk_ref[:, rows]
    for hd in range(H):
        a1 = kT[hd * DK:hd * DK + HALF, :]
        a2 = kT[hd * DK + HALF:(hd + 1) * DK, :]
        kT_ref[hd * DK:hd * DK + HALF, rows] = (a1 * ck - a2 * sk).astype(BF16)
        kT_ref[hd * DK + HALF:(hd + 1) * DK, rows] = (a1 * sk + a2 * ck).astype(BF16)

    v = _dot(h, w_ref[:, _OV:_OV + VW]).astype(BF16)
    v_ref[rows, :] = v
    for hd in range(H):
        S = S_ref[hd]
        sb_ref[hd, ci] = S.astype(BF16)
        kz = (kT_ref[hd * DK:(hd + 1) * DK, rows].astype(F32) * ZE_ref[hd]).astype(BF16)
        g_chunk = jnp.exp(jnp.full((1, DV), lg_ref[1, hd] * C, F32))
        S_ref[hd] = S * g_chunk + _dot(kz, v[:, hd * DV:(hd + 1) * DV])

    g = _dot(h, w_ref[:, _OG:_OG + VW])
    sg_ref[rows, :] = (g * jax.nn.sigmoid(g) * gnw_ref[...]).astype(BF16)

    f = _dot(h, w_ref[:, _OF:_OF + D]).astype(BF16)
    for gi in range(NG):
        z = _dot(f[:, gi * DG:(gi + 1) * DG], md_ref[...])
        G_ref[0, rows, gi * DG:(gi + 1) * DG] = z[:, :DG].astype(BF16)
        G_ref[1, rows, gi * DG:(gi + 1) * DG] = z[:, DG:].astype(BF16)

    sga_ref[rows, :] = jax.nn.sigmoid(_dot(h, w_ref[:, _OGA:_OGA + D])).astype(BF16)
    sgf_ref[rows, :] = jax.nn.sigmoid(_dot(h, w_ref[:, _OGF:_OGF + D])).astype(BF16)


def _proj_kernel(lg_ref, x_ref, sh_ref, sc_ref, nw_ref, w_ref, wkT_ref, md_ref, gnw_ref, cq_ref, sq_ref, ck_ref,
                 sk_ref, s0_ref, pr_ref, pf_ref, wo_ref, q_ref, kT_ref, v_ref, sg_ref, G_ref, sga_ref, sgf_ref, sb_ref,
                 prb_ref, pfb_ref, wob_ref, S_ref, ZE_ref):
    _cast_blocks((pr_ref, pf_ref, wo_ref), (prb_ref, pfb_ref, wob_ref))

    @pl.when(pl.program_id(0) == 0)
    def _():
        S_ref[...] = s0_ref[...]
        c = lax.broadcasted_iota(jnp.int32, (DK, C), 1).astype(F32)
        for hd in range(H):
            ZE_ref[hd] = jnp.exp(lg_ref[1, hd] * c)

    for ci in range(PT - 1, -1, -1):
        _proj_chunk(slice(ci * C, (ci + 1) * C), ci, lg_ref, x_ref, sh_ref, sc_ref, nw_ref, w_ref, wkT_ref, md_ref,
                    gnw_ref, cq_ref, sq_ref, ck_ref, sk_ref, q_ref, kT_ref, v_ref, sg_ref, G_ref, sga_ref, sgf_ref,
                    sb_ref, S_ref, ZE_ref)


def _proj(lg, x1, tables, nw, wall, wkT, md, gnw, cq, sq, ck, sk, s_b0, cast):
    tm = PT * C
    n = L // tm
    cast_specs, cast_shapes = _cast_specs(cast, n)
    row = lambda i: (n - 1 - i, 0)
    col = lambda i: (0, n - 1 - i)
    return pl.pallas_call(
        _proj_kernel,
        grid=(n,),
        in_specs=[
            pl.BlockSpec(memory_space=pltpu.SMEM),
            pl.BlockSpec((tm, D), row),
            _mod(tables, 3)[1], _mod(tables, 4)[1],
            _resident((1, D)),
            _resident((D, _OK)),
            _resident((QK, D)),
            _resident((DG, 2 * DG)),
            _resident((1, VW)),
            pl.BlockSpec((tm, HALF), row), pl.BlockSpec((tm, HALF), row),
            pl.BlockSpec((HALF, tm), col), pl.BlockSpec((HALF, tm), col),
            _resident((H, DK, DV)),
        ] + cast_specs,
        out_specs=[
            pl.BlockSpec((tm, QK), row),
            pl.BlockSpec((QK, tm), col),
            pl.BlockSpec((tm, VW), row),
            pl.BlockSpec((tm, VW), row),
            pl.BlockSpec((2, tm, D), lambda i: (0, n - 1 - i, 0)),
            pl.BlockSpec((tm, D), row),
            pl.BlockSpec((tm, D), row),
            pl.BlockSpec((H, PT, DK, DV), lambda i: (0, n - 1 - i, 0, 0)),
        ] + cast_specs,
        out_shape=[
            jax.ShapeDtypeStruct((L, QK), BF16),
            jax.ShapeDtypeStruct((QK, L), BF16),
            jax.ShapeDtypeStruct((L, VW), BF16),
            jax.ShapeDtypeStruct((L, VW), BF16),
            jax.ShapeDtypeStruct((2, L, D), BF16),
            jax.ShapeDtypeStruct((L, D), BF16),
            jax.ShapeDtypeStruct((L, D), BF16),
            jax.ShapeDtypeStruct((H, L // C, DK, DV), BF16),
        ] + cast_shapes,
        scratch_shapes=[pltpu.VMEM((H, DK, DV), F32), pltpu.VMEM((H, DK, C), F32)],
        compiler_params=_cparams(1, PROJ_VMEM_LIMIT),
        name="proj",
    )(lg, x1, tables[1], tables[1], nw, wall, wkT, md, gnw, cq, sq, ck, sk, s_b0, *cast)


FFT_NT = 16
FFT_KB = 16


def _fft1_kernel(m_ref, g_ref, o_ref):
    z = jnp.transpose(g_ref[...], (2, 0, 1, 3)).reshape(FFT_NT, 2 * N1, D)
    for t in range(FFT_NT):
        o_ref[t] = _dot(m_ref[...], z[t]).astype(BF16)


def _fft1(m1, g4):
    return pl.pallas_call(
        _fft1_kernel,
        grid=(N2 // FFT_NT,),
        in_specs=[_resident((2 * N1, 2 * N1)),
                  pl.BlockSpec((2, N1, FFT_NT, D), lambda j: (0, 0, j, 0))],
        out_specs=pl.BlockSpec((FFT_NT, 2 * N1, D), lambda j: (j, 0, 0)),
        out_shape=jax.ShapeDtypeStruct((N2, 2 * N1, D), BF16),
        compiler_params=_cparams(1),
        name="fft1",
    )(m1, g4)


def _fft3_kernel(t_ref, pf_ref, a_ref, o_ref, xr_ref):
    z = jnp.transpose(a_ref[...], (2, 1, 0, 3)).reshape(FFT_KB, 2 * N2, D)
    for i in range(FFT_KB):
        xr_ref[i * N2:(i + 1) * N2, :] = _dot(t_ref[i], z[i]).astype(BF16)
    fo = _dot(xr_ref[...], pf_ref[...]).astype(BF16).reshape(FFT_KB, N2, D)
    o_ref[...] = jnp.transpose(fo, (1, 0, 2))


def _fft3(t3, a4, pf):
    return pl.pallas_call(
        _fft3_kernel,
        grid=(N1 // FFT_KB,),
        in_specs=[pl.BlockSpec((FFT_KB, N2, 2 * N2), lambda j: (j, 0, 0)),
                  _resident((D, D)),
                  pl.BlockSpec((N2, 2, FFT_KB, D), lambda j: (0, 0, j, 0))],
        out_specs=pl.BlockSpec((N2, FFT_KB, D), lambda j: (0, j, 0)),
        out_shape=jax.ShapeDtypeStruct((N2, N1, D), BF16),
        scratch_shapes=[pltpu.VMEM((FFT_KB * N2, D), BF16)],
        compiler_params=_cparams(1),
        name="fft3",
    )(t3, pf, a4)


TB_R = 512


def _retmix_kernel(lg_ref, q_ref, kT_ref, v_ref, s0_ref, sb_ref, sgw_ref, x_ref, g_ref, fo_ref, sga_ref, sgf_ref,
                   pr_ref, wo_ref, o_ref, S_ref, D_ref, XF_ref, XB_ref, ZE_ref, rin_ref):
    @pl.when(pl.program_id(0) == 0)
    def _():
        S_ref[...] = s0_ref[...]
        r = lax.broadcasted_iota(jnp.int32, (C, C), 0).astype(F32)
        c = lax.broadcasted_iota(jnp.int32, (C, C), 1).astype(F32)
        d = r - c
        for hd in range(H):
            lgf, lgb = lg_ref[0, hd], lg_ref[1, hd]
            D_ref[hd] = (jnp.where(d >= 0, jnp.exp(lgf * jnp.maximum(d, 0.0)), 0.0)
                         + jnp.where(d <= 0, jnp.exp(lgb * jnp.maximum(-d, 0.0)), 0.0))
            XF_ref[hd] = jnp.exp(lgf * (r + 1.0))
            XB_ref[hd] = jnp.exp(lgb * (C - r))
            ZE_ref[hd] = jnp.exp(lgf * (C - 1.0 - c))

    for hd in range(H):
        g_chunk = jnp.exp(jnp.full((1, DV), lg_ref[0, hd] * C, F32))
        for ci in range(TB_R // C):
            rows = slice(ci * C, (ci + 1) * C)
            qc = q_ref[rows, hd * DK:(hd + 1) * DK]
            kTc = kT_ref[hd * DK:(hd + 1) * DK, rows]
            vc = v_ref[rows, hd * DV:(hd + 1) * DV]
            S = S_ref[hd]
            p = (_dot(qc, kTc) * D_ref[hd]).astype(BF16)
            qf = (qc.astype(F32) * XF_ref[hd]).astype(BF16)
            qb = (qc.astype(F32) * XB_ref[hd]).astype(BF16)
            o = _dot(p, vc) + _dot(qf, S.astype(BF16)) + _dot(qb, sb_ref[hd, ci])
            kz = (kTc.astype(F32) * ZE_ref[hd]).astype(BF16)
            S_ref[hd] = S * g_chunk + _dot(kz, vc)
            mu = jnp.mean(o, axis=-1, keepdims=True)
            d = o - mu
            var = jnp.mean(d * d, axis=-1, keepdims=True)
            on = d * lax.rsqrt(var + EPS)
            rin_ref[rows, hd * DV:(hd + 1) * DV] = (sgw_ref[rows, hd * DV:(hd + 1) * DV].astype(F32) * on).astype(BF16)

    r = _dot(rin_ref[...], pr_ref[...])
    m = sga_ref[...].astype(F32) * r + sgf_ref[...].astype(F32) * fo_ref[...].astype(F32)
    y = _dot(m.astype(BF16), wo_ref[...])
    o_ref[...] = x_ref[...] + g_ref[0:1, :] * y


def _retmix(lg, q, kT, v, s0, sb, sgw, x1, tables, fo, sga, sgf, p_ret, w_out):
    tb = TB_R
    row = lambda i: (i, 0)
    return pl.pallas_call(
        _retmix_kernel,
        grid=(L // tb,),
        in_specs=[
            pl.BlockSpec(memory_space=pltpu.SMEM),
            pl.BlockSpec((tb, QK), row),
            pl.BlockSpec((QK, tb), lambda i: (0, i)),
            pl.BlockSpec((tb, VW), row),
            _resident((H, DK, DV)),
            pl.BlockSpec((H, tb // C, DK, DV), lambda i: (0, i, 0, 0)),
            pl.BlockSpec((tb, VW), row),
            pl.BlockSpec((tb, D), row),
            _mod(tables, 5)[1],
            pl.BlockSpec((tb, D), row), pl.BlockSpec((tb, D), row), pl.BlockSpec((tb, D), row),
            _resident((VW, D)), _resident((D, D)),
        ],
        out_specs=pl.BlockSpec((tb, D), row),
        out_shape=jax.ShapeDtypeStruct((L, D), F32),
        scratch_shapes=[pltpu.VMEM((H, DK, DV), F32)] + [pltpu.VMEM((H, C, C), F32)] * 4
                       + [pltpu.VMEM((tb, VW), BF16)],
        compiler_params=_cparams(1),
        name="retmix",
    )(lg, q, kT, v, s0, sb, sgw, x1, tables[1], fo, sga, sgf, p_ret, w_out)


def _rope_tables():
    rows = L // GRID_W
    row = np.repeat(np.arange(rows, dtype=np.float64), GRID_W)
    col = np.tile(np.arange(GRID_W, dtype=np.float64), rows)
    n_freq = DK // 4
    inv = ROPE_BASE ** (-np.arange(n_freq, dtype=np.float64) / n_freq)
    ang = np.concatenate([row[:, None] * inv, col[:, None] * inv], axis=-1)
    return np.cos(ang), np.sin(ang)


def _dft_tables():
    two_pi = 2.0 * np.pi
    th = (np.outer(np.arange(DG), np.arange(DG)) % DG) * (two_pi / DG)
    md = np.concatenate([np.cos(th), -np.sin(th)], axis=1)
    th = (np.outer(np.arange(N1), np.arange(N1)) % N1) * (two_pi / N1)
    c1, s1 = np.cos(th), np.sin(th)
    m1 = np.block([[c1, s1], [-s1, c1]])
    k = np.arange(N1)[:, None, None] + N1 * np.arange(N2)[None, :, None]
    th = ((k * np.arange(N2)[None, None, :]) % L) * (two_pi / L)
    t3 = np.concatenate([np.cos(th), np.sin(th)], axis=-1) / np.sqrt(float(L) * DG)
    f32 = lambda a: jnp.asarray(a.astype(np.float32))
    return f32(md).astype(BF16), f32(m1).astype(BF16), f32(t3).astype(BF16)


def kernel(x, c, ctx, c_ctx, w_ada, b_ada, norm_ffn1, w13_ffn1, w2_ffn1, norm_mix, w_in, decay_fwd, decay_bwd,
           ret_gn_w, p_ret, p_four, w_out, norm_ffn2, w13_ffn2, w2_ffn2, norm_final):
    assert x.shape == (1, L, D) and ctx.shape == (1, CTX, D) and w_ada.shape[0] == 1

    pm = np.zeros((DK, DK), np.float32)
    pm[np.concatenate([np.arange(0, DK, 2), np.arange(1, DK, 2)]), np.arange(DK)] = 1.0
    perm = jnp.asarray(pm).astype(BF16)
    eye = jnp.asarray(np.eye(DK, dtype=np.float32)).astype(BF16)
    w13a, w2a = w13_ffn1[0].astype(BF16), w2_ffn1[0].astype(BF16)

    lg = jnp.stack([jax.nn.log_sigmoid(decay_fwd[0].astype(F32)),
                    jax.nn.log_sigmoid(decay_bwd[0].astype(F32))])
    cos, sin = _rope_tables()
    qscale = DK ** -0.5
    f32c = lambda a: jnp.asarray(np.ascontiguousarray(a).astype(np.float32))
    cq, sq, ck, sk = f32c(cos * qscale), f32c(sin * qscale), f32c(cos.T), f32c(sin.T)
    md, m1, t3 = _dft_tables()

    cs = jnp.concatenate([c, c_ctx[None, :], jnp.zeros((6, D), F32)], axis=0).T
    mods_a = _mods(cs, w_ada[0], b_ada)

    x0 = x[0]
    x1, w13b, w2b, wall, mods_b = _ffn(x0, (mods_a, None), 0, 0, norm_ffn1, w13a, w2a, cast=(w13_ffn2[0], w2_ffn2[0]),
                                       side=(w_in[0], perm, w_ada[0], cs, b_ada))
    mods = (mods_a, mods_b)
    wkT = _kT(wall, eye)
    ctx1 = _ffn(ctx[0], mods, 1, 0, norm_ffn1, w13a, w2a)

    s_f, s_b = _ctx_state(lg, ctx1, mods, norm_mix, wkT, wall)

    q, kT, v, sgw, G, sga, sgf, sb, pr, pf, wo = _proj(lg, x1, mods, norm_mix, wall, wkT, md, ret_gn_w, cq, sq, ck, sk,
                                                     s_b, (p_ret[0], p_four[0], w_out[0]))
    a = _fft1(m1, G.reshape(2, N1, N2, D))
    fo = _fft3(t3, a.reshape(N2, 2, N1, D), pf).reshape(L, D)

    x2 = _retmix(lg, q, kT, v, s_f, sb, sgw, x1, mods, fo, sga, sgf, pr, wo)
    out = _ffn(x2, mods, 0, 6, norm_ffn2, w13b, w2b, nf=norm_final[None, :])
    return out[None]
```

```python
import functools

import numpy as np
import jax
import jax.numpy as jnp
from jax import lax
from jax.experimental import pallas as pl
from jax.experimental.pallas import tpu as pltpu

D = 1024
L = 16384
GRID_W = 64
CTX = 256
H = 4
DK = 256
DV = 512
QK = H * DK
VW = H * DV
NG = 4
DG = 256
DFF = 2816
ROPE_BASE = 10000.0
EPS = 1e-6
NMOD = 9

F32 = jnp.float32
BF16 = jnp.bfloat16

C = 256
N1 = 128
N2 = 128

VMEM_LIMIT = 56 * 1024 * 1024


def _dot(a, b):
    return jnp.dot(a, b, preferred_element_type=F32)


def _cparams(n_axes, vmem_limit=VMEM_LIMIT):
    return pltpu.CompilerParams(
        dimension_semantics=("arbitrary",) * n_axes, vmem_limit_bytes=vmem_limit)


def _resident(shape):
    nd = len(shape)
    return pl.BlockSpec(shape, lambda *_: (0,) * nd, pipeline_mode=pl.Buffered(1))


def _mod_spec(j):
    return pl.BlockSpec((8, D), lambda *_: (0, j), pipeline_mode=pl.Buffered(1))


NMOD_A = 3


def _mod(tables, j):
    return (tables[0], _mod_spec(j)) if j < NMOD_A else (tables[1], _mod_spec(j - NMOD_A))


def _norm_mod(x, nw, sc, sh):
    y = x * lax.rsqrt(jnp.mean(x * x, axis=-1, keepdims=True) + EPS)
    return y * (nw * (1.0 + sc)) + sh


def _mods_block(w_ref, cs_ref, b_ref, o_ref):
    cs = cs_ref[...]
    s = cs * jax.nn.sigmoid(cs)
    w = w_ref[...]
    rows = [jnp.sum(w * s[:, r:r + 1], axis=0, keepdims=True) for r in range(2)]
    o_ref[...] = jnp.concatenate(rows + [jnp.zeros((6, w.shape[1]), F32)], axis=0) + b_ref[...]


def _mods_kernel(cs_ref, w_ref, b_ref, o_ref):
    _mods_block(w_ref, cs_ref, b_ref, o_ref)


def _mods(cs, w_ada, b_ada):
    tn = NMOD_A * D // 2
    return pl.pallas_call(
        _mods_kernel,
        grid=(2,),
        in_specs=[
            _resident((D, 8)),
            pl.BlockSpec((D, tn), lambda j: (0, j)),
            pl.BlockSpec((1, tn), lambda j: (0, j)),
        ],
        out_specs=pl.BlockSpec((8, tn), lambda j: (0, j)),
        out_shape=jax.ShapeDtypeStruct((8, NMOD_A * D), F32),
        compiler_params=_cparams(1),
        name="mods",
    )(cs, w_ada, b_ada)


def _cast_blocks(src_refs, dst_refs):
    for s, d in zip(src_refs, dst_refs):
        d[...] = s[...].astype(BF16)


def _cast_specs(arrays, steps):
    specs = [pl.BlockSpec((a.shape[0] // steps, a.shape[1]), lambda i: (i, 0)) for a in arrays]
    shapes = [jax.ShapeDtypeStruct(a.shape, BF16) for a in arrays]
    return specs, shapes


def _prep_block(win_ref, p_ref, wall_ref):
    w = win_ref[...].astype(BF16)
    rb = w.shape[0]
    heads = jnp.concatenate([w[:, hd * DK:(hd + 1) * DK] for hd in range(2 * H)], axis=0)
    perm = _dot(heads, p_ref[...]).astype(BF16)
    for hd in range(H):
        wall_ref[:, _OQ + hd * DK:_OQ + (hd + 1) * DK] = perm[hd * rb:(hd + 1) * rb]
        wall_ref[:, _OK + hd * DK:_OK + (hd + 1) * DK] = perm[(H + hd) * rb:(H + hd + 1) * rb]
    wall_ref[:, QK:_OK] = w[:, 2 * QK:]


MODS_TN = 256


def _kT_kernel(eye_ref, w_ref, o_ref):
    o_ref[...] = lax.dot_general(eye_ref[...], w_ref[...], (((1,), (1,)), ((), ())),
                                 preferred_element_type=F32).astype(BF16)


def _kT(wall, eye):
    return pl.pallas_call(
        _kT_kernel,
        grid=(H,),
        in_specs=[_resident((DK, DK)), pl.BlockSpec((D, DK), lambda h: (0, _OK // DK + h))],
        out_specs=pl.BlockSpec((DK, D), lambda h: (h, 0)),
        out_shape=jax.ShapeDtypeStruct((QK, D), BF16),
        compiler_params=_cparams(1),
        name="kT",
    )(eye, wall)


FC = 256


def _swiglu_rows(x, sh, sc, g, nw, w13_ref, w2_ref, t_ref):
    h = _norm_mod(x, nw, sc, sh).astype(BF16)
    for j in range(DFF // FC):
        a = _dot(h, w13_ref[:, j * FC:(j + 1) * FC])
        b = _dot(h, w13_ref[:, DFF + j * FC:DFF + (j + 1) * FC])
        t_ref[:, j * FC:(j + 1) * FC] = (a * jax.nn.sigmoid(a) * b).astype(BF16)
    return x + (0.5 * g) * _dot(t_ref[...], w2_ref[...])


W13_CH = 512
W2_CH = 256


def _load_cast(src_hbm, dst_ref, stage_ref, sem_ref, axis, chunk):
    n = src_hbm.shape[axis] // chunk

    def window(c):
        return (slice(None), pl.ds(c * chunk, chunk)) if axis == 1 else (pl.ds(c * chunk, chunk), slice(None))

    def copy(c):
        return pltpu.make_async_copy(src_hbm.at[window(c)], stage_ref.at[c % 2], sem_ref.at[c % 2])

    copy(0).start()
    for c in range(n):
        if c + 1 < n:
            copy(c + 1).start()
        copy(c).wait()
        dst_ref[window(c)] = stage_ref[c % 2].astype(BF16)


def _ffn1_kernel(ncast, x_ref, ctx_ref, sh_ref, sc_ref, g_ref, nw_ref, w13_hbm, w2_hbm, *rest):
    cast_in, rest = rest[:ncast], rest[ncast:]
    (win_ref, p_ref, wada_ref, cs_ref, b_ref), rest = rest[:5], rest[5:]
    (o_ref, octx_ref), rest = rest[:2], rest[2:]
    cast_out, rest = rest[:ncast], rest[ncast:]
    wall_ref, mb_ref, t_ref, w13_ref, w2_ref, st13_ref, st2_ref, sem_ref = rest

    @pl.when(pl.program_id(0) == 0)
    def _():
        _load_cast(w13_hbm, w13_ref, st13_ref, sem_ref, 1, W13_CH)
        _load_cast(w2_hbm, w2_ref, st2_ref, sem_ref, 0, W2_CH)
        octx_ref[...] = _swiglu_rows(ctx_ref[...], sh_ref[1:2, :], sc_ref[1:2, :], g_ref[1:2, :], nw_ref[...],
                                     w13_ref, w2_ref, t_ref.at[pl.ds(0, CTX)])

    o_ref[...] = _swiglu_rows(x_ref[...], sh_ref[0:1, :], sc_ref[0:1, :], g_ref[0:1, :], nw_ref[...],
                              w13_ref, w2_ref, t_ref)
    _cast_blocks(cast_in, cast_out)
    _prep_block(win_ref, p_ref, wall_ref)
    _mods_block(wada_ref, cs_ref, b_ref, mb_ref)


def _ffn1(x, ctx, mods_a, nw, w13, w2, cast, w_in, perm, w_ada, cs, b_ada, tm=512):
    n = x.shape[0]
    steps = n // tm
    rb = D // steps
    cast_specs, cast_shapes = _cast_specs(cast, steps)
    nlate = (NMOD - NMOD_A) * D
    first, last = NMOD_A * D // MODS_TN, NMOD * D // MODS_TN - 1
    late_in = lambda i: (0, jnp.minimum(first + i, last))
    late_out = lambda i: (0, jnp.minimum(i, last - first))
    rows = lambda i: (i, 0)
    return pl.pallas_call(
        functools.partial(_ffn1_kernel, len(cast)),
        grid=(steps,),
        in_specs=[pl.BlockSpec((tm, D), rows), _resident((CTX, D)),
                  _mod_spec(0), _mod_spec(1), _mod_spec(2), _resident((1, D)),
                  pl.BlockSpec(memory_space=pl.ANY), pl.BlockSpec(memory_space=pl.ANY)]
                 + cast_specs
                 + [pl.BlockSpec((rb, _WIN), rows), _resident((DK, DK)),
                    pl.BlockSpec((D, MODS_TN), late_in), _resident((D, 8)), pl.BlockSpec((1, MODS_TN), late_in)],
        out_specs=[pl.BlockSpec((tm, D), rows), pl.BlockSpec((CTX, D), lambda i: (0, 0))]
                  + cast_specs
                  + [pl.BlockSpec((rb, _WIN), rows), pl.BlockSpec((8, MODS_TN), late_out)],
        out_shape=[jax.ShapeDtypeStruct((n, D), F32), jax.ShapeDtypeStruct((CTX, D), F32)]
                  + cast_shapes
                  + [jax.ShapeDtypeStruct((D, _WIN), BF16), jax.ShapeDtypeStruct((8, nlate), F32)],
        scratch_shapes=[pltpu.VMEM((tm, DFF), BF16), pltpu.VMEM((D, 2 * DFF), BF16), pltpu.VMEM((DFF, D), BF16),
                        pltpu.VMEM((2, D, W13_CH), F32), pltpu.VMEM((2, W2_CH, D), F32),
                        pltpu.SemaphoreType.DMA((2,))],
        compiler_params=_cparams(1),
        name="ffn1",
    )(x, ctx, mods_a, mods_a, mods_a, nw, w13, w2, *cast, w_in, perm, w_ada, cs, b_ada)


def _ffn2_kernel(x_ref, sh_ref, sc_ref, g_ref, nw_ref, w13_ref, w2_ref, nf_ref, o_ref, t_ref):
    out = _swiglu_rows(x_ref[...], sh_ref[0:1, :], sc_ref[0:1, :], g_ref[0:1, :], nw_ref[...], w13_ref, w2_ref, t_ref)
    o_ref[...] = out * lax.rsqrt(jnp.mean(out * out, axis=-1, keepdims=True) + EPS) * nf_ref[...]


def _ffn2(x, tables, nw, w13, w2, nf, tm=1024):
    n = x.shape[0]
    mod_arrays, mod_specs = zip(*[_mod(tables, 6 + k) for k in range(3)])
    return pl.pallas_call(
        _ffn2_kernel,
        grid=(n // tm,),
        in_specs=[pl.BlockSpec((tm, D), lambda i: (i, 0)), *mod_specs,
                  _resident((1, D)), _resident((D, 2 * DFF)), _resident((DFF, D)), _resident((1, D))],
        out_specs=pl.BlockSpec((tm, D), lambda i: (i, 0)),
        out_shape=jax.ShapeDtypeStruct((n, D), F32),
        scratch_shapes=[pltpu.VMEM((tm, DFF), BF16)],
        compiler_params=_cparams(1),
        name="ffn2",
    )(x, *mod_arrays, nw, w13, w2, nf)


def _ctx_state_kernel(lg_ref, x_ref, sh_ref, sc_ref, nw_ref, wkT_ref, wv_ref, sf_ref, sb_ref):
    hd = pl.program_id(0)
    h = _norm_mod(x_ref[...], nw_ref[...], sc_ref[1:2, :], sh_ref[1:2, :]).astype(BF16)
    kT = lax.dot_general(wkT_ref[...], h, (((1,), (1,)), ((), ())), preferred_element_type=F32)
    v = _dot(h, wv_ref[...]).astype(BF16)
    j = lax.broadcasted_iota(jnp.int32, (DK, CTX), 1).astype(F32)
    wf = jnp.exp(lg_ref[0, hd] * (CTX - 1.0 - j))
    wb = jnp.exp(lg_ref[1, hd] * j)
    sf_ref[0] = _dot((kT * wf).astype(BF16), v)
    sb_ref[0] = _dot((kT * wb).astype(BF16), v)


def _ctx_state(lg, ctx1, tables, nw, wkT, wall):
    return pl.pallas_call(
        _ctx_state_kernel,
        grid=(H,),
        in_specs=[
            pl.BlockSpec(memory_space=pltpu.SMEM),
            _resident((CTX, D)),
            _mod(tables, 3)[1], _mod(tables, 4)[1],
            _resident((1, D)),
            pl.BlockSpec((DK, D), lambda h: (h, 0)),
            pl.BlockSpec((D, DV), lambda h: (0, _OV // DV + h)),
        ],
        out_specs=[pl.BlockSpec((1, DK, DV), lambda h: (h, 0, 0))] * 2,
        out_shape=[jax.ShapeDtypeStruct((H, DK, DV), F32)] * 2,
        compiler_params=_cparams(1),
        name="ctx_state",
    )(lg, ctx1, tables[1], tables[1], nw, wkT, wall)


_OQ, _OV, _OG = 0, QK, QK + VW
_OF, _OGA, _OGF = QK + 2 * VW, QK + 2 * VW + D, QK + 2 * VW + 2 * D
_WIN = 2 * QK + 2 * VW + 3 * D
_OK = _WIN - QK
HALF = DK // 2

PT = 2
PROJ_VMEM_LIMIT = 58 * 1024 * 1024


def _proj_chunk(rows, ci, lg_ref, x_ref, sh_ref, sc_ref, nw_ref, w_ref, wkT_ref, md_ref, gnw_ref, cq_ref, sq_ref,
                ck_ref, sk_ref, q_ref, kT_ref, v_ref, sg_ref, G_ref, sga_ref, sgf_ref, sb_ref, S_ref, ZE_ref):
    h = _norm_mod(x_ref[rows, :], nw_ref[...], sc_ref[0:1, :], sh_ref[0:1, :]).astype(BF16)

    q = _dot(h, w_ref[:, _OQ:_OQ + QK])
    cq, sq = cq_ref[rows, :], sq_ref[rows, :]
    for hd in range(H):
        a1 = q[:, hd * DK:hd * DK + HALF]
        a2 = q[:, hd * DK + HALF:(hd + 1) * DK]
        q_ref[rows, hd * DK:hd * DK + HALF] = (a1 * cq - a2 * sq).astype(BF16)
        q_ref[rows, hd * DK + HALF:(hd + 1) * DK] = (a1 * sq + a2 * cq).astype(BF16)

    kT = lax.dot_general(wkT_ref[...], h, (((1,), (1,)), ((), ())), preferred_element_type=F32)
    ck, sk = ck_ref[:, rows], sk_ref[:, rows]
    for hd in range(H):
        a1 = kT[hd * DK:hd * DK + HALF, :]
        a2 = kT[hd * DK + HALF:(hd + 1) * DK, :]
        kT_ref[hd * DK:hd * DK + HALF, rows] = (a1 * ck - a2 * sk).astype(BF16)
        kT_ref[hd * DK + HALF:(hd + 1) * DK, rows] = (a1 * sk + a2 * ck).astype(BF16)

    v = _dot(h, w_ref[:, _OV:_OV + VW]).astype(BF16)
    v_ref[rows, :] = v
    for hd in range(H):
        S = S_ref[hd]
        sb_ref[hd, ci] = S.astype(BF16)
        kz = (kT_ref[hd * DK:(hd + 1) * DK, rows].astype(F32) * ZE_ref[hd]).astype(BF16)
        g_chunk = jnp.exp(jnp.full((1, DV), lg_ref[1, hd] * C, F32))
        S_ref[hd] = S * g_chunk + _dot(kz, v[:, hd * DV:(hd + 1) * DV])

    g = _dot(h, w_ref[:, _OG:_OG + VW])
    sg_ref[rows, :] = (g * jax.nn.sigmoid(g) * gnw_ref[...]).astype(BF16)

    f = _dot(h, w_ref[:, _OF:_OF + D]).astype(BF16)
    for gi in range(NG):
        z = _dot(f[:, gi * DG:(gi + 1) * DG], md_ref[...])
        G_ref[0, rows, gi * DG:(gi + 1) * DG] = z[:, :DG].astype(BF16)
        G_ref[1, rows, gi * DG:(gi + 1) * DG] = z[:, DG:].astype(BF16)

    sga_ref[rows, :] = jax.nn.sigmoid(_dot(h, w_ref[:, _OGA:_OGA + D])).astype(BF16)
    sgf_ref[rows, :] = jax.nn.sigmoid(_dot(h, w_ref[:, _OGF:_OGF + D])).astype(BF16)


def _proj_kernel(lg_ref, x_ref, sh_ref, sc_ref, nw_ref, w_ref, wkT_ref, md_ref, gnw_ref, cq_ref, sq_ref, ck_ref,
                 sk_ref, s0_ref, pr_ref, pf_ref, wo_ref, q_ref, kT_ref, v_ref, sg_ref, G_ref, sga_ref, sgf_ref, sb_ref,
                 prb_ref, pfb_ref, wob_ref, S_ref, ZE_ref):
    _cast_blocks((pr_ref, pf_ref, wo_ref), (prb_ref, pfb_ref, wob_ref))

    @pl.when(pl.program_id(0) == 0)
    def _():
        S_ref[...] = s0_ref[...]
        c = lax.broadcasted_iota(jnp.int32, (DK, C), 1).astype(F32)
        for hd in range(H):
            ZE_ref[hd] = jnp.exp(lg_ref[1, hd] * c)

    for ci in range(PT - 1, -1, -1):
        _proj_chunk(slice(ci * C, (ci + 1) * C), ci, lg_ref, x_ref, sh_ref, sc_ref, nw_ref, w_ref, wkT_ref, md_ref,
                    gnw_ref, cq_ref, sq_ref, ck_ref, sk_ref, q_ref, kT_ref, v_ref, sg_ref, G_ref, sga_ref, sgf_ref,
                    sb_ref, S_ref, ZE_ref)


def _proj(lg, x1, tables, nw, wall, wkT, md, gnw, cq, sq, ck, sk, s_b0, cast):
    tm = PT * C
    n = L // tm
    cast_specs, cast_shapes = _cast_specs(cast, n)
    row = lambda i: (n - 1 - i, 0)
    col = lambda i: (0, n - 1 - i)
    return pl.pallas_call(
        _proj_kernel,
        grid=(n,),
        in_specs=[
            pl.BlockSpec(memory_space=pltpu.SMEM),
            pl.BlockSpec((tm, D), row),
            _mod(tables, 3)[1], _mod(tables, 4)[1],
            _resident((1, D)),
            _resident((D, _OK)),
            _resident((QK, D)),
            _resident((DG, 2 * DG)),
            _resident((1, VW)),
            pl.BlockSpec((tm, HALF), row), pl.BlockSpec((tm, HALF), row),
            pl.BlockSpec((HALF, tm), col), pl.BlockSpec((HALF, tm), col),
            _resident((H, DK, DV)),
        ] + cast_specs,
        out_specs=[
            pl.BlockSpec((tm, QK), row),
            pl.BlockSpec((QK, tm), col),
            pl.BlockSpec((tm, VW), row),
            pl.BlockSpec((tm, VW), row),
            pl.BlockSpec((2, tm, D), lambda i: (0, n - 1 - i, 0)),
            pl.BlockSpec((tm, D), row),
            pl.BlockSpec((tm, D), row),
            pl.BlockSpec((H, PT, DK, DV), lambda i: (0, n - 1 - i, 0, 0)),
        ] + cast_specs,
        out_shape=[
            jax.ShapeDtypeStruct((L, QK), BF16),
            jax.ShapeDtypeStruct((QK, L), BF16),
            jax.ShapeDtypeStruct((L, VW), BF16),
            jax.ShapeDtypeStruct((L, VW), BF16),
            jax.ShapeDtypeStruct((2, L, D), BF16),
            jax.ShapeDtypeStruct((L, D), BF16),
            jax.ShapeDtypeStruct((L, D), BF16),
            jax.ShapeDtypeStruct((H, L // C, DK, DV), BF16),
        ] + cast_shapes,
        scratch_shapes=[pltpu.VMEM((H, DK, DV), F32), pltpu.VMEM((H, DK, C), F32)],
        compiler_params=_cparams(1, PROJ_VMEM_LIMIT),
        name="proj",
    )(lg, x1, tables[1], tables[1], nw, wall, wkT, md, gnw, cq, sq, ck, sk, s_b0, *cast)


FFT_NT = 16
FFT_KB = 16


def _fft1_kernel(m_ref, g_ref, o_ref):
    z = jnp.transpose(g_ref[...], (2, 0, 1, 3)).reshape(FFT_NT, 2 * N1, D)
    for t in range(FFT_NT):
        o_ref[t] = _dot(m_ref[...], z[t]).astype(BF16)


def _fft1(m1, g4):
    return pl.pallas_call(
        _fft1_kernel,
        grid=(N2 // FFT_NT,),
        in_specs=[_resident((2 * N1, 2 * N1)),
                  pl.BlockSpec((2, N1, FFT_NT, D), lambda j: (0, 0, j, 0))],
        out_specs=pl.BlockSpec((FFT_NT, 2 * N1, D), lambda j: (j, 0, 0)),
        out_shape=jax.ShapeDtypeStruct((N2, 2 * N1, D), BF16),
        compiler_params=_cparams(1),
        name="fft1",
    )(m1, g4)


def _fft3_kernel(t_ref, pf_ref, a_ref, o_ref, xr_ref):
    z = jnp.transpose(a_ref[...], (2, 1, 0, 3)).reshape(FFT_KB, 2 * N2, D)
    for i in range(FFT_KB):
        xr_ref[i * N2:(i + 1) * N2, :] = _dot(t_ref[i], z[i]).astype(BF16)
    fo = _dot(xr_ref[...], pf_ref[...]).astype(BF16).reshape(FFT_KB, N2, D)
    o_ref[...] = jnp.transpose(fo, (1, 0, 2))


def _fft3(t3, a4, pf):
    return pl.pallas_call(
        _fft3_kernel,
        grid=(N1 // FFT_KB,),
        in_specs=[pl.BlockSpec((FFT_KB, N2, 2 * N2), lambda j: (j, 0, 0)),
                  _resident((D, D)),
                  pl.BlockSpec((N2, 2, FFT_KB, D), lambda j: (0, 0, j, 0))],
        out_specs=pl.BlockSpec((N2, FFT_KB, D), lambda j: (0, j, 0)),
        out_shape=jax.ShapeDtypeStruct((N2, N1, D), BF16),
        scratch_shapes=[pltpu.VMEM((FFT_KB * N2, D), BF16)],
        compiler_params=_cparams(1),
        name="fft3",
    )(t3, pf, a4)


TB_R = 512


def _retmix_kernel(lg_ref, q_ref, kT_ref, v_ref, s0_ref, sb_ref, sgw_ref, x_ref, g_ref, fo_ref, sga_ref, sgf_ref,
                   pr_ref, wo_ref, o_ref, S_ref, D_ref, XF_ref, XB_ref, ZE_ref, rin_ref):
    @pl.when(pl.program_id(0) == 0)
    def _():
        S_ref[...] = s0_ref[...]
        r = lax.broadcasted_iota(jnp.int32, (C, C), 0).astype(F32)
        c = lax.broadcasted_iota(jnp.int32, (C, C), 1).astype(F32)
        d = r - c
        for hd in range(H):
            lgf, lgb = lg_ref[0, hd], lg_ref[1, hd]
            D_ref[hd] = (jnp.where(d >= 0, jnp.exp(lgf * jnp.maximum(d, 0.0)), 0.0)
                         + jnp.where(d <= 0, jnp.exp(lgb * jnp.maximum(-d, 0.0)), 0.0))
            XF_ref[hd] = jnp.exp(lgf * (r + 1.0))
            XB_ref[hd] = jnp.exp(lgb * (C - r))
            ZE_ref[hd] = jnp.exp(lgf * (C - 1.0 - c))

    for hd in range(H):
        g_chunk = jnp.exp(jnp.full((1, DV), lg_ref[0, hd] * C, F32))
        for ci in range(TB_R // C):
            rows = slice(ci * C, (ci + 1) * C)
            qc = q_ref[rows, hd * DK:(hd + 1) * DK]
            kTc = kT_ref[hd * DK:(hd + 1) * DK, rows]
            vc = v_ref[rows, hd * DV:(hd + 1) * DV]
            S = S_ref[hd]
            p = (_dot(qc, kTc) * D_ref[hd]).astype(BF16)
            qf = (qc.astype(F32) * XF_ref[hd]).astype(BF16)
            qb = (qc.astype(F32) * XB_ref[hd]).astype(BF16)
            o = _dot(p, vc) + _dot(qf, S.astype(BF16)) + _dot(qb, sb_ref[hd, ci])
            kz = (kTc.astype(F32) * ZE_ref[hd]).astype(BF16)
            S_ref[hd] = S * g_chunk + _dot(kz, vc)
            mu = jnp.mean(o, axis=-1, keepdims=True)
            d = o - mu
            var = jnp.mean(d * d, axis=-1, keepdims=True)
            on = d * lax.rsqrt(var + EPS)
            rin_ref[rows, hd * DV:(hd + 1) * DV] = (sgw_ref[rows, hd * DV:(hd + 1) * DV].astype(F32) * on).astype(BF16)

    r = _dot(rin_ref[...], pr_ref[...])
    m = sga_ref[...].astype(F32) * r + sgf_ref[...].astype(F32) * fo_ref[...].astype(F32)
    y = _dot(m.astype(BF16), wo_ref[...])
    o_ref[...] = x_ref[...] + g_ref[0:1, :] * y


def _retmix(lg, q, kT, v, s0, sb, sgw, x1, tables, fo, sga, sgf, p_ret, w_out):
    tb = TB_R
    row = lambda i: (i, 0)
    return pl.pallas_call(
        _retmix_kernel,
        grid=(L // tb,),
        in_specs=[
            pl.BlockSpec(memory_space=pltpu.SMEM),
            pl.BlockSpec((tb, QK), row),
            pl.BlockSpec((QK, tb), lambda i: (0, i)),
            pl.BlockSpec((tb, VW), row),
            _resident((H, DK, DV)),
            pl.BlockSpec((H, tb // C, DK, DV), lambda i: (0, i, 0, 0)),
            pl.BlockSpec((tb, VW), row),
            pl.BlockSpec((tb, D), row),
            _mod(tables, 5)[1],
            pl.BlockSpec((tb, D), row), pl.BlockSpec((tb, D), row), pl.BlockSpec((tb, D), row),
            _resident((VW, D)), _resident((D, D)),
        ],
        out_specs=pl.BlockSpec((tb, D), row),
        out_shape=jax.ShapeDtypeStruct((L, D), F32),
        scratch_shapes=[pltpu.VMEM((H, DK, DV), F32)] + [pltpu.VMEM((H, C, C), F32)] * 4
                       + [pltpu.VMEM((tb, VW), BF16)],
        compiler_params=_cparams(1),
        name="retmix",
    )(lg, q, kT, v, s0, sb, sgw, x1, tables[1], fo, sga, sgf, p_ret, w_out)


def _rope_tables():
    rows = L // GRID_W
    row = np.repeat(np.arange(rows, dtype=np.float64), GRID_W)
    col = np.tile(np.arange(GRID_W, dtype=np.float64), rows)
    n_freq = DK // 4
    inv = ROPE_BASE ** (-np.arange(n_freq, dtype=np.float64) / n_freq)
    ang = np.concatenate([row[:, None] * inv, col[:, None] * inv], axis=-1)
    return np.cos(ang), np.sin(ang)


def _dft_tables():
    two_pi = 2.0 * np.pi
    th = (np.outer(np.arange(DG), np.arange(DG)) % DG) * (two_pi / DG)
    md = np.concatenate([np.cos(th), -np.sin(th)], axis=1)
    th = (np.outer(np.arange(N1), np.arange(N1)) % N1) * (two_pi / N1)
    c1, s1 = np.cos(th), np.sin(th)
    m1 = np.block([[c1, s1], [-s1, c1]])
    k = np.arange(N1)[:, None, None] + N1 * np.arange(N2)[None, :, None]
    th = ((k * np.arange(N2)[None, None, :]) % L) * (two_pi / L)
    t3 = np.concatenate([np.cos(th), np.sin(th)], axis=-1) / np.sqrt(float(L) * DG)
    f32 = lambda a: jnp.asarray(a.astype(np.float32))
    return f32(md).astype(BF16), f32(m1).astype(BF16), f32(t3).astype(BF16)


def kernel(x, c, ctx, c_ctx, w_ada, b_ada, norm_ffn1, w13_ffn1, w2_ffn1, norm_mix, w_in, decay_fwd, decay_bwd,
           ret_gn_w, p_ret, p_four, w_out, norm_ffn2, w13_ffn2, w2_ffn2, norm_final):
    assert x.shape == (1, L, D) and ctx.shape == (1, CTX, D) and w_ada.shape[0] == 1

    pm = np.zeros((DK, DK), np.float32)
    pm[np.concatenate([np.arange(0, DK, 2), np.arange(1, DK, 2)]), np.arange(DK)] = 1.0
    perm = jnp.asarray(pm).astype(BF16)
    eye = jnp.asarray(np.eye(DK, dtype=np.float32)).astype(BF16)

    lg = jnp.stack([jax.nn.log_sigmoid(decay_fwd[0].astype(F32)),
                    jax.nn.log_sigmoid(decay_bwd[0].astype(F32))])
    cos, sin = _rope_tables()
    qscale = DK ** -0.5
    f32c = lambda a: jnp.asarray(np.ascontiguousarray(a).astype(np.float32))
    cq, sq, ck, sk = f32c(cos * qscale), f32c(sin * qscale), f32c(cos.T), f32c(sin.T)
    md, m1, t3 = _dft_tables()

    cs = jnp.concatenate([c, c_ctx[None, :], jnp.zeros((6, D), F32)], axis=0).T
    mods_a = _mods(cs, w_ada[0], b_ada)

    x1, ctx1, w13b, w2b, wall, mods_b = _ffn1(x[0], ctx[0], mods_a, norm_ffn1, w13_ffn1[0], w2_ffn1[0],
                                              (w13_ffn2[0], w2_ffn2[0]), w_in[0], perm, w_ada[0], cs, b_ada)
    mods = (mods_a, mods_b)
    wkT = _kT(wall, eye)

    s_f, s_b = _ctx_state(lg, ctx1, mods, norm_mix, wkT, wall)

    q, kT, v, sgw, G, sga, sgf, sb, pr, pf, wo = _proj(lg, x1, mods, norm_mix, wall, wkT, md, ret_gn_w, cq, sq, ck, sk,
                                                     s_b, (p_ret[0], p_four[0], w_out[0]))
    a = _fft1(m1, G.reshape(2, N1, N2, D))
    fo = _fft3(t3, a.reshape(N2, 2, N1, D), pf).reshape(L, D)

    x2 = _retmix(lg, q, kT, v, s_f, sb, sgw, x1, mods, fo, sga, sgf, pr, wo)
    out = _ffn2(x2, mods, norm_ffn2, w13b, w2b, norm_final[None, :])
    return out[None]
```

```python
import functools

import numpy as np
import jax
import jax.numpy as jnp
from jax import lax
from jax.experimental import pallas as pl
from jax.experimental.pallas import tpu as pltpu

D = 1024
L = 16384
GRID_W = 64
CTX = 256
H = 4
DK = 256
DV = 512
QK = H * DK
VW = H * DV
NG = 4
DG = 256
DFF = 2816
ROPE_BASE = 10000.0
EPS = 1e-6
NMOD = 9

F32 = jnp.float32
BF16 = jnp.bfloat16

C = 256
N1 = 128
N2 = 128

VMEM_LIMIT = 56 * 1024 * 1024


def _dot(a, b):
    return jnp.dot(a, b, preferred_element_type=F32)


def _cparams(n_axes, vmem_limit=VMEM_LIMIT):
    return pltpu.CompilerParams(
        dimension_semantics=("arbitrary",) * n_axes, vmem_limit_bytes=vmem_limit)


def _resident(shape):
    nd = len(shape)
    return pl.BlockSpec(shape, lambda *_: (0,) * nd, pipeline_mode=pl.Buffered(1))


def _mod_spec(j):
    return pl.BlockSpec((8, D), lambda *_: (0, j), pipeline_mode=pl.Buffered(1))


NMOD_A = 3


def _mod(tables, j):
    return (tables[0], _mod_spec(j)) if j < NMOD_A else (tables[1], _mod_spec(j - NMOD_A))


def _norm_mod(x, nw, sc, sh):
    y = x * lax.rsqrt(jnp.mean(x * x, axis=-1, keepdims=True) + EPS)
    return y * (nw * (1.0 + sc)) + sh


def _mods_block(w_ref, cs_ref, b_ref, o_ref):
    cs = cs_ref[...]
    s = cs * jax.nn.sigmoid(cs)
    w = w_ref[...]
    rows = [jnp.sum(w * s[:, r:r + 1], axis=0, keepdims=True) for r in range(2)]
    o_ref[...] = jnp.concatenate(rows + [jnp.zeros((6, w.shape[1]), F32)], axis=0) + b_ref[...]


def _mods_kernel(cs_ref, w_ref, b_ref, o_ref):
    _mods_block(w_ref, cs_ref, b_ref, o_ref)


def _mods(cs, w_ada, b_ada):
    tn = NMOD_A * D // 2
    return pl.pallas_call(
        _mods_kernel,
        grid=(2,),
        in_specs=[
            _resident((D, 8)),
            pl.BlockSpec((D, tn), lambda j: (0, j)),
            pl.BlockSpec((1, tn), lambda j: (0, j)),
        ],
        out_specs=pl.BlockSpec((8, tn), lambda j: (0, j)),
        out_shape=jax.ShapeDtypeStruct((8, NMOD_A * D), F32),
        compiler_params=_cparams(1),
        name="mods",
    )(cs, w_ada, b_ada)


def _cast_blocks(src_refs, dst_refs):
    for s, d in zip(src_refs, dst_refs):
        d[...] = s[...].astype(BF16)


def _cast_specs(arrays, steps):
    specs = [pl.BlockSpec((a.shape[0] // steps, a.shape[1]), lambda i: (i, 0)) for a in arrays]
    shapes = [jax.ShapeDtypeStruct(a.shape, BF16) for a in arrays]
    return specs, shapes


def _prep_block(win_ref, p_ref, wall_ref):
    w = win_ref[...].astype(BF16)
    rb = w.shape[0]
    heads = jnp.concatenate([w[:, hd * DK:(hd + 1) * DK] for hd in range(2 * H)], axis=0)
    perm = _dot(heads, p_ref[...]).astype(BF16)
    for hd in range(H):
        wall_ref[:, _OQ + hd * DK:_OQ + (hd + 1) * DK] = perm[hd * rb:(hd + 1) * rb]
        wall_ref[:, _OK + hd * DK:_OK + (hd + 1) * DK] = perm[(H + hd) * rb:(H + hd + 1) * rb]
    wall_ref[:, QK:_OK] = w[:, 2 * QK:]


MODS_TN = 256


def _kT_kernel(eye_ref, w_ref, o_ref):
    o_ref[...] = lax.dot_general(eye_ref[...], w_ref[...], (((1,), (1,)), ((), ())),
                                 preferred_element_type=F32).astype(BF16)


def _kT(wall, eye):
    return pl.pallas_call(
        _kT_kernel,
        grid=(H,),
        in_specs=[_resident((DK, DK)), pl.BlockSpec((D, DK), lambda h: (0, _OK // DK + h))],
        out_specs=pl.BlockSpec((DK, D), lambda h: (h, 0)),
        out_shape=jax.ShapeDtypeStruct((QK, D), BF16),
        compiler_params=_cparams(1),
        name="kT",
    )(eye, wall)


FC = 256


def _swiglu_rows(x, sh, sc, g, nw, w13_ref, w2_ref, t_ref):
    h = _norm_mod(x, nw, sc, sh).astype(BF16)
    for j in range(DFF // FC):
        a = _dot(h, w13_ref[:, j * FC:(j + 1) * FC])
        b = _dot(h, w13_ref[:, DFF + j * FC:DFF + (j + 1) * FC])
        t_ref[:, j * FC:(j + 1) * FC] = (a * jax.nn.sigmoid(a) * b).astype(BF16)
    return x + (0.5 * g) * _dot(t_ref[...], w2_ref[...])


W13_CH = 512
W2_CH = 256


def _load_cast(src_hbm, dst_ref, stage_ref, sem_ref, axis, chunk):
    n = src_hbm.shape[axis] // chunk

    def window(c):
        return (slice(None), pl.ds(c * chunk, chunk)) if axis == 1 else (pl.ds(c * chunk, chunk), slice(None))

    def copy(c):
        return pltpu.make_async_copy(src_hbm.at[window(c)], stage_ref.at[c % 2], sem_ref.at[c % 2])

    copy(0).start()
    for c in range(n):
        if c + 1 < n:
            copy(c + 1).start()
        copy(c).wait()
        dst_ref[window(c)] = stage_ref[c % 2].astype(BF16)


def _ffn1_kernel(ncast, x_ref, ctx_ref, sh_ref, sc_ref, g_ref, nw_ref, w13_hbm, w2_hbm, *rest):
    cast_in, rest = rest[:ncast], rest[ncast:]
    (win_ref, p_ref, wada_ref, cs_ref, b_ref), rest = rest[:5], rest[5:]
    (o_ref, octx_ref), rest = rest[:2], rest[2:]
    cast_out, rest = rest[:ncast], rest[ncast:]
    wall_ref, mb_ref, t_ref, w13_ref, w2_ref, st13_ref, st2_ref, sem_ref = rest

    @pl.when(pl.program_id(0) == 0)
    def _():
        _load_cast(w13_hbm, w13_ref, st13_ref, sem_ref, 1, W13_CH)
        _load_cast(w2_hbm, w2_ref, st2_ref, sem_ref, 0, W2_CH)
        octx_ref[...] = _swiglu_rows(ctx_ref[...], sh_ref[1:2, :], sc_ref[1:2, :], g_ref[1:2, :], nw_ref[...],
                                     w13_ref, w2_ref, t_ref.at[pl.ds(0, CTX)])

    o_ref[...] = _swiglu_rows(x_ref[...], sh_ref[0:1, :], sc_ref[0:1, :], g_ref[0:1, :], nw_ref[...],
                              w13_ref, w2_ref, t_ref)
    _cast_blocks(cast_in, cast_out)
    _prep_block(win_ref, p_ref, wall_ref)
    _mods_block(wada_ref, cs_ref, b_ref, mb_ref)


def _ffn1(x, ctx, mods_a, nw, w13, w2, cast, w_in, perm, w_ada, cs, b_ada, tm=512):
    n = x.shape[0]
    steps = n // tm
    rb = D // steps
    cast_specs, cast_shapes = _cast_specs(cast, steps)
    nlate = (NMOD - NMOD_A) * D
    first, last = NMOD_A * D // MODS_TN, NMOD * D // MODS_TN - 1
    late_in = lambda i: (0, jnp.minimum(first + i, last))
    late_out = lambda i: (0, jnp.minimum(i, last - first))
    rows = lambda i: (i, 0)
    return pl.pallas_call(
        functools.partial(_ffn1_kernel, len(cast)),
        grid=(steps,),
        in_specs=[pl.BlockSpec((tm, D), rows), _resident((CTX, D)),
                  _mod_spec(0), _mod_spec(1), _mod_spec(2), _resident((1, D)),
                  pl.BlockSpec(memory_space=pl.ANY), pl.BlockSpec(memory_space=pl.ANY)]
                 + cast_specs
                 + [pl.BlockSpec((rb, _WIN), rows), _resident((DK, DK)),
                    pl.BlockSpec((D, MODS_TN), late_in), _resident((D, 8)), pl.BlockSpec((1, MODS_TN), late_in)],
        out_specs=[pl.BlockSpec((tm, D), rows), pl.BlockSpec((CTX, D), lambda i: (0, 0))]
                  + cast_specs
                  + [pl.BlockSpec((rb, _WIN), rows), pl.BlockSpec((8, MODS_TN), late_out)],
        out_shape=[jax.ShapeDtypeStruct((n, D), F32), jax.ShapeDtypeStruct((CTX, D), F32)]
                  + cast_shapes
                  + [jax.ShapeDtypeStruct((D, _WIN), BF16), jax.ShapeDtypeStruct((8, nlate), F32)],
        scratch_shapes=[pltpu.VMEM((tm, DFF), BF16), pltpu.VMEM((D, 2 * DFF), BF16), pltpu.VMEM((DFF, D), BF16),
                        pltpu.VMEM((2, D, W13_CH), F32), pltpu.VMEM((2, W2_CH, D), F32),
                        pltpu.SemaphoreType.DMA((2,))],
        compiler_params=_cparams(1),
        name="ffn1",
    )(x, ctx, mods_a, mods_a, mods_a, nw, w13, w2, *cast, w_in, perm, w_ada, cs, b_ada)


def _ffn2_kernel(x_ref, sh_ref, sc_ref, g_ref, nw_ref, w13_ref, w2_ref, nf_ref, o_ref, t_ref):
    out = _swiglu_rows(x_ref[...], sh_ref[0:1, :], sc_ref[0:1, :], g_ref[0:1, :], nw_ref[...], w13_ref, w2_ref, t_ref)
    o_ref[...] = out * lax.rsqrt(jnp.mean(out * out, axis=-1, keepdims=True) + EPS) * nf_ref[...]


def _ffn2(x, tables, nw, w13, w2, nf, tm=1024):
    n = x.shape[0]
    mod_arrays, mod_specs = zip(*[_mod(tables, 6 + k) for k in range(3)])
    return pl.pallas_call(
        _ffn2_kernel,
        grid=(n // tm,),
        in_specs=[pl.BlockSpec((tm, D), lambda i: (i, 0)), *mod_specs,
                  _resident((1, D)), _resident((D, 2 * DFF)), _resident((DFF, D)), _resident((1, D))],
        out_specs=pl.BlockSpec((tm, D), lambda i: (i, 0)),
        out_shape=jax.ShapeDtypeStruct((n, D), F32),
        scratch_shapes=[pltpu.VMEM((tm, DFF), BF16)],
        compiler_params=_cparams(1),
        name="ffn2",
    )(x, *mod_arrays, nw, w13, w2, nf)


def _ctx_state_kernel(lg_ref, x_ref, sh_ref, sc_ref, nw_ref, wkT_ref, wv_ref, sf_ref, sb_ref):
    hd = pl.program_id(0)
    h = _norm_mod(x_ref[...], nw_ref[...], sc_ref[1:2, :], sh_ref[1:2, :]).astype(BF16)
    kT = lax.dot_general(wkT_ref[...], h, (((1,), (1,)), ((), ())), preferred_element_type=F32)
    v = _dot(h, wv_ref[...]).astype(BF16)
    j = lax.broadcasted_iota(jnp.int32, (DK, CTX), 1).astype(F32)
    wf = jnp.exp(lg_ref[0, hd] * (CTX - 1.0 - j))
    wb = jnp.exp(lg_ref[1, hd] * j)
    sf_ref[0] = _dot((kT * wf).astype(BF16), v)
    sb_ref[0] = _dot((kT * wb).astype(BF16), v)


def _ctx_state(lg, ctx1, tables, nw, wkT, wall):
    return pl.pallas_call(
        _ctx_state_kernel,
        grid=(H,),
        in_specs=[
            pl.BlockSpec(memory_space=pltpu.SMEM),
            _resident((CTX, D)),
            _mod(tables, 3)[1], _mod(tables, 4)[1],
            _resident((1, D)),
            pl.BlockSpec((DK, D), lambda h: (h, 0)),
            pl.BlockSpec((D, DV), lambda h: (0, _OV // DV + h)),
        ],
        out_specs=[pl.BlockSpec((1, DK, DV), lambda h: (h, 0, 0))] * 2,
        out_shape=[jax.ShapeDtypeStruct((H, DK, DV), F32)] * 2,
        compiler_params=_cparams(1),
        name="ctx_state",
    )(lg, ctx1, tables[1], tables[1], nw, wkT, wall)


_OQ, _OV, _OG = 0, QK, QK + VW
_OF, _OGA, _OGF = QK + 2 * VW, QK + 2 * VW + D, QK + 2 * VW + 2 * D
_WIN = 2 * QK + 2 * VW + 3 * D
_OK = _WIN - QK
HALF = DK // 2

_TQ, _TV, _TS, _TGA, _TGF = 0, QK, QK + VW, QK + 2 * VW, QK + 2 * VW + D
_TW = QK + 2 * VW + 2 * D


def _tok_views(tok_ref):
    cut = lambda a, w: tok_ref.at[:, a:a + w]
    return cut(_TQ, QK), cut(_TV, VW), cut(_TS, VW), cut(_TGA, D), cut(_TGF, D)

PT = 2
PROJ_VMEM_LIMIT = 58 * 1024 * 1024


def _proj_chunk(rows, ci, lg_ref, x_ref, sh_ref, sc_ref, nw_ref, w_ref, wkT_ref, md_ref, gnw_ref, cq_ref, sq_ref,
                ck_ref, sk_ref, q_ref, kT_ref, v_ref, sg_ref, G_ref, sga_ref, sgf_ref, sb_ref, S_ref, ZE_ref):
    h = _norm_mod(x_ref[rows, :], nw_ref[...], sc_ref[0:1, :], sh_ref[0:1, :]).astype(BF16)

    q = _dot(h, w_ref[:, _OQ:_OQ + QK])
    cq, sq = cq_ref[rows, :], sq_ref[rows, :]
    for hd in range(H):
        a1 = q[:, hd * DK:hd * DK + HALF]
        a2 = q[:, hd * DK + HALF:(hd + 1) * DK]
        q_ref[rows, hd * DK:hd * DK + HALF] = (a1 * cq - a2 * sq).astype(BF16)
        q_ref[rows, hd * DK + HALF:(hd + 1) * DK] = (a1 * sq + a2 * cq).astype(BF16)

    kT = lax.dot_general(wkT_ref[...], h, (((1,), (1,)), ((), ())), preferred_element_type=F32)
    ck, sk = ck_ref[:, rows], sk_ref[:, rows]
    for hd in range(H):
        a1 = kT[hd * DK:hd * DK + HALF, :]
        a2 = kT[hd * DK + HALF:(hd + 1) * DK, :]
        kT_ref[hd * DK:hd * DK + HALF, rows] = (a1 * ck - a2 * sk).astype(BF16)
        kT_ref[hd * DK + HALF:(hd + 1) * DK, rows] = (a1 * sk + a2 * ck).astype(BF16)

    v = _dot(h, w_ref[:, _OV:_OV + VW]).astype(BF16)
    v_ref[rows, :] = v
    for hd in range(H):
        S = S_ref[hd]
        sb_ref[hd, ci] = S.astype(BF16)
        kz = (kT_ref[hd * DK:(hd + 1) * DK, rows].astype(F32) * ZE_ref[hd]).astype(BF16)
        g_chunk = jnp.exp(jnp.full((1, DV), lg_ref[1, hd] * C, F32))
        S_ref[hd] = S * g_chunk + _dot(kz, v[:, hd * DV:(hd + 1) * DV])

    g = _dot(h, w_ref[:, _OG:_OG + VW])
    sg_ref[rows, :] = (g * jax.nn.sigmoid(g) * gnw_ref[...]).astype(BF16)

    f = _dot(h, w_ref[:, _OF:_OF + D]).astype(BF16)
    for gi in range(NG):
        z = _dot(f[:, gi * DG:(gi + 1) * DG], md_ref[...])
        G_ref[0, rows, gi * DG:(gi + 1) * DG] = z[:, :DG].astype(BF16)
        G_ref[1, rows, gi * DG:(gi + 1) * DG] = z[:, DG:].astype(BF16)

    sga_ref[rows, :] = jax.nn.sigmoid(_dot(h, w_ref[:, _OGA:_OGA + D])).astype(BF16)
    sgf_ref[rows, :] = jax.nn.sigmoid(_dot(h, w_ref[:, _OGF:_OGF + D])).astype(BF16)


def _proj_kernel(lg_ref, x_ref, sh_ref, sc_ref, nw_ref, w_ref, wkT_ref, md_ref, gnw_ref, cq_ref, sq_ref, ck_ref,
                 sk_ref, s0_ref, pr_ref, pf_ref, wo_ref, tok_ref, kT_ref, G_ref, sb_ref,
                 prb_ref, pfb_ref, wob_ref, S_ref, ZE_ref):
    _cast_blocks((pr_ref, pf_ref, wo_ref), (prb_ref, pfb_ref, wob_ref))
    q_ref, v_ref, sg_ref, sga_ref, sgf_ref = _tok_views(tok_ref)

    @pl.when(pl.program_id(0) == 0)
    def _():
        S_ref[...] = s0_ref[...]
        c = lax.broadcasted_iota(jnp.int32, (DK, C), 1).astype(F32)
        for hd in range(H):
            ZE_ref[hd] = jnp.exp(lg_ref[1, hd] * c)

    for ci in range(PT - 1, -1, -1):
        _proj_chunk(slice(ci * C, (ci + 1) * C), ci, lg_ref, x_ref, sh_ref, sc_ref, nw_ref, w_ref, wkT_ref, md_ref,
                    gnw_ref, cq_ref, sq_ref, ck_ref, sk_ref, q_ref, kT_ref, v_ref, sg_ref, G_ref, sga_ref, sgf_ref,
                    sb_ref, S_ref, ZE_ref)


def _proj(lg, x1, tables, nw, wall, wkT, md, gnw, cq, sq, ck, sk, s_b0, cast):
    tm = PT * C
    n = L // tm
    cast_specs, cast_shapes = _cast_specs(cast, n)
    row = lambda i: (n - 1 - i, 0)
    col = lambda i: (0, n - 1 - i)
    return pl.pallas_call(
        _proj_kernel,
        grid=(n,),
        in_specs=[
            pl.BlockSpec(memory_space=pltpu.SMEM),
            pl.BlockSpec((tm, D), row),
            _mod(tables, 3)[1], _mod(tables, 4)[1],
            _resident((1, D)),
            _resident((D, _OK)),
            _resident((QK, D)),
            _resident((DG, 2 * DG)),
            _resident((1, VW)),
            pl.BlockSpec((tm, HALF), row), pl.BlockSpec((tm, HALF), row),
            pl.BlockSpec((HALF, tm), col), pl.BlockSpec((HALF, tm), col),
            _resident((H, DK, DV)),
        ] + cast_specs,
        out_specs=[
            pl.BlockSpec((tm, _TW), row),
            pl.BlockSpec((QK, tm), col),
            pl.BlockSpec((2, tm, D), lambda i: (0, n - 1 - i, 0)),
            pl.BlockSpec((H, PT, DK, DV), lambda i: (0, n - 1 - i, 0, 0)),
        ] + cast_specs,
        out_shape=[
            jax.ShapeDtypeStruct((L, _TW), BF16),
            jax.ShapeDtypeStruct((QK, L), BF16),
            jax.ShapeDtypeStruct((2, L, D), BF16),
            jax.ShapeDtypeStruct((H, L // C, DK, DV), BF16),
        ] + cast_shapes,
        scratch_shapes=[pltpu.VMEM((H, DK, DV), F32), pltpu.VMEM((H, DK, C), F32)],
        compiler_params=_cparams(1, PROJ_VMEM_LIMIT),
        name="proj",
    )(lg, x1, tables[1], tables[1], nw, wall, wkT, md, gnw, cq, sq, ck, sk, s_b0, *cast)


FFT_NT = 16
FFT_KB = 16


def _fft1_kernel(m_ref, g_ref, o_ref):
    z = jnp.transpose(g_ref[...], (2, 0, 1, 3)).reshape(FFT_NT, 2 * N1, D)
    for t in range(FFT_NT):
        o_ref[t] = _dot(m_ref[...], z[t]).astype(BF16)


def _fft1(m1, g4):
    return pl.pallas_call(
        _fft1_kernel,
        grid=(N2 // FFT_NT,),
        in_specs=[_resident((2 * N1, 2 * N1)),
                  pl.BlockSpec((2, N1, FFT_NT, D), lambda j: (0, 0, j, 0))],
        out_specs=pl.BlockSpec((FFT_NT, 2 * N1, D), lambda j: (j, 0, 0)),
        out_shape=jax.ShapeDtypeStruct((N2, 2 * N1, D), BF16),
        compiler_params=_cparams(1),
        name="fft1",
    )(m1, g4)


def _fft3_kernel(t_ref, pf_ref, a_ref, o_ref, xr_ref):
    z = jnp.transpose(a_ref[...], (2, 1, 0, 3)).reshape(FFT_KB, 2 * N2, D)
    for i in range(FFT_KB):
        xr_ref[i * N2:(i + 1) * N2, :] = _dot(t_ref[i], z[i]).astype(BF16)
    fo = _dot(xr_ref[...], pf_ref[...]).astype(BF16).reshape(FFT_KB, N2, D)
    o_ref[...] = jnp.transpose(fo, (1, 0, 2))


def _fft3(t3, a4, pf):
    return pl.pallas_call(
        _fft3_kernel,
        grid=(N1 // FFT_KB,),
        in_specs=[pl.BlockSpec((FFT_KB, N2, 2 * N2), lambda j: (j, 0, 0)),
                  _resident((D, D)),
                  pl.BlockSpec((N2, 2, FFT_KB, D), lambda j: (0, 0, j, 0))],
        out_specs=pl.BlockSpec((N2, FFT_KB, D), lambda j: (0, j, 0)),
        out_shape=jax.ShapeDtypeStruct((N2, N1, D), BF16),
        scratch_shapes=[pltpu.VMEM((FFT_KB * N2, D), BF16)],
        compiler_params=_cparams(1),
        name="fft3",
    )(t3, pf, a4)


TB_R = 512


def _retmix_kernel(lg_ref, tok_ref, kT_ref, s0_ref, sb_ref, x_ref, g_ref, fo_ref,
                   pr_ref, wo_ref, o_ref, S_ref, D_ref, XF_ref, XB_ref, ZE_ref, rin_ref):
    q_ref, v_ref, sgw_ref, sga_ref, sgf_ref = _tok_views(tok_ref)
    @pl.when(pl.program_id(0) == 0)
    def _():
        S_ref[...] = s0_ref[...]
        r = lax.broadcasted_iota(jnp.int32, (C, C), 0).astype(F32)
        c = lax.broadcasted_iota(jnp.int32, (C, C), 1).astype(F32)
        d = r - c
        for hd in range(H):
            lgf, lgb = lg_ref[0, hd], lg_ref[1, hd]
            D_ref[hd] = (jnp.where(d >= 0, jnp.exp(lgf * jnp.maximum(d, 0.0)), 0.0)
                         + jnp.where(d <= 0, jnp.exp(lgb * jnp.maximum(-d, 0.0)), 0.0))
            XF_ref[hd] = jnp.exp(lgf * (r + 1.0))
            XB_ref[hd] = jnp.exp(lgb * (C - r))
            ZE_ref[hd] = jnp.exp(lgf * (C - 1.0 - c))

    for hd in range(H):
        g_chunk = jnp.exp(jnp.full((1, DV), lg_ref[0, hd] * C, F32))
        for ci in range(TB_R // C):
            rows = slice(ci * C, (ci + 1) * C)
            qc = q_ref[rows, hd * DK:(hd + 1) * DK]
            kTc = kT_ref[hd * DK:(hd + 1) * DK, rows]
            vc = v_ref[rows, hd * DV:(hd + 1) * DV]
            S = S_ref[hd]
            p = (_dot(qc, kTc) * D_ref[hd]).astype(BF16)
            qf = (qc.astype(F32) * XF_ref[hd]).astype(BF16)
            qb = (qc.astype(F32) * XB_ref[hd]).astype(BF16)
            o = _dot(p, vc) + _dot(qf, S.astype(BF16)) + _dot(qb, sb_ref[hd, ci])
            kz = (kTc.astype(F32) * ZE_ref[hd]).astype(BF16)
            S_ref[hd] = S * g_chunk + _dot(kz, vc)
            mu = jnp.mean(o, axis=-1, keepdims=True)
            d = o - mu
            var = jnp.mean(d * d, axis=-1, keepdims=True)
            on = d * lax.rsqrt(var + EPS)
            rin_ref[rows, hd * DV:(hd + 1) * DV] = (sgw_ref[rows, hd * DV:(hd + 1) * DV].astype(F32) * on).astype(BF16)

    r = _dot(rin_ref[...], pr_ref[...])
    m = sga_ref[...].astype(F32) * r + sgf_ref[...].astype(F32) * fo_ref[...].astype(F32)
    y = _dot(m.astype(BF16), wo_ref[...])
    o_ref[...] = x_ref[...] + g_ref[0:1, :] * y


def _retmix(lg, tok, kT, s0, sb, x1, tables, fo, p_ret, w_out):
    tb = TB_R
    row = lambda i: (i, 0)
    return pl.pallas_call(
        _retmix_kernel,
        grid=(L // tb,),
        in_specs=[
            pl.BlockSpec(memory_space=pltpu.SMEM),
            pl.BlockSpec((tb, _TW), row),
            pl.BlockSpec((QK, tb), lambda i: (0, i)),
            _resident((H, DK, DV)),
            pl.BlockSpec((H, tb // C, DK, DV), lambda i: (0, i, 0, 0)),
            pl.BlockSpec((tb, D), row),
            _mod(tables, 5)[1],
            pl.BlockSpec((tb, D), row),
            _resident((VW, D)), _resident((D, D)),
        ],
        out_specs=pl.BlockSpec((tb, D), row),
        out_shape=jax.ShapeDtypeStruct((L, D), F32),
        scratch_shapes=[pltpu.VMEM((H, DK, DV), F32)] + [pltpu.VMEM((H, C, C), F32)] * 4
                       + [pltpu.VMEM((tb, VW), BF16)],
        compiler_params=_cparams(1),
        name="retmix",
    )(lg, tok, kT, s0, sb, x1, tables[1], fo, p_ret, w_out)


def _rope_tables():
    rows = L // GRID_W
    row = np.repeat(np.arange(rows, dtype=np.float64), GRID_W)
    col = np.tile(np.arange(GRID_W, dtype=np.float64), rows)
    n_freq = DK // 4
    inv = ROPE_BASE ** (-np.arange(n_freq, dtype=np.float64) / n_freq)
    ang = np.concatenate([row[:, None] * inv, col[:, None] * inv], axis=-1)
    return np.cos(ang), np.sin(ang)


def _dft_tables():
    two_pi = 2.0 * np.pi
    th = (np.outer(np.arange(DG), np.arange(DG)) % DG) * (two_pi / DG)
    md = np.concatenate([np.cos(th), -np.sin(th)], axis=1)
    th = (np.outer(np.arange(N1), np.arange(N1)) % N1) * (two_pi / N1)
    c1, s1 = np.cos(th), np.sin(th)
    m1 = np.block([[c1, s1], [-s1, c1]])
    k = np.arange(N1)[:, None, None] + N1 * np.arange(N2)[None, :, None]
    th = ((k * np.arange(N2)[None, None, :]) % L) * (two_pi / L)
    t3 = np.concatenate([np.cos(th), np.sin(th)], axis=-1) / np.sqrt(float(L) * DG)
    f32 = lambda a: jnp.asarray(a.astype(np.float32))
    return f32(md).astype(BF16), f32(m1).astype(BF16), f32(t3).astype(BF16)


def kernel(x, c, ctx, c_ctx, w_ada, b_ada, norm_ffn1, w13_ffn1, w2_ffn1, norm_mix, w_in, decay_fwd, decay_bwd,
           ret_gn_w, p_ret, p_four, w_out, norm_ffn2, w13_ffn2, w2_ffn2, norm_final):
    assert x.shape == (1, L, D) and ctx.shape == (1, CTX, D) and w_ada.shape[0] == 1

    pm = np.zeros((DK, DK), np.float32)
    pm[np.concatenate([np.arange(0, DK, 2), np.arange(1, DK, 2)]), np.arange(DK)] = 1.0
    perm = jnp.asarray(pm).astype(BF16)
    eye = jnp.asarray(np.eye(DK, dtype=np.float32)).astype(BF16)

    lg = jnp.stack([jax.nn.log_sigmoid(decay_fwd[0].astype(F32)),
                    jax.nn.log_sigmoid(decay_bwd[0].astype(F32))])
    cos, sin = _rope_tables()
    qscale = DK ** -0.5
    f32c = lambda a: jnp.asarray(np.ascontiguousarray(a).astype(np.float32))
    cq, sq, ck, sk = f32c(cos * qscale), f32c(sin * qscale), f32c(cos.T), f32c(sin.T)
    md, m1, t3 = _dft_tables()

    cs = jnp.concatenate([c, c_ctx[None, :], jnp.zeros((6, D), F32)], axis=0).T
    mods_a = _mods(cs, w_ada[0], b_ada)

    x1, ctx1, w13b, w2b, wall, mods_b = _ffn1(x[0], ctx[0], mods_a, norm_ffn1, w13_ffn1[0], w2_ffn1[0],
                                              (w13_ffn2[0], w2_ffn2[0]), w_in[0], perm, w_ada[0], cs, b_ada)
    mods = (mods_a, mods_b)
    wkT = _kT(wall, eye)

    s_f, s_b = _ctx_state(lg, ctx1, mods, norm_mix, wkT, wall)

    tok, kT, G, sb, pr, pf, wo = _proj(lg, x1, mods, norm_mix, wall, wkT, md, ret_gn_w, cq, sq, ck, sk,
                                                     s_b, (p_ret[0], p_four[0], w_out[0]))
    a = _fft1(m1, G.reshape(2, N1, N2, D))
    fo = _fft3(t3, a.reshape(N2, 2, N1, D), pf).reshape(L, D)

    x2 = _retmix(lg, tok, kT, s_f, sb, x1, mods, fo, pr, wo)
    out = _ffn2(x2, mods, norm_ffn2, w13b, w2b, norm_final[None, :])
    return out[None]
```

```python
import functools

import numpy as np
import jax
import jax.numpy as jnp
from jax import lax
from jax.experimental import pallas as pl
from jax.experimental.pallas import tpu as pltpu

D = 1024
L = 16384
GRID_W = 64
CTX = 256
H = 4
DK = 256
DV = 512
QK = H * DK
VW = H * DV
NG = 4
DG = 256
DFF = 2816
ROPE_BASE = 10000.0
EPS = 1e-6
NMOD = 9

F32 = jnp.float32
BF16 = jnp.bfloat16

C = 256
N1 = 128
N2 = 128

VMEM_LIMIT = 56 * 1024 * 1024


def _dot(a, b):
    return jnp.dot(a, b, preferred_element_type=F32)


def _cparams(n_axes, vmem_limit=VMEM_LIMIT):
    return pltpu.CompilerParams(
        dimension_semantics=("arbitrary",) * n_axes, vmem_limit_bytes=vmem_limit)


def _resident(shape):
    nd = len(shape)
    return pl.BlockSpec(shape, lambda *_: (0,) * nd, pipeline_mode=pl.Buffered(1))


def _mod_spec(j):
    return pl.BlockSpec((8, D), lambda *_: (0, j), pipeline_mode=pl.Buffered(1))


NMOD_A = 3


def _mod(tables, j):
    return (tables[0], _mod_spec(j)) if j < NMOD_A else (tables[1], _mod_spec(j - NMOD_A))


def _norm_mod(x, nw, sc, sh):
    y = x * lax.rsqrt(jnp.mean(x * x, axis=-1, keepdims=True) + EPS)
    return y * (nw * (1.0 + sc)) + sh


def _mods_block(w_ref, cs_ref, b_ref, o_ref):
    cs = cs_ref[...]
    s = cs * jax.nn.sigmoid(cs)
    w = w_ref[...]
    rows = [jnp.sum(w * s[:, r:r + 1], axis=0, keepdims=True) for r in range(2)]
    o_ref[...] = jnp.concatenate(rows + [jnp.zeros((6, w.shape[1]), F32)], axis=0) + b_ref[...]


def _mods_kernel(cs_ref, w_ref, b_ref, o_ref):
    _mods_block(w_ref, cs_ref, b_ref, o_ref)


def _mods(cs, w_ada, b_ada):
    tn = NMOD_A * D // 2
    return pl.pallas_call(
        _mods_kernel,
        grid=(2,),
        in_specs=[
            _resident((D, 8)),
            pl.BlockSpec((D, tn), lambda j: (0, j)),
            pl.BlockSpec((1, tn), lambda j: (0, j)),
        ],
        out_specs=pl.BlockSpec((8, tn), lambda j: (0, j)),
        out_shape=jax.ShapeDtypeStruct((8, NMOD_A * D), F32),
        compiler_params=_cparams(1),
        name="mods",
    )(cs, w_ada, b_ada)


def _cast_blocks(src_refs, dst_refs):
    for s, d in zip(src_refs, dst_refs):
        d[...] = s[...].astype(BF16)


def _cast_specs(arrays, steps):
    specs = [pl.BlockSpec((a.shape[0] // steps, a.shape[1]), lambda i: (i, 0)) for a in arrays]
    shapes = [jax.ShapeDtypeStruct(a.shape, BF16) for a in arrays]
    return specs, shapes


def _prep_block(win_ref, p_ref, wall_ref):
    w = win_ref[...].astype(BF16)
    rb = w.shape[0]
    heads = jnp.concatenate([w[:, hd * DK:(hd + 1) * DK] for hd in range(2 * H)], axis=0)
    perm = _dot(heads, p_ref[...]).astype(BF16)
    for hd in range(H):
        wall_ref[:, _OQ + hd * DK:_OQ + (hd + 1) * DK] = perm[hd * rb:(hd + 1) * rb]
        wall_ref[:, _OK + hd * DK:_OK + (hd + 1) * DK] = perm[(H + hd) * rb:(H + hd + 1) * rb]
    wall_ref[:, QK:_OK] = w[:, 2 * QK:]


MODS_TN = 256


def _kT_kernel(eye_ref, w_ref, o_ref):
    o_ref[...] = lax.dot_general(eye_ref[...], w_ref[...], (((1,), (1,)), ((), ())),
                                 preferred_element_type=F32).astype(BF16)


def _kT(wall, eye):
    return pl.pallas_call(
        _kT_kernel,
        grid=(H,),
        in_specs=[_resident((DK, DK)), pl.BlockSpec((D, DK), lambda h: (0, _OK // DK + h))],
        out_specs=pl.BlockSpec((DK, D), lambda h: (h, 0)),
        out_shape=jax.ShapeDtypeStruct((QK, D), BF16),
        compiler_params=_cparams(1),
        name="kT",
    )(eye, wall)


FC = 256


def _swiglu_rows(x, sh, sc, g, nw, w13_ref, w2_ref, t_ref):
    h = _norm_mod(x, nw, sc, sh).astype(BF16)
    for j in range(DFF // FC):
        a = _dot(h, w13_ref[:, j * FC:(j + 1) * FC])
        b = _dot(h, w13_ref[:, DFF + j * FC:DFF + (j + 1) * FC])
        t_ref[:, j * FC:(j + 1) * FC] = (a * jax.nn.sigmoid(a) * b).astype(BF16)
    return x + (0.5 * g) * _dot(t_ref[...], w2_ref[...])


W13_CH = 512
W2_CH = 256


def _load_cast(src_hbm, dst_ref, stage_ref, sem_ref, axis, chunk):
    n = src_hbm.shape[axis] // chunk

    def window(c):
        return (slice(None), pl.ds(c * chunk, chunk)) if axis == 1 else (pl.ds(c * chunk, chunk), slice(None))

    def copy(c):
        return pltpu.make_async_copy(src_hbm.at[window(c)], stage_ref.at[c % 2], sem_ref.at[c % 2])

    copy(0).start()
    for c in range(n):
        if c + 1 < n:
            copy(c + 1).start()
        copy(c).wait()
        dst_ref[window(c)] = stage_ref[c % 2].astype(BF16)


def _ffn1_kernel(ncast, x_ref, ctx_ref, sh_ref, sc_ref, g_ref, nw_ref, w13_hbm, w2_hbm, *rest):
    cast_in, rest = rest[:ncast], rest[ncast:]
    (win_ref, p_ref, wada_ref, cs_ref, b_ref), rest = rest[:5], rest[5:]
    (o_ref, octx_ref), rest = rest[:2], rest[2:]
    cast_out, rest = rest[:ncast], rest[ncast:]
    wall_ref, mb_ref, t_ref, w13_ref, w2_ref, st13_ref, st2_ref, sem_ref = rest

    @pl.when(pl.program_id(0) == 0)
    def _():
        _load_cast(w13_hbm, w13_ref, st13_ref, sem_ref, 1, W13_CH)
        _load_cast(w2_hbm, w2_ref, st2_ref, sem_ref, 0, W2_CH)
        octx_ref[...] = _swiglu_rows(ctx_ref[...], sh_ref[1:2, :], sc_ref[1:2, :], g_ref[1:2, :], nw_ref[...],
                                     w13_ref, w2_ref, t_ref.at[pl.ds(0, CTX)])

    o_ref[...] = _swiglu_rows(x_ref[...], sh_ref[0:1, :], sc_ref[0:1, :], g_ref[0:1, :], nw_ref[...],
                              w13_ref, w2_ref, t_ref)
    _cast_blocks(cast_in, cast_out)
    _prep_block(win_ref, p_ref, wall_ref)
    _mods_block(wada_ref, cs_ref, b_ref, mb_ref)


def _ffn1(x, ctx, mods_a, nw, w13, w2, cast, w_in, perm, w_ada, cs, b_ada, tm=512):
    n = x.shape[0]
    steps = n // tm
    rb = D // steps
    cast_specs, cast_shapes = _cast_specs(cast, steps)
    nlate = (NMOD - NMOD_A) * D
    first, last = NMOD_A * D // MODS_TN, NMOD * D // MODS_TN - 1
    late_in = lambda i: (0, jnp.minimum(first + i, last))
    late_out = lambda i: (0, jnp.minimum(i, last - first))
    rows = lambda i: (i, 0)
    return pl.pallas_call(
        functools.partial(_ffn1_kernel, len(cast)),
        grid=(steps,),
        in_specs=[pl.BlockSpec((tm, D), rows), _resident((CTX, D)),
                  _mod_spec(0), _mod_spec(1), _mod_spec(2), _resident((1, D)),
                  pl.BlockSpec(memory_space=pl.ANY), pl.BlockSpec(memory_space=pl.ANY)]
                 + cast_specs
                 + [pl.BlockSpec((rb, _WIN), rows), _resident((DK, DK)),
                    pl.BlockSpec((D, MODS_TN), late_in), _resident((D, 8)), pl.BlockSpec((1, MODS_TN), late_in)],
        out_specs=[pl.BlockSpec((tm, D), rows), pl.BlockSpec((CTX, D), lambda i: (0, 0))]
                  + cast_specs
                  + [pl.BlockSpec((rb, _WIN), rows), pl.BlockSpec((8, MODS_TN), late_out)],
        out_shape=[jax.ShapeDtypeStruct((n, D), F32), jax.ShapeDtypeStruct((CTX, D), F32)]
                  + cast_shapes
                  + [jax.ShapeDtypeStruct((D, _WIN), BF16), jax.ShapeDtypeStruct((8, nlate), F32)],
        scratch_shapes=[pltpu.VMEM((tm, DFF), BF16), pltpu.VMEM((D, 2 * DFF), BF16), pltpu.VMEM((DFF, D), BF16),
                        pltpu.VMEM((2, D, W13_CH), F32), pltpu.VMEM((2, W2_CH, D), F32),
                        pltpu.SemaphoreType.DMA((2,))],
        compiler_params=_cparams(1),
        name="ffn1",
    )(x, ctx, mods_a, mods_a, mods_a, nw, w13, w2, *cast, w_in, perm, w_ada, cs, b_ada)


def _ffn2_kernel(x_ref, sh_ref, sc_ref, g_ref, nw_ref, w13_ref, w2_ref, nf_ref, o_ref, t_ref):
    out = _swiglu_rows(x_ref[...], sh_ref[0:1, :], sc_ref[0:1, :], g_ref[0:1, :], nw_ref[...], w13_ref, w2_ref, t_ref)
    o_ref[...] = out * lax.rsqrt(jnp.mean(out * out, axis=-1, keepdims=True) + EPS) * nf_ref[...]


def _ffn2(x, tables, nw, w13, w2, nf, tm=1024):
    n = x.shape[0]
    mod_arrays, mod_specs = zip(*[_mod(tables, 6 + k) for k in range(3)])
    return pl.pallas_call(
        _ffn2_kernel,
        grid=(n // tm,),
        in_specs=[pl.BlockSpec((tm, D), lambda i: (i, 0)), *mod_specs,
                  _resident((1, D)), _resident((D, 2 * DFF)), _resident((DFF, D)), _resident((1, D))],
        out_specs=pl.BlockSpec((tm, D), lambda i: (i, 0)),
        out_shape=jax.ShapeDtypeStruct((n, D), F32),
        scratch_shapes=[pltpu.VMEM((tm, DFF), BF16)],
        compiler_params=_cparams(1),
        name="ffn2",
    )(x, *mod_arrays, nw, w13, w2, nf)


def _ctx_state_kernel(lg_ref, x_ref, sh_ref, sc_ref, nw_ref, wkT_ref, wv_ref, sf_ref, sb_ref):
    hd = pl.program_id(0)
    h = _norm_mod(x_ref[...], nw_ref[...], sc_ref[1:2, :], sh_ref[1:2, :]).astype(BF16)
    kT = lax.dot_general(wkT_ref[...], h, (((1,), (1,)), ((), ())), preferred_element_type=F32)
    v = _dot(h, wv_ref[...]).astype(BF16)
    j = lax.broadcasted_iota(jnp.int32, (DK, CTX), 1).astype(F32)
    wf = jnp.exp(lg_ref[0, hd] * (CTX - 1.0 - j))
    wb = jnp.exp(lg_ref[1, hd] * j)
    sf_ref[0] = _dot((kT * wf).astype(BF16), v)
    sb_ref[0] = _dot((kT * wb).astype(BF16), v)


def _ctx_state(lg, ctx1, tables, nw, wkT, wall):
    return pl.pallas_call(
        _ctx_state_kernel,
        grid=(H,),
        in_specs=[
            pl.BlockSpec(memory_space=pltpu.SMEM),
            _resident((CTX, D)),
            _mod(tables, 3)[1], _mod(tables, 4)[1],
            _resident((1, D)),
            pl.BlockSpec((DK, D), lambda h: (h, 0)),
            pl.BlockSpec((D, DV), lambda h: (0, _OV // DV + h)),
        ],
        out_specs=[pl.BlockSpec((1, DK, DV), lambda h: (h, 0, 0))] * 2,
        out_shape=[jax.ShapeDtypeStruct((H, DK, DV), F32)] * 2,
        compiler_params=_cparams(1),
        name="ctx_state",
    )(lg, ctx1, tables[1], tables[1], nw, wkT, wall)


_OQ, _OV, _OG = 0, QK, QK + VW
_OF, _OGA, _OGF = QK + 2 * VW, QK + 2 * VW + D, QK + 2 * VW + 2 * D
_WIN = 2 * QK + 2 * VW + 3 * D
_OK = _WIN - QK
HALF = DK // 2

PT = 2
PROJ_VMEM_LIMIT = 58 * 1024 * 1024


def _proj_chunk(rows, ci, lg_ref, x_ref, sh_ref, sc_ref, nw_ref, w_ref, wkT_ref, md_ref, gnw_ref, cq_ref, sq_ref,
                ck_ref, sk_ref, q_ref, kT_ref, v_ref, sg_ref, G_ref, sga_ref, sgf_ref, sb_ref, S_ref, ZE_ref):
    h = _norm_mod(x_ref[rows, :], nw_ref[...], sc_ref[0:1, :], sh_ref[0:1, :]).astype(BF16)

    q = _dot(h, w_ref[:, _OQ:_OQ + QK])
    cq, sq = cq_ref[rows, :], sq_ref[rows, :]
    for hd in range(H):
        a1 = q[:, hd * DK:hd * DK + HALF]
        a2 = q[:, hd * DK + HALF:(hd + 1) * DK]
        q_ref[rows, hd * DK:hd * DK + HALF] = (a1 * cq - a2 * sq).astype(BF16)
        q_ref[rows, hd * DK + HALF:(hd + 1) * DK] = (a1 * sq + a2 * cq).astype(BF16)

    kT = lax.dot_general(wkT_ref[...], h, (((1,), (1,)), ((), ())), preferred_element_type=F32)
    ck, sk = ck_ref[:, rows], sk_ref[:, rows]
    for hd in range(H):
        a1 = kT[hd * DK:hd * DK + HALF, :]
        a2 = kT[hd * DK + HALF:(hd + 1) * DK, :]
        kT_ref[hd * DK:hd * DK + HALF, rows] = (a1 * ck - a2 * sk).astype(BF16)
        kT_ref[hd * DK + HALF:(hd + 1) * DK, rows] = (a1 * sk + a2 * ck).astype(BF16)

    v = _dot(h, w_ref[:, _OV:_OV + VW]).astype(BF16)
    v_ref[rows, :] = v
    for hd in range(H):
        S = S_ref[hd]
        sb_ref[hd, ci] = S.astype(BF16)
        kz = (kT_ref[hd * DK:(hd + 1) * DK, rows].astype(F32) * ZE_ref[hd]).astype(BF16)
        g_chunk = jnp.exp(jnp.full((1, DV), lg_ref[1, hd] * C, F32))
        S_ref[hd] = S * g_chunk + _dot(kz, v[:, hd * DV:(hd + 1) * DV])

    g = _dot(h, w_ref[:, _OG:_OG + VW])
    sg_ref[rows, :] = (g * jax.nn.sigmoid(g) * gnw_ref[...]).astype(BF16)

    f = _dot(h, w_ref[:, _OF:_OF + D]).astype(BF16)
    for gi in range(NG):
        z = _dot(f[:, gi * DG:(gi + 1) * DG], md_ref[...])
        G_ref[0, rows, gi * DG:(gi + 1) * DG] = z[:, :DG].astype(BF16)
        G_ref[1, rows, gi * DG:(gi + 1) * DG] = z[:, DG:].astype(BF16)

    sga_ref[rows, :] = jax.nn.sigmoid(_dot(h, w_ref[:, _OGA:_OGA + D])).astype(BF16)
    sgf_ref[rows, :] = jax.nn.sigmoid(_dot(h, w_ref[:, _OGF:_OGF + D])).astype(BF16)


def _proj_kernel(lg_ref, x_ref, sh_ref, sc_ref, nw_ref, w_ref, wkT_ref, md_ref, gnw_ref, cq_ref, sq_ref, ck_ref,
                 sk_ref, s0_ref, pr_ref, pf_ref, wo_ref, q_ref, kT_ref, v_ref, sg_ref, G_ref, sga_ref, sgf_ref, sb_ref,
                 prb_ref, pfb_ref, wob_ref, S_ref, ZE_ref):
    _cast_blocks((pr_ref, pf_ref, wo_ref), (prb_ref, pfb_ref, wob_ref))

    @pl.when(pl.program_id(0) == 0)
    def _():
        S_ref[...] = s0_ref[...]
        c = lax.broadcasted_iota(jnp.int32, (DK, C), 1).astype(F32)
        for hd in range(H):
            ZE_ref[hd] = jnp.exp(lg_ref[1, hd] * c)

    for ci in range(PT - 1, -1, -1):
        _proj_chunk(slice(ci * C, (ci + 1) * C), ci, lg_ref, x_ref, sh_ref, sc_ref, nw_ref, w_ref, wkT_ref, md_ref,
                    gnw_ref, cq_ref, sq_ref, ck_ref, sk_ref, q_ref, kT_ref, v_ref, sg_ref, G_ref, sga_ref, sgf_ref,
                    sb_ref, S_ref, ZE_ref)


def _proj(lg, x1, tables, nw, wall, wkT, md, gnw, cq, sq, ck, sk, s_b0, cast):
    tm = PT * C
    n = L // tm
    cast_specs, cast_shapes = _cast_specs(cast, n)
    row = lambda i: (n - 1 - i, 0)
    col = lambda i: (0, n - 1 - i)
    return pl.pallas_call(
        _proj_kernel,
        grid=(n,),
        in_specs=[
            pl.BlockSpec(memory_space=pltpu.SMEM),
            pl.BlockSpec((tm, D), row),
            _mod(tables, 3)[1], _mod(tables, 4)[1],
            _resident((1, D)),
            _resident((D, _OK)),
            _resident((QK, D)),
            _resident((DG, 2 * DG)),
            _resident((1, VW)),
            pl.BlockSpec((tm, HALF), row), pl.BlockSpec((tm, HALF), row),
            pl.BlockSpec((HALF, tm), col), pl.BlockSpec((HALF, tm), col),
            _resident((H, DK, DV)),
        ] + cast_specs,
        out_specs=[
            pl.BlockSpec((tm, QK), row),
            pl.BlockSpec((QK, tm), col),
            pl.BlockSpec((tm, VW), row),
            pl.BlockSpec((tm, VW), row),
            pl.BlockSpec((2, tm, D), lambda i: (0, n - 1 - i, 0)),
            pl.BlockSpec((tm, D), row),
            pl.BlockSpec((tm, D), row),
            pl.BlockSpec((H, PT, DK, DV), lambda i: (0, n - 1 - i, 0, 0)),
        ] + cast_specs,
        out_shape=[
            jax.ShapeDtypeStruct((L, QK), BF16),
            jax.ShapeDtypeStruct((QK, L), BF16),
            jax.ShapeDtypeStruct((L, VW), BF16),
            jax.ShapeDtypeStruct((L, VW), BF16),
            jax.ShapeDtypeStruct((2, L, D), BF16),
            jax.ShapeDtypeStruct((L, D), BF16),
            jax.ShapeDtypeStruct((L, D), BF16),
            jax.ShapeDtypeStruct((H, L // C, DK, DV), BF16),
        ] + cast_shapes,
        scratch_shapes=[pltpu.VMEM((H, DK, DV), F32), pltpu.VMEM((H, DK, C), F32)],
        compiler_params=_cparams(1, PROJ_VMEM_LIMIT),
        name="proj",
    )(lg, x1, tables[1], tables[1], nw, wall, wkT, md, gnw, cq, sq, ck, sk, s_b0, *cast)


FFT_NT = 16
FFT_KB = 16


def _fft1_kernel(m_ref, g_ref, o_ref, r_ref):
    z = jnp.transpose(g_ref[...], (2, 0, 1, 3)).reshape(FFT_NT, 2 * N1, D)
    for t in range(FFT_NT):
        r_ref[t] = _dot(m_ref[...], z[t]).astype(BF16).reshape(2, N1, D)
    o_ref[...] = jnp.transpose(r_ref[...], (2, 1, 0, 3))


def _fft1(m1, g4):
    return pl.pallas_call(
        _fft1_kernel,
        grid=(N2 // FFT_NT,),
        in_specs=[_resident((2 * N1, 2 * N1)),
                  pl.BlockSpec((2, N1, FFT_NT, D), lambda j: (0, 0, j, 0))],
        out_specs=pl.BlockSpec((N1, 2, FFT_NT, D), lambda j: (0, 0, j, 0)),
        out_shape=jax.ShapeDtypeStruct((N1, 2, N2, D), BF16),
        scratch_shapes=[pltpu.VMEM((FFT_NT, 2, N1, D), BF16)],
        compiler_params=_cparams(1),
        name="fft1",
    )(m1, g4)


def _fft3_kernel(t_ref, pf_ref, a_ref, o_ref, xr_ref):
    for i in range(FFT_KB):
        xr_ref[i * N2:(i + 1) * N2, :] = _dot(t_ref[i], a_ref[i].reshape(2 * N2, D)).astype(BF16)
    fo = _dot(xr_ref[...], pf_ref[...]).astype(BF16).reshape(FFT_KB, N2, D)
    o_ref[...] = jnp.transpose(fo, (1, 0, 2))


def _fft3(t3, a4, pf):
    return pl.pallas_call(
        _fft3_kernel,
        grid=(N1 // FFT_KB,),
        in_specs=[pl.BlockSpec((FFT_KB, N2, 2 * N2), lambda j: (j, 0, 0)),
                  _resident((D, D)),
                  pl.BlockSpec((FFT_KB, 2, N2, D), lambda j: (j, 0, 0, 0))],
        out_specs=pl.BlockSpec((N2, FFT_KB, D), lambda j: (0, j, 0)),
        out_shape=jax.ShapeDtypeStruct((N2, N1, D), BF16),
        scratch_shapes=[pltpu.VMEM((FFT_KB * N2, D), BF16)],
        compiler_params=_cparams(1),
        name="fft3",
    )(t3, pf, a4)


TB_R = 512


def _retmix_kernel(lg_ref, q_ref, kT_ref, v_ref, s0_ref, sb_ref, sgw_ref, x_ref, g_ref, fo_ref, sga_ref, sgf_ref,
                   pr_ref, wo_ref, o_ref, S_ref, D_ref, XF_ref, XB_ref, ZE_ref, rin_ref):
    @pl.when(pl.program_id(0) == 0)
    def _():
        S_ref[...] = s0_ref[...]
        r = lax.broadcasted_iota(jnp.int32, (C, C), 0).astype(F32)
        c = lax.broadcasted_iota(jnp.int32, (C, C), 1).astype(F32)
        d = r - c
        for hd in range(H):
            lgf, lgb = lg_ref[0, hd], lg_ref[1, hd]
            D_ref[hd] = (jnp.where(d >= 0, jnp.exp(lgf * jnp.maximum(d, 0.0)), 0.0)
                         + jnp.where(d <= 0, jnp.exp(lgb * jnp.maximum(-d, 0.0)), 0.0))
            XF_ref[hd] = jnp.exp(lgf * (r + 1.0))
            XB_ref[hd] = jnp.exp(lgb * (C - r))
            ZE_ref[hd] = jnp.exp(lgf * (C - 1.0 - c))

    for hd in range(H):
        g_chunk = jnp.exp(jnp.full((1, DV), lg_ref[0, hd] * C, F32))
        for ci in range(TB_R // C):
            rows = slice(ci * C, (ci + 1) * C)
            qc = q_ref[rows, hd * DK:(hd + 1) * DK]
            kTc = kT_ref[hd * DK:(hd + 1) * DK, rows]
            vc = v_ref[rows, hd * DV:(hd + 1) * DV]
            S = S_ref[hd]
            p = (_dot(qc, kTc) * D_ref[hd]).astype(BF16)
            qf = (qc.astype(F32) * XF_ref[hd]).astype(BF16)
            qb = (qc.astype(F32) * XB_ref[hd]).astype(BF16)
            o = _dot(p, vc) + _dot(qf, S.astype(BF16)) + _dot(qb, sb_ref[hd, ci])
            kz = (kTc.astype(F32) * ZE_ref[hd]).astype(BF16)
            S_ref[hd] = S * g_chunk + _dot(kz, vc)
            mu = jnp.mean(o, axis=-1, keepdims=True)
            d = o - mu
            var = jnp.mean(d * d, axis=-1, keepdims=True)
            on = d * lax.rsqrt(var + EPS)
            rin_ref[rows, hd * DV:(hd + 1) * DV] = (sgw_ref[rows, hd * DV:(hd + 1) * DV].astype(F32) * on).astype(BF16)

    r = _dot(rin_ref[...], pr_ref[...])
    m = sga_ref[...].astype(F32) * r + sgf_ref[...].astype(F32) * fo_ref[...].astype(F32)
    y = _dot(m.astype(BF16), wo_ref[...])
    o_ref[...] = x_ref[...] + g_ref[0:1, :] * y


def _retmix(lg, q, kT, v, s0, sb, sgw, x1, tables, fo, sga, sgf, p_ret, w_out):
    tb = TB_R
    row = lambda i: (i, 0)
    return pl.pallas_call(
        _retmix_kernel,
        grid=(L // tb,),
        in_specs=[
            pl.BlockSpec(memory_space=pltpu.SMEM),
            pl.BlockSpec((tb, QK), row),
            pl.BlockSpec((QK, tb), lambda i: (0, i)),
            pl.BlockSpec((tb, VW), row),
            _resident((H, DK, DV)),
            pl.BlockSpec((H, tb // C, DK, DV), lambda i: (0, i, 0, 0)),
            pl.BlockSpec((tb, VW), row),
            pl.BlockSpec((tb, D), row),
            _mod(tables, 5)[1],
            pl.BlockSpec((tb, D), row), pl.BlockSpec((tb, D), row), pl.BlockSpec((tb, D), row),
            _resident((VW, D)), _resident((D, D)),
        ],
        out_specs=pl.BlockSpec((tb, D), row),
        out_shape=jax.ShapeDtypeStruct((L, D), F32),
        scratch_shapes=[pltpu.VMEM((H, DK, DV), F32)] + [pltpu.VMEM((H, C, C), F32)] * 4
                       + [pltpu.VMEM((tb, VW), BF16)],
        compiler_params=_cparams(1),
        name="retmix",
    )(lg, q, kT, v, s0, sb, sgw, x1, tables[1], fo, sga, sgf, p_ret, w_out)


def _rope_tables():
    rows = L // GRID_W
    row = np.repeat(np.arange(rows, dtype=np.float64), GRID_W)
    col = np.tile(np.arange(GRID_W, dtype=np.float64), rows)
    n_freq = DK // 4
    inv = ROPE_BASE ** (-np.arange(n_freq, dtype=np.float64) / n_freq)
    ang = np.concatenate([row[:, None] * inv, col[:, None] * inv], axis=-1)
    return np.cos(ang), np.sin(ang)


def _dft_tables():
    two_pi = 2.0 * np.pi
    th = (np.outer(np.arange(DG), np.arange(DG)) % DG) * (two_pi / DG)
    md = np.concatenate([np.cos(th), -np.sin(th)], axis=1)
    th = (np.outer(np.arange(N1), np.arange(N1)) % N1) * (two_pi / N1)
    c1, s1 = np.cos(th), np.sin(th)
    m1 = np.block([[c1, s1], [-s1, c1]])
    k = np.arange(N1)[:, None, None] + N1 * np.arange(N2)[None, :, None]
    th = ((k * np.arange(N2)[None, None, :]) % L) * (two_pi / L)
    t3 = np.concatenate([np.cos(th), np.sin(th)], axis=-1) / np.sqrt(float(L) * DG)
    f32 = lambda a: jnp.asarray(a.astype(np.float32))
    return f32(md).astype(BF16), f32(m1).astype(BF16), f32(t3).astype(BF16)


def kernel(x, c, ctx, c_ctx, w_ada, b_ada, norm_ffn1, w13_ffn1, w2_ffn1, norm_mix, w_in, decay_fwd, decay_bwd,
           ret_gn_w, p_ret, p_four, w_out, norm_ffn2, w13_ffn2, w2_ffn2, norm_final):
    assert x.shape == (1, L, D) and ctx.shape == (1, CTX, D) and w_ada.shape[0] == 1

    pm = np.zeros((DK, DK), np.float32)
    pm[np.concatenate([np.arange(0, DK, 2), np.arange(1, DK, 2)]), np.arange(DK)] = 1.0
    perm = jnp.asarray(pm).astype(BF16)
    eye = jnp.asarray(np.eye(DK, dtype=np.float32)).astype(BF16)

    lg = jnp.stack([jax.nn.log_sigmoid(decay_fwd[0].astype(F32)),
                    jax.nn.log_sigmoid(decay_bwd[0].astype(F32))])
    cos, sin = _rope_tables()
    qscale = DK ** -0.5
    f32c = lambda a: jnp.asarray(np.ascontiguousarray(a).astype(np.float32))
    cq, sq, ck, sk = f32c(cos * qscale), f32c(sin * qscale), f32c(cos.T), f32c(sin.T)
    md, m1, t3 = _dft_tables()

    cs = jnp.concatenate([c, c_ctx[None, :], jnp.zeros((6, D), F32)], axis=0).T
    mods_a = _mods(cs, w_ada[0], b_ada)

    x1, ctx1, w13b, w2b, wall, mods_b = _ffn1(x[0], ctx[0], mods_a, norm_ffn1, w13_ffn1[0], w2_ffn1[0],
                                              (w13_ffn2[0], w2_ffn2[0]), w_in[0], perm, w_ada[0], cs, b_ada)
    mods = (mods_a, mods_b)
    wkT = _kT(wall, eye)

    s_f, s_b = _ctx_state(lg, ctx1, mods, norm_mix, wkT, wall)

    q, kT, v, sgw, G, sga, sgf, sb, pr, pf, wo = _proj(lg, x1, mods, norm_mix, wall, wkT, md, ret_gn_w, cq, sq, ck, sk,
                                                     s_b, (p_ret[0], p_four[0], w_out[0]))
    a = _fft1(m1, G.reshape(2, N1, N2, D))
    fo = _fft3(t3, a, pf).reshape(L, D)

    x2 = _retmix(lg, q, kT, v, s_f, sb, sgw, x1, mods, fo, sga, sgf, pr, wo)
    out = _ffn2(x2, mods, norm_ffn2, w13b, w2b, norm_final[None, :])
    return out[None]
```

```python
import functools

import numpy as np
import jax
import jax.numpy as jnp
from jax import lax
from jax.experimental import pallas as pl
from jax.experimental.pallas import tpu as pltpu

D = 1024
L = 16384
GRID_W = 64
CTX = 256
H = 4
DK = 256
DV = 512
QK = H * DK
VW = H * DV
NG = 4
DG = 256
DFF = 2816
ROPE_BASE = 10000.0
EPS = 1e-6
NMOD = 9

F32 = jnp.float32
BF16 = jnp.bfloat16

C = 256
N1 = 128
N2 = 128

VMEM_LIMIT = 56 * 1024 * 1024


def _dot(a, b):
    return jnp.dot(a, b, preferred_element_type=F32)


def _cparams(n_axes, vmem_limit=VMEM_LIMIT):
    return pltpu.CompilerParams(
        dimension_semantics=("arbitrary",) * n_axes, vmem_limit_bytes=vmem_limit)


def _resident(shape):
    nd = len(shape)
    return pl.BlockSpec(shape, lambda *_: (0,) * nd, pipeline_mode=pl.Buffered(1))


def _mod_spec(j):
    return pl.BlockSpec((8, D), lambda *_: (0, j), pipeline_mode=pl.Buffered(1))


NMOD_A = 3


def _mod(tables, j):
    return (tables[0], _mod_spec(j)) if j < NMOD_A else (tables[1], _mod_spec(j - NMOD_A))


def _norm_mod(x, nw, sc, sh):
    y = x * lax.rsqrt(jnp.mean(x * x, axis=-1, keepdims=True) + EPS)
    return y * (nw * (1.0 + sc)) + sh


def _mods_block(w_ref, cs_ref, b_ref, o_ref):
    cs = cs_ref[...]
    s = cs * jax.nn.sigmoid(cs)
    w = w_ref[...]
    rows = [jnp.sum(w * s[:, r:r + 1], axis=0, keepdims=True) for r in range(2)]
    o_ref[...] = jnp.concatenate(rows + [jnp.zeros((6, w.shape[1]), F32)], axis=0) + b_ref[...]


def _mods_kernel(cs_ref, w_ref, b_ref, o_ref):
    _mods_block(w_ref, cs_ref, b_ref, o_ref)


def _mods(cs, w_ada, b_ada):
    tn = NMOD_A * D // 2
    return pl.pallas_call(
        _mods_kernel,
        grid=(2,),
        in_specs=[
            _resident((D, 8)),
            pl.BlockSpec((D, tn), lambda j: (0, j)),
            pl.BlockSpec((1, tn), lambda j: (0, j)),
        ],
        out_specs=pl.BlockSpec((8, tn), lambda j: (0, j)),
        out_shape=jax.ShapeDtypeStruct((8, NMOD_A * D), F32),
        compiler_params=_cparams(1),
        name="mods",
    )(cs, w_ada, b_ada)


def _cast_blocks(src_refs, dst_refs):
    for s, d in zip(src_refs, dst_refs):
        d[...] = s[...].astype(BF16)


def _cast_specs(arrays, steps):
    specs = [pl.BlockSpec((a.shape[0] // steps, a.shape[1]), lambda i: (i, 0)) for a in arrays]
    shapes = [jax.ShapeDtypeStruct(a.shape, BF16) for a in arrays]
    return specs, shapes


def _prep_block(win_ref, p_ref, wall_ref):
    w = win_ref[...].astype(BF16)
    rb = w.shape[0]
    heads = jnp.concatenate([w[:, hd * DK:(hd + 1) * DK] for hd in range(2 * H)], axis=0)
    perm = _dot(heads, p_ref[...]).astype(BF16)
    for hd in range(H):
        wall_ref[:, _OQ + hd * DK:_OQ + (hd + 1) * DK] = perm[hd * rb:(hd + 1) * rb]
        wall_ref[:, _OK + hd * DK:_OK + (hd + 1) * DK] = perm[(H + hd) * rb:(H + hd + 1) * rb]
    wall_ref[:, QK:_OK] = w[:, 2 * QK:]


MODS_TN = 256


def _kT_kernel(eye_ref, w_ref, o_ref):
    o_ref[...] = lax.dot_general(eye_ref[...], w_ref[...], (((1,), (1,)), ((), ())),
                                 preferred_element_type=F32).astype(BF16)


def _kT(wall, eye):
    return pl.pallas_call(
        _kT_kernel,
        grid=(H,),
        in_specs=[_resident((DK, DK)), pl.BlockSpec((D, DK), lambda h: (0, _OK // DK + h))],
        out_specs=pl.BlockSpec((DK, D), lambda h: (h, 0)),
        out_shape=jax.ShapeDtypeStruct((QK, D), BF16),
        compiler_params=_cparams(1),
        name="kT",
    )(eye, wall)


FC = 256


def _swiglu_rows(x, sh, sc, g, nw, w13_ref, w2_ref, t_ref):
    h = _norm_mod(x, nw, sc, sh).astype(BF16)
    for j in range(DFF // FC):
        a = _dot(h, w13_ref[:, j * FC:(j + 1) * FC])
        b = _dot(h, w13_ref[:, DFF + j * FC:DFF + (j + 1) * FC])
        t_ref[:, j * FC:(j + 1) * FC] = (a * jax.nn.sigmoid(a) * b).astype(BF16)
    return x + (0.5 * g) * _dot(t_ref[...], w2_ref[...])


W13_CH = 512
W2_CH = 256


def _load_cast(src_hbm, dst_ref, stage_ref, sem_ref, axis, chunk):
    n = src_hbm.shape[axis] // chunk

    def window(c):
        return (slice(None), pl.ds(c * chunk, chunk)) if axis == 1 else (pl.ds(c * chunk, chunk), slice(None))

    def copy(c):
        return pltpu.make_async_copy(src_hbm.at[window(c)], stage_ref.at[c % 2], sem_ref.at[c % 2])

    copy(0).start()
    for c in range(n):
        if c + 1 < n:
            copy(c + 1).start()
        copy(c).wait()
        dst_ref[window(c)] = stage_ref[c % 2].astype(BF16)


def _ffn1_kernel(ncast, x_ref, ctx_ref, sh_ref, sc_ref, g_ref, nw_ref, w13_hbm, w2_hbm, *rest):
    cast_in, rest = rest[:ncast], rest[ncast:]
    (win_ref, p_ref, wada_ref, cs_ref, b_ref), rest = rest[:5], rest[5:]
    (o_ref, octx_ref), rest = rest[:2], rest[2:]
    cast_out, rest = rest[:ncast], rest[ncast:]
    wall_ref, mb_ref, t_ref, w13_ref, w2_ref, st13_ref, st2_ref, sem_ref = rest

    @pl.when(pl.program_id(0) == 0)
    def _():
        _load_cast(w13_hbm, w13_ref, st13_ref, sem_ref, 1, W13_CH)
        _load_cast(w2_hbm, w2_ref, st2_ref, sem_ref, 0, W2_CH)
        octx_ref[...] = _swiglu_rows(ctx_ref[...], sh_ref[1:2, :], sc_ref[1:2, :], g_ref[1:2, :], nw_ref[...],
                                     w13_ref, w2_ref, t_ref.at[pl.ds(0, CTX)])

    o_ref[...] = _swiglu_rows(x_ref[...], sh_ref[0:1, :], sc_ref[0:1, :], g_ref[0:1, :], nw_ref[...],
                              w13_ref, w2_ref, t_ref)
    _cast_blocks(cast_in, cast_out)
    _prep_block(win_ref, p_ref, wall_ref)
    _mods_block(wada_ref, cs_ref, b_ref, mb_ref)


def _ffn1(x, ctx, mods_a, nw, w13, w2, cast, w_in, perm, w_ada, cs, b_ada, tm=512):
    n = x.shape[0]
    steps = n // tm
    rb = D // steps
    cast_specs, cast_shapes = _cast_specs(cast, steps)
    nlate = (NMOD - NMOD_A) * D
    first, last = NMOD_A * D // MODS_TN, NMOD * D // MODS_TN - 1
    late_in = lambda i: (0, jnp.minimum(first + i, last))
    late_out = lambda i: (0, jnp.minimum(i, last - first))
    rows = lambda i: (i, 0)
    return pl.pallas_call(
        functools.partial(_ffn1_kernel, len(cast)),
        grid=(steps,),
        in_specs=[pl.BlockSpec((tm, D), rows), _resident((CTX, D)),
                  _mod_spec(0), _mod_spec(1), _mod_spec(2), _resident((1, D)),
                  pl.BlockSpec(memory_space=pl.ANY), pl.BlockSpec(memory_space=pl.ANY)]
                 + cast_specs
                 + [pl.BlockSpec((rb, _WIN), rows), _resident((DK, DK)),
                    pl.BlockSpec((D, MODS_TN), late_in), _resident((D, 8)), pl.BlockSpec((1, MODS_TN), late_in)],
        out_specs=[pl.BlockSpec((tm, D), rows), pl.BlockSpec((CTX, D), lambda i: (0, 0))]
                  + cast_specs
                  + [pl.BlockSpec((rb, _WIN), rows), pl.BlockSpec((8, MODS_TN), late_out)],
        out_shape=[jax.ShapeDtypeStruct((n, D), F32), jax.ShapeDtypeStruct((CTX, D), F32)]
                  + cast_shapes
                  + [jax.ShapeDtypeStruct((D, _WIN), BF16), jax.ShapeDtypeStruct((8, nlate), F32)],
        scratch_shapes=[pltpu.VMEM((tm, DFF), BF16), pltpu.VMEM((D, 2 * DFF), BF16), pltpu.VMEM((DFF, D), BF16),
                        pltpu.VMEM((2, D, W13_CH), F32), pltpu.VMEM((2, W2_CH, D), F32),
                        pltpu.SemaphoreType.DMA((2,))],
        compiler_params=_cparams(1),
        name="ffn1",
    )(x, ctx, mods_a, mods_a, mods_a, nw, w13, w2, *cast, w_in, perm, w_ada, cs, b_ada)


def _ffn2_kernel(x_ref, sh_ref, sc_ref, g_ref, nw_ref, w13_ref, w2_ref, nf_ref, o_ref, t_ref):
    out = _swiglu_rows(x_ref[...], sh_ref[0:1, :], sc_ref[0:1, :], g_ref[0:1, :], nw_ref[...], w13_ref, w2_ref, t_ref)
    o_ref[...] = out * lax.rsqrt(jnp.mean(out * out, axis=-1, keepdims=True) + EPS) * nf_ref[...]


def _ffn2(x, tables, nw, w13, w2, nf, tm=1024):
    n = x.shape[0]
    mod_arrays, mod_specs = zip(*[_mod(tables, 6 + k) for k in range(3)])
    return pl.pallas_call(
        _ffn2_kernel,
        grid=(n // tm,),
        in_specs=[pl.BlockSpec((tm, D), lambda i: (i, 0)), *mod_specs,
                  _resident((1, D)), _resident((D, 2 * DFF)), _resident((DFF, D)), _resident((1, D))],
        out_specs=pl.BlockSpec((tm, D), lambda i: (i, 0)),
        out_shape=jax.ShapeDtypeStruct((n, D), F32),
        scratch_shapes=[pltpu.VMEM((tm, DFF), BF16)],
        compiler_params=_cparams(1),
        name="ffn2",
    )(x, *mod_arrays, nw, w13, w2, nf)


def _ctx_state_kernel(lg_ref, x_ref, sh_ref, sc_ref, nw_ref, wkT_ref, wv_ref, sf_ref, sb_ref):
    hd = pl.program_id(0)
    h = _norm_mod(x_ref[...], nw_ref[...], sc_ref[1:2, :], sh_ref[1:2, :]).astype(BF16)
    kT = lax.dot_general(wkT_ref[...], h, (((1,), (1,)), ((), ())), preferred_element_type=F32)
    v = _dot(h, wv_ref[...]).astype(BF16)
    j = lax.broadcasted_iota(jnp.int32, (DK, CTX), 1).astype(F32)
    wf = jnp.exp(lg_ref[0, hd] * (CTX - 1.0 - j))
    wb = jnp.exp(lg_ref[1, hd] * j)
    sf_ref[0] = _dot((kT * wf).astype(BF16), v)
    sb_ref[0] = _dot((kT * wb).astype(BF16), v)


def _ctx_state(lg, ctx1, tables, nw, wkT, wall):
    return pl.pallas_call(
        _ctx_state_kernel,
        grid=(H,),
        in_specs=[
            pl.BlockSpec(memory_space=pltpu.SMEM),
            _resident((CTX, D)),
            _mod(tables, 3)[1], _mod(tables, 4)[1],
            _resident((1, D)),
            pl.BlockSpec((DK, D), lambda h: (h, 0)),
            pl.BlockSpec((D, DV), lambda h: (0, _OV // DV + h)),
        ],
        out_specs=[pl.BlockSpec((1, DK, DV), lambda h: (h, 0, 0))] * 2,
        out_shape=[jax.ShapeDtypeStruct((H, DK, DV), F32)] * 2,
        compiler_params=_cparams(1),
        name="ctx_state",
    )(lg, ctx1, tables[1], tables[1], nw, wkT, wall)


_OQ, _OV, _OG = 0, QK, QK + VW
_OF, _OGA, _OGF = QK + 2 * VW, QK + 2 * VW + D, QK + 2 * VW + 2 * D
_WIN = 2 * QK + 2 * VW + 3 * D
_OK = _WIN - QK
HALF = DK // 2

PT = 2
PROJ_VMEM_LIMIT = 58 * 1024 * 1024


def _proj_chunk(rows, ci, lg_ref, x_ref, sh_ref, sc_ref, nw_ref, w_ref, wkT_ref, md_ref, gnw_ref, cq_ref, sq_ref,
                ck_ref, sk_ref, q_ref, kT_ref, v_ref, sg_ref, G_ref, sga_ref, sgf_ref, sb_ref, S_ref, ZE_ref):
    h = _norm_mod(x_ref[rows, :], nw_ref[...], sc_ref[0:1, :], sh_ref[0:1, :]).astype(BF16)

    q = _dot(h, w_ref[:, _OQ:_OQ + QK])
    cq, sq = cq_ref[rows, :], sq_ref[rows, :]
    for hd in range(H):
        a1 = q[:, hd * DK:hd * DK + HALF]
        a2 = q[:, hd * DK + HALF:(hd + 1) * DK]
        q_ref[rows, hd * DK:hd * DK + HALF] = (a1 * cq - a2 * sq).astype(BF16)
        q_ref[rows, hd * DK + HALF:(hd + 1) * DK] = (a1 * sq + a2 * cq).astype(BF16)

    kT = lax.dot_general(wkT_ref[...], h, (((1,), (1,)), ((), ())), preferred_element_type=F32)
    ck, sk = ck_ref[:, rows], sk_ref[:, rows]
    for hd in range(H):
        a1 = kT[hd * DK:hd * DK + HALF, :]
        a2 = kT[hd * DK + HALF:(hd + 1) * DK, :]
        kT_ref[hd * DK:hd * DK + HALF, rows] = (a1 * ck - a2 * sk).astype(BF16)
        kT_ref[hd * DK + HALF:(hd + 1) * DK, rows] = (a1 * sk + a2 * ck).astype(BF16)

    v = _dot(h, w_ref[:, _OV:_OV + VW]).astype(BF16)
    v_ref[rows, :] = v
    for hd in range(H):
        S = S_ref[hd]
        sb_ref[hd, ci] = S.astype(BF16)
        kz = (kT_ref[hd * DK:(hd + 1) * DK, rows].astype(F32) * ZE_ref[hd]).astype(BF16)
        g_chunk = jnp.exp(jnp.full((1, DV), lg_ref[1, hd] * C, F32))
        S_ref[hd] = S * g_chunk + _dot(kz, v[:, hd * DV:(hd + 1) * DV])

    g = _dot(h, w_ref[:, _OG:_OG + VW])
    sg_ref[rows, :] = (g * jax.nn.sigmoid(g) * gnw_ref[...]).astype(BF16)

    f = _dot(h, w_ref[:, _OF:_OF + D]).astype(BF16)
    for gi in range(NG):
        z = _dot(f[:, gi * DG:(gi + 1) * DG], md_ref[...])
        G_ref[0, rows, gi * DG:(gi + 1) * DG] = z[:, :DG].astype(BF16)
        G_ref[1, rows, gi * DG:(gi + 1) * DG] = z[:, DG:].astype(BF16)

    sga_ref[rows, :] = jax.nn.sigmoid(_dot(h, w_ref[:, _OGA:_OGA + D])).astype(BF16)
    sgf_ref[rows, :] = jax.nn.sigmoid(_dot(h, w_ref[:, _OGF:_OGF + D])).astype(BF16)


def _proj_kernel(lg_ref, x_ref, sh_ref, sc_ref, nw_ref, w_ref, wkT_ref, md_ref, gnw_ref, cq_ref, sq_ref, ck_ref,
                 sk_ref, s0_ref, pr_ref, pf_ref, wo_ref, q_ref, kT_ref, v_ref, sg_ref, G_ref, sga_ref, sgf_ref, sb_ref,
                 prb_ref, pfb_ref, wob_ref, S_ref, ZE_ref):
    _cast_blocks((pr_ref, pf_ref, wo_ref), (prb_ref, pfb_ref, wob_ref))

    @pl.when(pl.program_id(0) == 0)
    def _():
        S_ref[...] = s0_ref[...]
        c = lax.broadcasted_iota(jnp.int32, (DK, C), 1).astype(F32)
        for hd in range(H):
            ZE_ref[hd] = jnp.exp(lg_ref[1, hd] * c)

    for ci in range(PT - 1, -1, -1):
        _proj_chunk(slice(ci * C, (ci + 1) * C), ci, lg_ref, x_ref, sh_ref, sc_ref, nw_ref, w_ref, wkT_ref, md_ref,
                    gnw_ref, cq_ref, sq_ref, ck_ref, sk_ref, q_ref, kT_ref, v_ref, sg_ref, G_ref, sga_ref, sgf_ref,
                    sb_ref, S_ref, ZE_ref)


def _proj(lg, x1, tables, nw, wall, wkT, md, gnw, cq, sq, ck, sk, s_b0, cast):
    tm = PT * C
    n = L // tm
    cast_specs, cast_shapes = _cast_specs(cast, n)
    row = lambda i: (n - 1 - i, 0)
    col = lambda i: (0, n - 1 - i)
    return pl.pallas_call(
        _proj_kernel,
        grid=(n,),
        in_specs=[
            pl.BlockSpec(memory_space=pltpu.SMEM),
            pl.BlockSpec((tm, D), row),
            _mod(tables, 3)[1], _mod(tables, 4)[1],
            _resident((1, D)),
            _resident((D, _OK)),
            _resident((QK, D)),
            _resident((DG, 2 * DG)),
            _resident((1, VW)),
            pl.BlockSpec((tm, HALF), row), pl.BlockSpec((tm, HALF), row),
            pl.BlockSpec((HALF, tm), col), pl.BlockSpec((HALF, tm), col),
            _resident((H, DK, DV)),
        ] + cast_specs,
        out_specs=[
            pl.BlockSpec((tm, QK), row),
            pl.BlockSpec((QK, tm), col),
            pl.BlockSpec((tm, VW), row),
            pl.BlockSpec((tm, VW), row),
            pl.BlockSpec((2, tm, D), lambda i: (0, n - 1 - i, 0)),
            pl.BlockSpec((tm, D), row),
            pl.BlockSpec((tm, D), row),
            pl.BlockSpec((H, PT, DK, DV), lambda i: (0, n - 1 - i, 0, 0)),
        ] + cast_specs,
        out_shape=[
            jax.ShapeDtypeStruct((L, QK), BF16),
            jax.ShapeDtypeStruct((QK, L), BF16),
            jax.ShapeDtypeStruct((L, VW), BF16),
            jax.ShapeDtypeStruct((L, VW), BF16),
            jax.ShapeDtypeStruct((2, L, D), BF16),
            jax.ShapeDtypeStruct((L, D), BF16),
            jax.ShapeDtypeStruct((L, D), BF16),
            jax.ShapeDtypeStruct((H, L // C, DK, DV), BF16),
        ] + cast_shapes,
        scratch_shapes=[pltpu.VMEM((H, DK, DV), F32), pltpu.VMEM((H, DK, C), F32)],
        compiler_params=_cparams(1, PROJ_VMEM_LIMIT),
        name="proj",
    )(lg, x1, tables[1], tables[1], nw, wall, wkT, md, gnw, cq, sq, ck, sk, s_b0, *cast)


FFT_NT = 16
FFT_KB = 16


def _fft1_kernel(m_ref, g_ref, o_ref, r_ref):
    z = jnp.transpose(g_ref[...], (2, 0, 1, 3)).reshape(FFT_NT, 2 * N1, D)
    for t in range(FFT_NT):
        r_ref[t] = _dot(m_ref[...], z[t]).astype(BF16).reshape(2, N1, D)
    o_ref[...] = jnp.transpose(r_ref[...], (2, 1, 0, 3))


def _fft1(m1, g4):
    return pl.pallas_call(
        _fft1_kernel,
        grid=(N2 // FFT_NT,),
        in_specs=[_resident((2 * N1, 2 * N1)),
                  pl.BlockSpec((2, N1, FFT_NT, D), lambda j: (0, 0, j, 0))],
        out_specs=pl.BlockSpec((N1, 2, FFT_NT, D), lambda j: (0, 0, j, 0)),
        out_shape=jax.ShapeDtypeStruct((N1, 2, N2, D), BF16),
        scratch_shapes=[pltpu.VMEM((FFT_NT, 2, N1, D), BF16)],
        compiler_params=_cparams(1),
        name="fft1",
    )(m1, g4)


def _fft3_kernel(t_ref, pf_ref, a_ref, o_ref, xr_ref):
    for i in range(FFT_KB):
        xr_ref[i * N2:(i + 1) * N2, :] = _dot(t_ref[i], a_ref[i].reshape(2 * N2, D)).astype(BF16)
    fo = _dot(xr_ref[...], pf_ref[...]).astype(BF16).reshape(FFT_KB, N2, D)
    o_ref[...] = jnp.transpose(fo, (1, 0, 2))


def _fft3(t3, a4, pf):
    return pl.pallas_call(
        _fft3_kernel,
        grid=(N1 // FFT_KB,),
        in_specs=[pl.BlockSpec((FFT_KB, N2, 2 * N2), lambda j: (j, 0, 0)),
                  _resident((D, D)),
                  pl.BlockSpec((FFT_KB, 2, N2, D), lambda j: (j, 0, 0, 0))],
        out_specs=pl.BlockSpec((N2, FFT_KB, D), lambda j: (0, j, 0)),
        out_shape=jax.ShapeDtypeStruct((N2, N1, D), BF16),
        scratch_shapes=[pltpu.VMEM((FFT_KB * N2, D), BF16)],
        compiler_params=_cparams(1),
        name="fft3",
    )(t3, pf, a4)


TB_R = 512


def _retmix_kernel(lg_ref, q_ref, kT_ref, v_ref, s0_ref, sb_ref, sgw_ref, x_ref, g_ref, fo_ref, sga_ref, sgf_ref,
                   pr_ref, wo_ref, o_ref, S_ref, D_ref, XF_ref, XB_ref, ZE_ref, rin_ref):
    @pl.when(pl.program_id(0) == 0)
    def _():
        S_ref[...] = s0_ref[...]
        r = lax.broadcasted_iota(jnp.int32, (C, C), 0).astype(F32)
        c = lax.broadcasted_iota(jnp.int32, (C, C), 1).astype(F32)
        d = r - c
        for hd in range(H):
            lgf, lgb = lg_ref[0, hd], lg_ref[1, hd]
            D_ref[hd] = (jnp.where(d >= 0, jnp.exp(lgf * jnp.maximum(d, 0.0)), 0.0)
                         + jnp.where(d <= 0, jnp.exp(lgb * jnp.maximum(-d, 0.0)), 0.0))
            XF_ref[hd] = jnp.exp(lgf * (r + 1.0))
            XB_ref[hd] = jnp.exp(lgb * (C - r))
            ZE_ref[hd] = jnp.exp(lgf * (C - 1.0 - c))

    for ci in range(TB_R // C):
        for hd in range(H):
            g_chunk = jnp.exp(jnp.full((1, DV), lg_ref[0, hd] * C, F32))
            rows = slice(ci * C, (ci + 1) * C)
            qc = q_ref[rows, hd * DK:(hd + 1) * DK]
            kTc = kT_ref[hd * DK:(hd + 1) * DK, rows]
            vc = v_ref[rows, hd * DV:(hd + 1) * DV]
            S = S_ref[hd]
            p = (_dot(qc, kTc) * D_ref[hd]).astype(BF16)
            qf = (qc.astype(F32) * XF_ref[hd]).astype(BF16)
            qb = (qc.astype(F32) * XB_ref[hd]).astype(BF16)
            o = _dot(p, vc) + _dot(qf, S.astype(BF16)) + _dot(qb, sb_ref[hd, ci])
            kz = (kTc.astype(F32) * ZE_ref[hd]).astype(BF16)
            S_ref[hd] = S * g_chunk + _dot(kz, vc)
            mu = jnp.mean(o, axis=-1, keepdims=True)
            d = o - mu
            var = jnp.mean(d * d, axis=-1, keepdims=True)
            on = d * lax.rsqrt(var + EPS)
            rin_ref[rows, hd * DV:(hd + 1) * DV] = (sgw_ref[rows, hd * DV:(hd + 1) * DV].astype(F32) * on).astype(BF16)

    r = _dot(rin_ref[...], pr_ref[...])
    m = sga_ref[...].astype(F32) * r + sgf_ref[...].astype(F32) * fo_ref[...].astype(F32)
    y = _dot(m.astype(BF16), wo_ref[...])
    o_ref[...] = x_ref[...] + g_ref[0:1, :] * y


def _retmix(lg, q, kT, v, s0, sb, sgw, x1, tables, fo, sga, sgf, p_ret, w_out):
    tb = TB_R
    row = lambda i: (i, 0)
    return pl.pallas_call(
        _retmix_kernel,
        grid=(L // tb,),
        in_specs=[
            pl.BlockSpec(memory_space=pltpu.SMEM),
            pl.BlockSpec((tb, QK), row),
            pl.BlockSpec((QK, tb), lambda i: (0, i)),
            pl.BlockSpec((tb, VW), row),
            _resident((H, DK, DV)),
            pl.BlockSpec((H, tb // C, DK, DV), lambda i: (0, i, 0, 0)),
            pl.BlockSpec((tb, VW), row),
            pl.BlockSpec((tb, D), row),
            _mod(tables, 5)[1],
            pl.BlockSpec((tb, D), row), pl.BlockSpec((tb, D), row), pl.BlockSpec((tb, D), row),
            _resident((VW, D)), _resident((D, D)),
        ],
        out_specs=pl.BlockSpec((tb, D), row),
        out_shape=jax.ShapeDtypeStruct((L, D), F32),
        scratch_shapes=[pltpu.VMEM((H, DK, DV), F32)] + [pltpu.VMEM((H, C, C), F32)] * 4
                       + [pltpu.VMEM((tb, VW), BF16)],
        compiler_params=_cparams(1),
        name="retmix",
    )(lg, q, kT, v, s0, sb, sgw, x1, tables[1], fo, sga, sgf, p_ret, w_out)


def _rope_tables():
    rows = L // GRID_W
    row = np.repeat(np.arange(rows, dtype=np.float64), GRID_W)
    col = np.tile(np.arange(GRID_W, dtype=np.float64), rows)
    n_freq = DK // 4
    inv = ROPE_BASE ** (-np.arange(n_freq, dtype=np.float64) / n_freq)
    ang = np.concatenate([row[:, None] * inv, col[:, None] * inv], axis=-1)
    return np.cos(ang), np.sin(ang)


def _dft_tables():
    two_pi = 2.0 * np.pi
    th = (np.outer(np.arange(DG), np.arange(DG)) % DG) * (two_pi / DG)
    md = np.concatenate([np.cos(th), -np.sin(th)], axis=1)
    th = (np.outer(np.arange(N1), np.arange(N1)) % N1) * (two_pi / N1)
    c1, s1 = np.cos(th), np.sin(th)
    m1 = np.block([[c1, s1], [-s1, c1]])
    k = np.arange(N1)[:, None, None] + N1 * np.arange(N2)[None, :, None]
    th = ((k * np.arange(N2)[None, None, :]) % L) * (two_pi / L)
    t3 = np.concatenate([np.cos(th), np.sin(th)], axis=-1) / np.sqrt(float(L) * DG)
    f32 = lambda a: jnp.asarray(a.astype(np.float32))
    return f32(md).astype(BF16), f32(m1).astype(BF16), f32(t3).astype(BF16)


def kernel(x, c, ctx, c_ctx, w_ada, b_ada, norm_ffn1, w13_ffn1, w2_ffn1, norm_mix, w_in, decay_fwd, decay_bwd,
           ret_gn_w, p_ret, p_four, w_out, norm_ffn2, w13_ffn2, w2_ffn2, norm_final):
    assert x.shape == (1, L, D) and ctx.shape == (1, CTX, D) and w_ada.shape[0] == 1

    pm = np.zeros((DK, DK), np.float32)
    pm[np.concatenate([np.arange(0, DK, 2), np.arange(1, DK, 2)]), np.arange(DK)] = 1.0
    perm = jnp.asarray(pm).astype(BF16)
    eye = jnp.asarray(np.eye(DK, dtype=np.float32)).astype(BF16)

    lg = jnp.stack([jax.nn.log_sigmoid(decay_fwd[0].astype(F32)),
                    jax.nn.log_sigmoid(decay_bwd[0].astype(F32))])
    cos, sin = _rope_tables()
    qscale = DK ** -0.5
    f32c = lambda a: jnp.asarray(np.ascontiguousarray(a).astype(np.float32))
    cq, sq, ck, sk = f32c(cos * qscale), f32c(sin * qscale), f32c(cos.T), f32c(sin.T)
    md, m1, t3 = _dft_tables()

    cs = jnp.concatenate([c, c_ctx[None, :], jnp.zeros((6, D), F32)], axis=0).T
    mods_a = _mods(cs, w_ada[0], b_ada)

    x1, ctx1, w13b, w2b, wall, mods_b = _ffn1(x[0], ctx[0], mods_a, norm_ffn1, w13_ffn1[0], w2_ffn1[0],
                                              (w13_ffn2[0], w2_ffn2[0]), w_in[0], perm, w_ada[0], cs, b_ada)
    mods = (mods_a, mods_b)
    wkT = _kT(wall, eye)

    s_f, s_b = _ctx_state(lg, ctx1, mods, norm_mix, wkT, wall)

    q, kT, v, sgw, G, sga, sgf, sb, pr, pf, wo = _proj(lg, x1, mods, norm_mix, wall, wkT, md, ret_gn_w, cq, sq, ck, sk,
                                                     s_b, (p_ret[0], p_four[0], w_out[0]))
    a = _fft1(m1, G.reshape(2, N1, N2, D))
    fo = _fft3(t3, a, pf).reshape(L, D)

    x2 = _retmix(lg, q, kT, v, s_f, sb, sgw, x1, mods, fo, sga, sgf, pr, wo)
    out = _ffn2(x2, mods, norm_ffn2, w13b, w2b, norm_final[None, :])
    return out[None]
```

```python
import functools

import numpy as np
import jax
import jax.numpy as jnp
from jax import lax
from jax.experimental import pallas as pl
from jax.experimental.pallas import tpu as pltpu

D = 1024
L = 16384
GRID_W = 64
CTX = 256
H = 4
DK = 256
DV = 512
QK = H * DK
VW = H * DV
NG = 4
DG = 256
DFF = 2816
ROPE_BASE = 10000.0
EPS = 1e-6
NMOD = 9

F32 = jnp.float32
BF16 = jnp.bfloat16

C = 256
N1 = 128
N2 = 128

V7X_VMEM_BYTES = 64 * 1024 * 1024
VMEM_LIMIT = V7X_VMEM_BYTES * 7 // 8


def _dot(a, b):
    return jnp.dot(a, b, preferred_element_type=F32)


def _cparams(n_axes, vmem_limit=VMEM_LIMIT):
    return pltpu.CompilerParams(
        dimension_semantics=("arbitrary",) * n_axes, vmem_limit_bytes=vmem_limit)


def _resident(shape):
    nd = len(shape)
    return pl.BlockSpec(shape, lambda *_: (0,) * nd, pipeline_mode=pl.Buffered(1))


def _mod_spec(j):
    return pl.BlockSpec((8, D), lambda *_: (0, j), pipeline_mode=pl.Buffered(1))


NMOD_A = 3


def _mod(tables, j):
    return (tables[0], _mod_spec(j)) if j < NMOD_A else (tables[1], _mod_spec(j - NMOD_A))


def _norm_mod(x, nw, sc, sh):
    y = x * lax.rsqrt(jnp.mean(x * x, axis=-1, keepdims=True) + EPS)
    return y * (nw * (1.0 + sc)) + sh


def _mods_block(w_ref, cs_ref, b_ref, o_ref):
    cs = cs_ref[...]
    s = cs * jax.nn.sigmoid(cs)
    w = w_ref[...]
    rows = [jnp.sum(w * s[:, r:r + 1], axis=0, keepdims=True) for r in range(2)]
    o_ref[...] = jnp.concatenate(rows + [jnp.zeros((6, w.shape[1]), F32)], axis=0) + b_ref[...]


def _mods_kernel(cs_ref, w_ref, b_ref, o_ref):
    _mods_block(w_ref, cs_ref, b_ref, o_ref)


def _mods(cs, w_ada, b_ada):
    tn = NMOD_A * D // 4
    return pl.pallas_call(
        _mods_kernel,
        grid=(4,),
        in_specs=[
            _resident((D, 8)),
            pl.BlockSpec((D, tn), lambda j: (0, j)),
            pl.BlockSpec((1, tn), lambda j: (0, j)),
        ],
        out_specs=pl.BlockSpec((8, tn), lambda j: (0, j)),
        out_shape=jax.ShapeDtypeStruct((8, NMOD_A * D), F32),
        compiler_params=_cparams(1),
        name="mods",
    )(cs, w_ada, b_ada)


def _cast_blocks(src_refs, dst_refs):
    for s, d in zip(src_refs, dst_refs):
        d[...] = s[...].astype(BF16)


def _cast_specs(arrays, steps):
    specs = [pl.BlockSpec((a.shape[0] // steps, a.shape[1]), lambda i: (i, 0)) for a in arrays]
    shapes = [jax.ShapeDtypeStruct(a.shape, BF16) for a in arrays]
    return specs, shapes


def _prep_block(win_ref, p_ref, wall_ref):
    w = win_ref[...].astype(BF16)
    rb = w.shape[0]
    heads = jnp.concatenate([w[:, hd * DK:(hd + 1) * DK] for hd in range(2 * H)], axis=0)
    perm = _dot(heads, p_ref[...]).astype(BF16)
    for hd in range(H):
        wall_ref[:, _OQ + hd * DK:_OQ + (hd + 1) * DK] = perm[hd * rb:(hd + 1) * rb]
        wall_ref[:, _OK + hd * DK:_OK + (hd + 1) * DK] = perm[(H + hd) * rb:(H + hd + 1) * rb]
    wall_ref[:, QK:_OK] = w[:, 2 * QK:]


MODS_TN = 256


FC = 256


def _swiglu_rows(x, sh, sc, g, nw, w13_ref, w2_ref, t_ref):
    h = _norm_mod(x, nw, sc, sh).astype(BF16)
    for j in range(DFF // FC):
        a = _dot(h, w13_ref[:, j * FC:(j + 1) * FC])
        b = _dot(h, w13_ref[:, DFF + j * FC:DFF + (j + 1) * FC])
        t_ref[:, j * FC:(j + 1) * FC] = (a * jax.nn.sigmoid(a) * b).astype(BF16)
    return x + (0.5 * g) * _dot(t_ref[...], w2_ref[...])


W13_CH = 512
W2_CH = 256


def _load_cast(src_hbm, dst_ref, stage_ref, sem_ref, axis, chunk):
    n = src_hbm.shape[axis] // chunk

    def window(c):
        return (slice(None), pl.ds(c * chunk, chunk)) if axis == 1 else (pl.ds(c * chunk, chunk), slice(None))

    def copy(c):
        return pltpu.make_async_copy(src_hbm.at[window(c)], stage_ref.at[c % 2], sem_ref.at[c % 2])

    copy(0).start()
    for c in range(n):
        if c + 1 < n:
            copy(c + 1).start()
        copy(c).wait()
        dst_ref[window(c)] = stage_ref[c % 2].astype(BF16)


def _ffn1_kernel(ncast, x_ref, ctx_ref, sh_ref, sc_ref, g_ref, nw_ref, w13_hbm, w2_hbm, *rest):
    cast_in, rest = rest[:ncast], rest[ncast:]
    (win_ref, p_ref, wada_ref, cs_ref, b_ref), rest = rest[:5], rest[5:]
    (o_ref, octx_ref), rest = rest[:2], rest[2:]
    cast_out, rest = rest[:ncast], rest[ncast:]
    wall_ref, mb_ref, t_ref, w13_ref, w2_ref, st13_ref, st2_ref, sem_ref = rest

    @pl.when(pl.program_id(0) == 0)
    def _():
        _load_cast(w13_hbm, w13_ref, st13_ref, sem_ref, 1, W13_CH)
        _load_cast(w2_hbm, w2_ref, st2_ref, sem_ref, 0, W2_CH)
        octx_ref[...] = _swiglu_rows(ctx_ref[...], sh_ref[1:2, :], sc_ref[1:2, :], g_ref[1:2, :], nw_ref[...],
                                     w13_ref, w2_ref, t_ref.at[pl.ds(0, CTX)])

    o_ref[...] = _swiglu_rows(x_ref[...], sh_ref[0:1, :], sc_ref[0:1, :], g_ref[0:1, :], nw_ref[...],
                              w13_ref, w2_ref, t_ref)
    _cast_blocks(cast_in, cast_out)
    _prep_block(win_ref, p_ref, wall_ref)
    _mods_block(wada_ref, cs_ref, b_ref, mb_ref)


def _ffn1(x, ctx, mods_a, nw, w13, w2, cast, w_in, perm, w_ada, cs, b_ada, tm=512):
    n = x.shape[0]
    steps = n // tm
    rb = D // steps
    cast_specs, cast_shapes = _cast_specs(cast, steps)
    nlate = (NMOD - NMOD_A) * D
    first, last = NMOD_A * D // MODS_TN, NMOD * D // MODS_TN - 1
    late_in = lambda i: (0, jnp.minimum(first + i, last))
    late_out = lambda i: (0, jnp.minimum(i, last - first))
    rows = lambda i: (i, 0)
    return pl.pallas_call(
        functools.partial(_ffn1_kernel, len(cast)),
        grid=(steps,),
        in_specs=[pl.BlockSpec((tm, D), rows), _resident((CTX, D)),
                  _mod_spec(0), _mod_spec(1), _mod_spec(2), _resident((1, D)),
                  pl.BlockSpec(memory_space=pl.ANY), pl.BlockSpec(memory_space=pl.ANY)]
                 + cast_specs
                 + [pl.BlockSpec((rb, _WIN), rows), _resident((DK, DK)),
                    pl.BlockSpec((D, MODS_TN), late_in), _resident((D, 8)), pl.BlockSpec((1, MODS_TN), late_in)],
        out_specs=[pl.BlockSpec((tm, D), rows), pl.BlockSpec((CTX, D), lambda i: (0, 0))]
                  + cast_specs
                  + [pl.BlockSpec((rb, _WIN), rows), pl.BlockSpec((8, MODS_TN), late_out)],
        out_shape=[jax.ShapeDtypeStruct((n, D), F32), jax.ShapeDtypeStruct((CTX, D), F32)]
                  + cast_shapes
                  + [jax.ShapeDtypeStruct((D, _WIN), BF16), jax.ShapeDtypeStruct((8, nlate), F32)],
        scratch_shapes=[pltpu.VMEM((tm, DFF), BF16), pltpu.VMEM((D, 2 * DFF), BF16), pltpu.VMEM((DFF, D), BF16),
                        pltpu.VMEM((2, D, W13_CH), F32), pltpu.VMEM((2, W2_CH, D), F32),
                        pltpu.SemaphoreType.DMA((2,))],
        compiler_params=_cparams(1),
        name="ffn1",
    )(x, ctx, mods_a, mods_a, mods_a, nw, w13, w2, *cast, w_in, perm, w_ada, cs, b_ada)


def _ffn2_kernel(x_ref, sh_ref, sc_ref, g_ref, nw_ref, w13_ref, w2_ref, nf_ref, o_ref, t_ref):
    out = _swiglu_rows(x_ref[...], sh_ref[0:1, :], sc_ref[0:1, :], g_ref[0:1, :], nw_ref[...], w13_ref, w2_ref, t_ref)
    o_ref[...] = out * lax.rsqrt(jnp.mean(out * out, axis=-1, keepdims=True) + EPS) * nf_ref[...]


def _ffn2(x, tables, nw, w13, w2, nf, tm=1024):
    n = x.shape[0]
    mod_arrays, mod_specs = zip(*[_mod(tables, 6 + k) for k in range(3)])
    return pl.pallas_call(
        _ffn2_kernel,
        grid=(n // tm,),
        in_specs=[pl.BlockSpec((tm, D), lambda i: (i, 0)), *mod_specs,
                  _resident((1, D)), _resident((D, 2 * DFF)), _resident((DFF, D)), _resident((1, D))],
        out_specs=pl.BlockSpec((tm, D), lambda i: (i, 0)),
        out_shape=jax.ShapeDtypeStruct((n, D), F32),
        scratch_shapes=[pltpu.VMEM((tm, DFF), BF16)],
        compiler_params=_cparams(1),
        name="ffn2",
    )(x, *mod_arrays, nw, w13, w2, nf)


def _ctx_state_kernel(lg_ref, x_ref, sh_ref, sc_ref, nw_ref, eye_ref, wk_ref, wv_ref, sf_ref, sb_ref, wkT_ref):
    hd = pl.program_id(0)
    wkT = lax.dot_general(eye_ref[...], wk_ref[...], (((1,), (1,)), ((), ())),
                          preferred_element_type=F32).astype(BF16)
    wkT_ref[...] = wkT
    h = _norm_mod(x_ref[...], nw_ref[...], sc_ref[1:2, :], sh_ref[1:2, :]).astype(BF16)
    kT = lax.dot_general(wkT, h, (((1,), (1,)), ((), ())), preferred_element_type=F32)
    v = _dot(h, wv_ref[...]).astype(BF16)
    j = lax.broadcasted_iota(jnp.int32, (DK, CTX), 1).astype(F32)
    wf = jnp.exp(lg_ref[0, hd] * (CTX - 1.0 - j))
    wb = jnp.exp(lg_ref[1, hd] * j)
    sf_ref[0] = _dot((kT * wf).astype(BF16), v)
    sb_ref[0] = _dot((kT * wb).astype(BF16), v)


def _ctx_state(lg, ctx1, tables, nw, eye, wall):
    return pl.pallas_call(
        _ctx_state_kernel,
        grid=(H,),
        in_specs=[
            pl.BlockSpec(memory_space=pltpu.SMEM),
            _resident((CTX, D)),
            _mod(tables, 3)[1], _mod(tables, 4)[1],
            _resident((1, D)),
            _resident((DK, DK)),
            pl.BlockSpec((D, DK), lambda h: (0, _OK // DK + h)),
            pl.BlockSpec((D, DV), lambda h: (0, _OV // DV + h)),
        ],
        out_specs=[pl.BlockSpec((1, DK, DV), lambda h: (h, 0, 0))] * 2 + [pl.BlockSpec((DK, D), lambda h: (h, 0))],
        out_shape=[jax.ShapeDtypeStruct((H, DK, DV), F32)] * 2 + [jax.ShapeDtypeStruct((QK, D), BF16)],
        compiler_params=_cparams(1),
        name="ctx_state",
    )(lg, ctx1, tables[1], tables[1], nw, eye, wall, wall)


_OQ, _OV, _OG = 0, QK, QK + VW
_OF, _OGA, _OGF = QK + 2 * VW, QK + 2 * VW + D, QK + 2 * VW + 2 * D
_WIN = 2 * QK + 2 * VW + 3 * D
_OK = _WIN - QK
HALF = DK // 2

PT = 2
PROJ_VMEM_LIMIT = V7X_VMEM_BYTES * 29 // 32


def _proj_chunk(rows, ci, lg_ref, x_ref, sh_ref, sc_ref, nw_ref, w_ref, wkT_ref, md_ref, gnw_ref, cq_ref, sq_ref,
                ck_ref, sk_ref, q_ref, kT_ref, v_ref, sg_ref, G_ref, sga_ref, sgf_ref, sb_ref, S_ref, ZE_ref):
    h = _norm_mod(x_ref[rows, :], nw_ref[...], sc_ref[0:1, :], sh_ref[0:1, :]).astype(BF16)

    q = _dot(h, w_ref[:, _OQ:_OQ + QK])
    cq, sq = cq_ref[rows, :], sq_ref[rows, :]
    for hd in range(H):
        a1 = q[:, hd * DK:hd * DK + HALF]
        a2 = q[:, hd * DK + HALF:(hd + 1) * DK]
        q_ref[rows, hd * DK:hd * DK + HALF] = (a1 * cq - a2 * sq).astype(BF16)
        q_ref[rows, hd * DK + HALF:(hd + 1) * DK] = (a1 * sq + a2 * cq).astype(BF16)

    kT = lax.dot_general(wkT_ref[...], h, (((1,), (1,)), ((), ())), preferred_element_type=F32)
    ck, sk = ck_ref[:, rows], sk_ref[:, rows]
    for hd in range(H):
        a1 = kT[hd * DK:hd * DK + HALF, :]
        a2 = kT[hd * DK + HALF:(hd + 1) * DK, :]
        kT_ref[hd * DK:hd * DK + HALF, rows] = (a1 * ck - a2 * sk).astype(BF16)
        kT_ref[hd * DK + HALF:(hd + 1) * DK, rows] = (a1 * sk + a2 * ck).astype(BF16)

    v = _dot(h, w_ref[:, _OV:_OV + VW]).astype(BF16)
    v_ref[rows, :] = v
    for hd in range(H):
        S = S_ref[hd]
        sb_ref[hd, ci] = S.astype(BF16)
        kz = (kT_ref[hd * DK:(hd + 1) * DK, rows].astype(F32) * ZE_ref[hd]).astype(BF16)
        g_chunk = jnp.exp(jnp.full((1, DV), lg_ref[1, hd] * C, F32))
        S_ref[hd] = S * g_chunk + _dot(kz, v[:, hd * DV:(hd + 1) * DV])

    g = _dot(h, w_ref[:, _OG:_OG + VW])
    sg_ref[rows, :] = (g * jax.nn.sigmoid(g) * gnw_ref[...]).astype(BF16)

    f = _dot(h, w_ref[:, _OF:_OF + D]).astype(BF16)
    for gi in range(NG):
        z = _dot(f[:, gi * DG:(gi + 1) * DG], md_ref[...])
        G_ref[0, rows, gi * DG:(gi + 1) * DG] = z[:, :DG].astype(BF16)
        G_ref[1, rows, gi * DG:(gi + 1) * DG] = z[:, DG:].astype(BF16)

    sga_ref[rows, :] = jax.nn.sigmoid(_dot(h, w_ref[:, _OGA:_OGA + D])).astype(BF16)
    sgf_ref[rows, :] = jax.nn.sigmoid(_dot(h, w_ref[:, _OGF:_OGF + D])).astype(BF16)


def _proj_kernel(lg_ref, x_ref, sh_ref, sc_ref, nw_ref, w_ref, wkT_ref, md_ref, gnw_ref, cq_ref, sq_ref, ck_ref,
                 sk_ref, s0_ref, pr_ref, pf_ref, wo_ref, q_ref, kT_ref, v_ref, sg_ref, G_ref, sga_ref, sgf_ref, sb_ref,
                 prb_ref, pfb_ref, wob_ref, S_ref, ZE_ref):
    _cast_blocks((pr_ref, pf_ref, wo_ref), (prb_ref, pfb_ref, wob_ref))

    @pl.when(pl.program_id(0) == 0)
    def _():
        S_ref[...] = s0_ref[...]
        c = lax.broadcasted_iota(jnp.int32, (DK, C), 1).astype(F32)
        for hd in range(H):
            ZE_ref[hd] = jnp.exp(lg_ref[1, hd] * c)

    for ci in range(PT - 1, -1, -1):
        _proj_chunk(slice(ci * C, (ci + 1) * C), ci, lg_ref, x_ref, sh_ref, sc_ref, nw_ref, w_ref, wkT_ref, md_ref,
                    gnw_ref, cq_ref, sq_ref, ck_ref, sk_ref, q_ref, kT_ref, v_ref, sg_ref, G_ref, sga_ref, sgf_ref,
                    sb_ref, S_ref, ZE_ref)


def _proj(lg, x1, tables, nw, wall, wkT, md, gnw, cq, sq, ck, sk, s_b0, cast):
    tm = PT * C
    n = L // tm
    cast_specs, cast_shapes = _cast_specs(cast, n)
    row = lambda i: (n - 1 - i, 0)
    col = lambda i: (0, n - 1 - i)
    return pl.pallas_call(
        _proj_kernel,
        grid=(n,),
        in_specs=[
            pl.BlockSpec(memory_space=pltpu.SMEM),
            pl.BlockSpec((tm, D), row),
            _mod(tables, 3)[1], _mod(tables, 4)[1],
            _resident((1, D)),
            _resident((D, _OK)),
            _resident((QK, D)),
            _resident((DG, 2 * DG)),
            _resident((1, VW)),
            pl.BlockSpec((tm, HALF), row), pl.BlockSpec((tm, HALF), row),
            pl.BlockSpec((HALF, tm), col), pl.BlockSpec((HALF, tm), col),
            _resident((H, DK, DV)),
        ] + cast_specs,
        out_specs=[
            pl.BlockSpec((tm, QK), row),
            pl.BlockSpec((QK, tm), col),
            pl.BlockSpec((tm, VW), row),
            pl.BlockSpec((tm, VW), row),
            pl.BlockSpec((2, tm, D), lambda i: (0, n - 1 - i, 0)),
            pl.BlockSpec((tm, D), row),
            pl.BlockSpec((tm, D), row),
            pl.BlockSpec((H, PT, DK, DV), lambda i: (0, n - 1 - i, 0, 0)),
        ] + cast_specs,
        out_shape=[
            jax.ShapeDtypeStruct((L, QK), BF16),
            jax.ShapeDtypeStruct((QK, L), BF16),
            jax.ShapeDtypeStruct((L, VW), BF16),
            jax.ShapeDtypeStruct((L, VW), BF16),
            jax.ShapeDtypeStruct((2, L, D), BF16),
            jax.ShapeDtypeStruct((L, D), BF16),
            jax.ShapeDtypeStruct((L, D), BF16),
            jax.ShapeDtypeStruct((H, L // C, DK, DV), BF16),
        ] + cast_shapes,
        scratch_shapes=[pltpu.VMEM((H, DK, DV), F32), pltpu.VMEM((H, DK, C), F32)],
        compiler_params=_cparams(1, PROJ_VMEM_LIMIT),
        name="proj",
    )(lg, x1, tables[1], tables[1], nw, wall, wkT, md, gnw, cq, sq, ck, sk, s_b0, *cast)


FFT_NT = 16
FFT_KB = 16


def _fft1_kernel(m_ref, g_ref, o_ref, r_ref):
    z = jnp.transpose(g_ref[...], (2, 0, 1, 3)).reshape(FFT_NT, 2 * N1, D)
    for t in range(FFT_NT):
        r_ref[t] = _dot(m_ref[...], z[t]).astype(BF16).reshape(2, N1, D)
    o_ref[...] = jnp.transpose(r_ref[...], (2, 1, 0, 3))


def _fft1(m1, g4):
    return pl.pallas_call(
        _fft1_kernel,
        grid=(N2 // FFT_NT,),
        in_specs=[_resident((2 * N1, 2 * N1)),
                  pl.BlockSpec((2, N1, FFT_NT, D), lambda j: (0, 0, j, 0))],
        out_specs=pl.BlockSpec((N1, 2, FFT_NT, D), lambda j: (0, 0, j, 0)),
        out_shape=jax.ShapeDtypeStruct((N1, 2, N2, D), BF16),
        scratch_shapes=[pltpu.VMEM((FFT_NT, 2, N1, D), BF16)],
        compiler_params=_cparams(1),
        name="fft1",
    )(m1, g4)


def _fft3_kernel(t_ref, pf_ref, a_ref, o_ref, xr_ref):
    for i in range(FFT_KB):
        xr_ref[i * N2:(i + 1) * N2, :] = _dot(t_ref[i], a_ref[i].reshape(2 * N2, D)).astype(BF16)
    fo = _dot(xr_ref[...], pf_ref[...]).astype(BF16).reshape(FFT_KB, N2, D)
    o_ref[...] = jnp.transpose(fo, (1, 0, 2))


def _fft3(t3, a4, pf):
    return pl.pallas_call(
        _fft3_kernel,
        grid=(N1 // FFT_KB,),
        in_specs=[pl.BlockSpec((FFT_KB, N2, 2 * N2), lambda j: (j, 0, 0)),
                  _resident((D, D)),
                  pl.BlockSpec((FFT_KB, 2, N2, D), lambda j: (j, 0, 0, 0))],
        out_specs=pl.BlockSpec((N2, FFT_KB, D), lambda j: (0, j, 0)),
        out_shape=jax.ShapeDtypeStruct((N2, N1, D), BF16),
        scratch_shapes=[pltpu.VMEM((FFT_KB * N2, D), BF16)],
        compiler_params=_cparams(1),
        name="fft3",
    )(t3, pf, a4)


TB_R = 512


def _retmix_kernel(lg_ref, q_ref, kT_ref, v_ref, s0_ref, sb_ref, sgw_ref, x_ref, g_ref, fo_ref, sga_ref, sgf_ref,
                   pr_ref, wo_ref, o_ref, S_ref, D_ref, XF_ref, XB_ref, ZE_ref, rin_ref):
    @pl.when(pl.program_id(0) == 0)
    def _():
        S_ref[...] = s0_ref[...]
        r = lax.broadcasted_iota(jnp.int32, (C, C), 0).astype(F32)
        c = lax.broadcasted_iota(jnp.int32, (C, C), 1).astype(F32)
        d = r - c
        for hd in range(H):
            lgf, lgb = lg_ref[0, hd], lg_ref[1, hd]
            D_ref[hd] = (jnp.where(d >= 0, jnp.exp(lgf * jnp.maximum(d, 0.0)), 0.0)
                         + jnp.where(d <= 0, jnp.exp(lgb * jnp.maximum(-d, 0.0)), 0.0))
            XF_ref[hd] = jnp.exp(lgf * (r + 1.0))
            XB_ref[hd] = jnp.exp(lgb * (C - r))
            ZE_ref[hd] = jnp.exp(lgf * (C - 1.0 - c))

    for ci in range(TB_R // C):
        for hd in range(H):
            g_chunk = jnp.exp(jnp.full((1, DV), lg_ref[0, hd] * C, F32))
            rows = slice(ci * C, (ci + 1) * C)
            qc = q_ref[rows, hd * DK:(hd + 1) * DK]
            kTc = kT_ref[hd * DK:(hd + 1) * DK, rows]
            vc = v_ref[rows, hd * DV:(hd + 1) * DV]
            S = S_ref[hd]
            p = (_dot(qc, kTc) * D_ref[hd]).astype(BF16)
            qf = (qc.astype(F32) * XF_ref[hd]).astype(BF16)
            qb = (qc.astype(F32) * XB_ref[hd]).astype(BF16)
            o = _dot(p, vc) + _dot(qf, S.astype(BF16)) + _dot(qb, sb_ref[hd, ci])
            kz = (kTc.astype(F32) * ZE_ref[hd]).astype(BF16)
            S_ref[hd] = S * g_chunk + _dot(kz, vc)
            mu = jnp.mean(o, axis=-1, keepdims=True)
            d = o - mu
            var = jnp.mean(d * d, axis=-1, keepdims=True)
            on = d * lax.rsqrt(var + EPS)
            rin_ref[rows, hd * DV:(hd + 1) * DV] = (sgw_ref[rows, hd * DV:(hd + 1) * DV].astype(F32) * on).astype(BF16)

    r = _dot(rin_ref[...], pr_ref[...])
    m = sga_ref[...].astype(F32) * r + sgf_ref[...].astype(F32) * fo_ref[...].astype(F32)
    y = _dot(m.astype(BF16), wo_ref[...])
    o_ref[...] = x_ref[...] + g_ref[0:1, :] * y


def _retmix(lg, q, kT, v, s0, sb, sgw, x1, tables, fo, sga, sgf, p_ret, w_out):
    tb = TB_R
    row = lambda i: (i, 0)
    return pl.pallas_call(
        _retmix_kernel,
        grid=(L // tb,),
        in_specs=[
            pl.BlockSpec(memory_space=pltpu.SMEM),
            pl.BlockSpec((tb, QK), row),
            pl.BlockSpec((QK, tb), lambda i: (0, i)),
            pl.BlockSpec((tb, VW), row),
            _resident((H, DK, DV)),
            pl.BlockSpec((H, tb // C, DK, DV), lambda i: (0, i, 0, 0)),
            pl.BlockSpec((tb, VW), row),
            pl.BlockSpec((tb, D), row),
            _mod(tables, 5)[1],
            pl.BlockSpec((tb, D), row), pl.BlockSpec((tb, D), row), pl.BlockSpec((tb, D), row),
            _resident((VW, D)), _resident((D, D)),
        ],
        out_specs=pl.BlockSpec((tb, D), row),
        out_shape=jax.ShapeDtypeStruct((L, D), F32),
        scratch_shapes=[pltpu.VMEM((H, DK, DV), F32)] + [pltpu.VMEM((H, C, C), F32)] * 4
                       + [pltpu.VMEM((tb, VW), BF16)],
        compiler_params=_cparams(1),
        name="retmix",
    )(lg, q, kT, v, s0, sb, sgw, x1, tables[1], fo, sga, sgf, p_ret, w_out)


def _rope_tables():
    rows = L // GRID_W
    row = np.repeat(np.arange(rows, dtype=np.float64), GRID_W)
    col = np.tile(np.arange(GRID_W, dtype=np.float64), rows)
    n_freq = DK // 4
    inv = ROPE_BASE ** (-np.arange(n_freq, dtype=np.float64) / n_freq)
    ang = np.concatenate([row[:, None] * inv, col[:, None] * inv], axis=-1)
    return np.cos(ang), np.sin(ang)


def _dft_tables():
    two_pi = 2.0 * np.pi
    th = (np.outer(np.arange(DG), np.arange(DG)) % DG) * (two_pi / DG)
    md = np.concatenate([np.cos(th), -np.sin(th)], axis=1)
    th = (np.outer(np.arange(N1), np.arange(N1)) % N1) * (two_pi / N1)
    c1, s1 = np.cos(th), np.sin(th)
    m1 = np.block([[c1, s1], [-s1, c1]])
    k = np.arange(N1)[:, None, None] + N1 * np.arange(N2)[None, :, None]
    th = ((k * np.arange(N2)[None, None, :]) % L) * (two_pi / L)
    t3 = np.concatenate([np.cos(th), np.sin(th)], axis=-1) / np.sqrt(float(L) * DG)
    f32 = lambda a: jnp.asarray(a.astype(np.float32))
    return f32(md).astype(BF16), f32(m1).astype(BF16), f32(t3).astype(BF16)


def kernel(x, c, ctx, c_ctx, w_ada, b_ada, norm_ffn1, w13_ffn1, w2_ffn1, norm_mix, w_in, decay_fwd, decay_bwd,
           ret_gn_w, p_ret, p_four, w_out, norm_ffn2, w13_ffn2, w2_ffn2, norm_final):
    assert x.shape == (1, L, D) and ctx.shape == (1, CTX, D) and w_ada.shape[0] == 1

    pm = np.zeros((DK, DK), np.float32)
    pm[np.concatenate([np.arange(0, DK, 2), np.arange(1, DK, 2)]), np.arange(DK)] = 1.0
    perm = jnp.asarray(pm).astype(BF16)
    eye = jnp.asarray(np.eye(DK, dtype=np.float32)).astype(BF16)

    lg = jnp.stack([jax.nn.log_sigmoid(decay_fwd[0].astype(F32)),
                    jax.nn.log_sigmoid(decay_bwd[0].astype(F32))])
    cos, sin = _rope_tables()
    qscale = DK ** -0.5
    f32c = lambda a: jnp.asarray(np.ascontiguousarray(a).astype(np.float32))
    cq, sq, ck, sk = f32c(cos * qscale), f32c(sin * qscale), f32c(cos.T), f32c(sin.T)
    md, m1, t3 = _dft_tables()

    cs = jnp.concatenate([c, c_ctx[None, :], jnp.zeros((6, D), F32)], axis=0).T
    mods_a = _mods(cs, w_ada[0], b_ada)

    x1, ctx1, w13b, w2b, wall, mods_b = _ffn1(x[0], ctx[0], mods_a, norm_ffn1, w13_ffn1[0], w2_ffn1[0],
                                              (w13_ffn2[0], w2_ffn2[0]), w_in[0], perm, w_ada[0], cs, b_ada)
    mods = (mods_a, mods_b)

    s_f, s_b, wkT = _ctx_state(lg, ctx1, mods, norm_mix, eye, wall)

    q, kT, v, sgw, G, sga, sgf, sb, pr, pf, wo = _proj(lg, x1, mods, norm_mix, wall, wkT, md, ret_gn_w, cq, sq, ck, sk,
                                                     s_b, (p_ret[0], p_four[0], w_out[0]))
    a = _fft1(m1, G.reshape(2, N1, N2, D))
    fo = _fft3(t3, a, pf).reshape(L, D)

    x2 = _retmix(lg, q, kT, v, s_f, sb, sgw, x1, mods, fo, sga, sgf, pr, wo)
    out = _ffn2(x2, mods, norm_ffn2, w13b, w2b, norm_final[None, :])
    return out[None]
```

```python
import functools

import numpy as np
import jax
import jax.numpy as jnp
from jax import lax
from jax.experimental import pallas as pl
from jax.experimental.pallas import tpu as pltpu

D = 1024
L = 16384
GRID_W = 64
CTX = 256
H = 4
DK = 256
DV = 512
QK = H * DK
VW = H * DV
NG = 4
DG = 256
DFF = 2816
ROPE_BASE = 10000.0
EPS = 1e-6
NMOD = 9

F32 = jnp.float32
BF16 = jnp.bfloat16

C = 256
N1 = 128
N2 = 128

V7X_VMEM_BYTES = 64 * 1024 * 1024
VMEM_LIMIT = V7X_VMEM_BYTES * 7 // 8


def _dot(a, b):
    return jnp.dot(a, b, preferred_element_type=F32)


def _cparams(n_axes, vmem_limit=VMEM_LIMIT):
    return pltpu.CompilerParams(
        dimension_semantics=("arbitrary",) * n_axes, vmem_limit_bytes=vmem_limit)


def _resident(shape):
    nd = len(shape)
    return pl.BlockSpec(shape, lambda *_: (0,) * nd, pipeline_mode=pl.Buffered(1))


def _mod_spec(j):
    return pl.BlockSpec((8, D), lambda *_: (0, j), pipeline_mode=pl.Buffered(1))


NMOD_A = 3


def _mod(tables, j):
    return (tables[0], _mod_spec(j)) if j < NMOD_A else (tables[1], _mod_spec(j - NMOD_A))


def _norm_mod(x, nw, sc, sh):
    y = x * lax.rsqrt(jnp.mean(x * x, axis=-1, keepdims=True) + EPS)
    return y * (nw * (1.0 + sc)) + sh


def _mods_block(w_ref, cs_ref, b_ref, o_ref):
    cs = cs_ref[...]
    s = cs * jax.nn.sigmoid(cs)
    w = w_ref[...]
    rows = [jnp.sum(w * s[:, r:r + 1], axis=0, keepdims=True) for r in range(2)]
    o_ref[...] = jnp.concatenate(rows + [jnp.zeros((6, w.shape[1]), F32)], axis=0) + b_ref[...]


def _mods_kernel(cs_ref, w_ref, b_ref, o_ref):
    _mods_block(w_ref, cs_ref, b_ref, o_ref)


def _mods(cs, w_ada, b_ada):
    tn = NMOD_A * D // 4
    return pl.pallas_call(
        _mods_kernel,
        grid=(4,),
        in_specs=[
            _resident((D, 8)),
            pl.BlockSpec((D, tn), lambda j: (0, j)),
            pl.BlockSpec((1, tn), lambda j: (0, j)),
        ],
        out_specs=pl.BlockSpec((8, tn), lambda j: (0, j)),
        out_shape=jax.ShapeDtypeStruct((8, NMOD_A * D), F32),
        compiler_params=_cparams(1),
        name="mods",
    )(cs, w_ada, b_ada)


def _cast_blocks(src_refs, dst_refs):
    for s, d in zip(src_refs, dst_refs):
        d[...] = s[...].astype(BF16)


def _cast_specs(arrays, steps):
    specs = [pl.BlockSpec((a.shape[0] // steps, a.shape[1]), lambda i: (i, 0)) for a in arrays]
    shapes = [jax.ShapeDtypeStruct(a.shape, BF16) for a in arrays]
    return specs, shapes


def _prep_block(win_ref, p_ref, wall_ref):
    w = win_ref[...].astype(BF16)
    rb = w.shape[0]
    heads = jnp.concatenate([w[:, hd * DK:(hd + 1) * DK] for hd in range(2 * H)], axis=0)
    perm = _dot(heads, p_ref[...]).astype(BF16)
    for hd in range(H):
        wall_ref[:, _OQ + hd * DK:_OQ + (hd + 1) * DK] = perm[hd * rb:(hd + 1) * rb]
        wall_ref[:, _OK + hd * DK:_OK + (hd + 1) * DK] = perm[(H + hd) * rb:(H + hd + 1) * rb]
    wall_ref[:, QK:_OK] = w[:, 2 * QK:]


MODS_TN = 256


FC = 256


def _swiglu_rows(x, sh, sc, g, nw, w13_ref, w2_ref, t_ref):
    h = _norm_mod(x, nw, sc, sh).astype(BF16)
    for j in range(DFF // FC):
        a = _dot(h, w13_ref[:, j * FC:(j + 1) * FC])
        b = _dot(h, w13_ref[:, DFF + j * FC:DFF + (j + 1) * FC])
        t_ref[:, j * FC:(j + 1) * FC] = (a * jax.nn.sigmoid(a) * b).astype(BF16)
    return x + (0.5 * g) * _dot(t_ref[...], w2_ref[...])


W13_CH = 512
W2_CH = 256


def _load_cast(src_hbm, dst_ref, stage_ref, sem_ref, axis, chunk):
    n = src_hbm.shape[axis] // chunk

    def window(c):
        return (slice(None), pl.ds(c * chunk, chunk)) if axis == 1 else (pl.ds(c * chunk, chunk), slice(None))

    def copy(c):
        return pltpu.make_async_copy(src_hbm.at[window(c)], stage_ref.at[c % 2], sem_ref.at[c % 2])

    copy(0).start()
    for c in range(n):
        if c + 1 < n:
            copy(c + 1).start()
        copy(c).wait()
        dst_ref[window(c)] = stage_ref[c % 2].astype(BF16)


def _ffn1_kernel(ncast, x_ref, ctx_ref, sh_ref, sc_ref, g_ref, nw_ref, w13_hbm, w2_hbm, *rest):
    cast_in, rest = rest[:ncast], rest[ncast:]
    (win_ref, p_ref, wada_ref, cs_ref, b_ref), rest = rest[:5], rest[5:]
    (o_ref, octx_ref), rest = rest[:2], rest[2:]
    cast_out, rest = rest[:ncast], rest[ncast:]
    wall_ref, mb_ref, t_ref, w13_ref, w2_ref, st13_ref, st2_ref, sem_ref = rest

    @pl.when(pl.program_id(0) == 0)
    def _():
        _load_cast(w13_hbm, w13_ref, st13_ref, sem_ref, 1, W13_CH)
        _load_cast(w2_hbm, w2_ref, st2_ref, sem_ref, 0, W2_CH)
        octx_ref[...] = _swiglu_rows(ctx_ref[...], sh_ref[1:2, :], sc_ref[1:2, :], g_ref[1:2, :], nw_ref[...],
                                     w13_ref, w2_ref, t_ref.at[pl.ds(0, CTX)])

    o_ref[...] = _swiglu_rows(x_ref[...], sh_ref[0:1, :], sc_ref[0:1, :], g_ref[0:1, :], nw_ref[...],
                              w13_ref, w2_ref, t_ref)
    _cast_blocks(cast_in, cast_out)
    _prep_block(win_ref, p_ref, wall_ref)
    _mods_block(wada_ref, cs_ref, b_ref, mb_ref)


def _ffn1(x, ctx, mods_a, nw, w13, w2, cast, w_in, perm, w_ada, cs, b_ada, tm=512):
    n = x.shape[0]
    steps = n // tm
    rb = D // steps
    cast_specs, cast_shapes = _cast_specs(cast, steps)
    nlate = (NMOD - NMOD_A) * D
    first, last = NMOD_A * D // MODS_TN, NMOD * D // MODS_TN - 1
    late_in = lambda i: (0, jnp.minimum(first + i, last))
    late_out = lambda i: (0, jnp.minimum(i, last - first))
    rows = lambda i: (i, 0)
    return pl.pallas_call(
        functools.partial(_ffn1_kernel, len(cast)),
        grid=(steps,),
        in_specs=[pl.BlockSpec((tm, D), rows), _resident((CTX, D)),
                  _mod_spec(0), _mod_spec(1), _mod_spec(2), _resident((1, D)),
                  pl.BlockSpec(memory_space=pl.ANY), pl.BlockSpec(memory_space=pl.ANY)]
                 + cast_specs
                 + [pl.BlockSpec((rb, _WIN), rows), _resident((DK, DK)),
                    pl.BlockSpec((D, MODS_TN), late_in), _resident((D, 8)), pl.BlockSpec((1, MODS_TN), late_in)],
        out_specs=[pl.BlockSpec((tm, D), rows), pl.BlockSpec((CTX, D), lambda i: (0, 0))]
                  + cast_specs
                  + [pl.BlockSpec((rb, _WIN), rows), pl.BlockSpec((8, MODS_TN), late_out)],
        out_shape=[jax.ShapeDtypeStruct((n, D), F32), jax.ShapeDtypeStruct((CTX, D), F32)]
                  + cast_shapes
                  + [jax.ShapeDtypeStruct((D, _WIN), BF16), jax.ShapeDtypeStruct((8, nlate), F32)],
        scratch_shapes=[pltpu.VMEM((tm, DFF), BF16), pltpu.VMEM((D, 2 * DFF), BF16), pltpu.VMEM((DFF, D), BF16),
                        pltpu.VMEM((2, D, W13_CH), F32), pltpu.VMEM((2, W2_CH, D), F32),
                        pltpu.SemaphoreType.DMA((2,))],
        compiler_params=_cparams(1),
        name="ffn1",
    )(x, ctx, mods_a, mods_a, mods_a, nw, w13, w2, *cast, w_in, perm, w_ada, cs, b_ada)


def _ffn2_kernel(x_ref, sh_ref, sc_ref, g_ref, nw_ref, w13_ref, w2_ref, nf_ref, o_ref, t_ref):
    out = _swiglu_rows(x_ref[...], sh_ref[0:1, :], sc_ref[0:1, :], g_ref[0:1, :], nw_ref[...], w13_ref, w2_ref, t_ref)
    o_ref[...] = out * lax.rsqrt(jnp.mean(out * out, axis=-1, keepdims=True) + EPS) * nf_ref[...]


def _ffn2(x, tables, nw, w13, w2, nf, tm=1024):
    n = x.shape[0]
    mod_arrays, mod_specs = zip(*[_mod(tables, 6 + k) for k in range(3)])
    return pl.pallas_call(
        _ffn2_kernel,
        grid=(n // tm,),
        in_specs=[pl.BlockSpec((tm, D), lambda i: (i, 0)), *mod_specs,
                  _resident((1, D)), _resident((D, 2 * DFF)), _resident((DFF, D)), _resident((1, D))],
        out_specs=pl.BlockSpec((tm, D), lambda i: (i, 0)),
        out_shape=jax.ShapeDtypeStruct((n, D), F32),
        scratch_shapes=[pltpu.VMEM((tm, DFF), BF16)],
        compiler_params=_cparams(1),
        name="ffn2",
    )(x, *mod_arrays, nw, w13, w2, nf)


def _ctx_state_kernel(lg_ref, x_ref, sh_ref, sc_ref, nw_ref, eye_ref, wk_ref, wv_ref, sf_ref, sb_ref, wkT_ref):
    hd = pl.program_id(0)
    wkT = lax.dot_general(eye_ref[...], wk_ref[...], (((1,), (1,)), ((), ())),
                          preferred_element_type=F32).astype(BF16)
    wkT_ref[...] = wkT
    h = _norm_mod(x_ref[...], nw_ref[...], sc_ref[1:2, :], sh_ref[1:2, :]).astype(BF16)
    kT = lax.dot_general(wkT, h, (((1,), (1,)), ((), ())), preferred_element_type=F32)
    v = _dot(h, wv_ref[...]).astype(BF16)
    j = lax.broadcasted_iota(jnp.int32, (DK, CTX), 1).astype(F32)
    wf = jnp.exp(lg_ref[0, hd] * (CTX - 1.0 - j))
    wb = jnp.exp(lg_ref[1, hd] * j)
    sf_ref[0] = _dot((kT * wf).astype(BF16), v)
    sb_ref[0] = _dot((kT * wb).astype(BF16), v)


def _ctx_state(lg, ctx1, tables, nw, eye, wall):
    return pl.pallas_call(
        _ctx_state_kernel,
        grid=(H,),
        in_specs=[
            pl.BlockSpec(memory_space=pltpu.SMEM),
            _resident((CTX, D)),
            _mod(tables, 3)[1], _mod(tables, 4)[1],
            _resident((1, D)),
            _resident((DK, DK)),
            pl.BlockSpec((D, DK), lambda h: (0, _OK // DK + h)),
            pl.BlockSpec((D, DV), lambda h: (0, _OV // DV + h)),
        ],
        out_specs=[pl.BlockSpec((1, DK, DV), lambda h: (h, 0, 0))] * 2 + [pl.BlockSpec((DK, D), lambda h: (h, 0))],
        out_shape=[jax.ShapeDtypeStruct((H, DK, DV), F32)] * 2 + [jax.ShapeDtypeStruct((QK, D), BF16)],
        compiler_params=_cparams(1),
        name="ctx_state",
    )(lg, ctx1, tables[1], tables[1], nw, eye, wall, wall)


_OQ, _OV, _OG = 0, QK, QK + VW
_OF, _OGA, _OGF = QK + 2 * VW, QK + 2 * VW + D, QK + 2 * VW + 2 * D
_WIN = 2 * QK + 2 * VW + 3 * D
_OK = _WIN - QK
HALF = DK // 2

PT = 2
PROJ_VMEM_LIMIT = V7X_VMEM_BYTES * 29 // 32


def _proj_chunk(rows, ci, lg_ref, x_ref, sh_ref, sc_ref, nw_ref, w_ref, wkT_ref, md_ref, gnw_ref, cq_ref, sq_ref,
                ck_ref, sk_ref, q_ref, kT_ref, v_ref, sg_ref, G_ref, sga_ref, sgf_ref, sb_ref, S_ref, ZE_ref):
    h = _norm_mod(x_ref[rows, :], nw_ref[...], sc_ref[0:1, :], sh_ref[0:1, :]).astype(BF16)

    q = _dot(h, w_ref[:, _OQ:_OQ + QK])
    cq, sq = cq_ref[rows, :], sq_ref[rows, :]
    for hd in range(H):
        a1 = q[:, hd * DK:hd * DK + HALF]
        a2 = q[:, hd * DK + HALF:(hd + 1) * DK]
        q_ref[rows, hd * DK:hd * DK + HALF] = (a1 * cq - a2 * sq).astype(BF16)
        q_ref[rows, hd * DK + HALF:(hd + 1) * DK] = (a1 * sq + a2 * cq).astype(BF16)

    kT = lax.dot_general(wkT_ref[...], h, (((1,), (1,)), ((), ())), preferred_element_type=F32)
    ck, sk = ck_ref[:, rows], sk_ref[:, rows]
    for hd in range(H):
        a1 = kT[hd * DK:hd * DK + HALF, :]
        a2 = kT[hd * DK + HALF:(hd + 1) * DK, :]
        kT_ref[hd * DK:hd * DK + HALF, rows] = (a1 * ck - a2 * sk).astype(BF16)
        kT_ref[hd * DK + HALF:(hd + 1) * DK, rows] = (a1 * sk + a2 * ck).astype(BF16)

    v = _dot(h, w_ref[:, _OV:_OV + VW]).astype(BF16)
    v_ref[rows, :] = v
    for hd in range(H):
        S = S_ref[hd]
        sb_ref[hd, ci] = S.astype(BF16)
        kz = (kT_ref[hd * DK:(hd + 1) * DK, rows].astype(F32) * ZE_ref[hd]).astype(BF16)
        g_chunk = jnp.exp(jnp.full((1, DV), lg_ref[1, hd] * C, F32))
        S_ref[hd] = S * g_chunk + _dot(kz, v[:, hd * DV:(hd + 1) * DV])

    g = _dot(h, w_ref[:, _OG:_OG + VW])
    sg_ref[rows, :] = (g * jax.nn.sigmoid(g) * gnw_ref[...]).astype(BF16)

    f = _dot(h, w_ref[:, _OF:_OF + D]).astype(BF16)
    for gi in range(NG):
        z = _dot(f[:, gi * DG:(gi + 1) * DG], md_ref[...])
        G_ref[0, rows, gi * DG:(gi + 1) * DG] = z[:, :DG].astype(BF16)
        G_ref[1, rows, gi * DG:(gi + 1) * DG] = z[:, DG:].astype(BF16)

    sga_ref[rows, :] = jax.nn.sigmoid(_dot(h, w_ref[:, _OGA:_OGA + D])).astype(BF16)
    sgf_ref[rows, :] = jax.nn.sigmoid(_dot(h, w_ref[:, _OGF:_OGF + D])).astype(BF16)


def _proj_kernel(lg_ref, x_ref, sh_ref, sc_ref, nw_ref, w_ref, wkT_ref, md_ref, gnw_ref, cq_ref, sq_ref, ck_ref,
                 sk_ref, s0_ref, pr_ref, pf_ref, wo_ref, q_ref, kT_ref, v_ref, sg_ref, G_ref, sga_ref, sgf_ref, sb_ref,
                 prb_ref, pfb_ref, wob_ref, S_ref, ZE_ref):
    _cast_blocks((pr_ref, pf_ref, wo_ref), (prb_ref, pfb_ref, wob_ref))

    @pl.when(pl.program_id(0) == 0)
    def _():
        S_ref[...] = s0_ref[...]
        c = lax.broadcasted_iota(jnp.int32, (DK, C), 1).astype(F32)
        for hd in range(H):
            ZE_ref[hd] = jnp.exp(lg_ref[1, hd] * c)

    for ci in range(PT - 1, -1, -1):
        _proj_chunk(slice(ci * C, (ci + 1) * C), ci, lg_ref, x_ref, sh_ref, sc_ref, nw_ref, w_ref, wkT_ref, md_ref,
                    gnw_ref, cq_ref, sq_ref, ck_ref, sk_ref, q_ref, kT_ref, v_ref, sg_ref, G_ref, sga_ref, sgf_ref,
                    sb_ref, S_ref, ZE_ref)


def _proj(lg, x1, tables, nw, wall, wkT, md, gnw, cq, sq, ck, sk, s_b0, cast):
    tm = PT * C
    n = L // tm
    cast_specs, cast_shapes = _cast_specs(cast, n)
    row = lambda i: (n - 1 - i, 0)
    col = lambda i: (0, n - 1 - i)
    return pl.pallas_call(
        _proj_kernel,
        grid=(n,),
        in_specs=[
            pl.BlockSpec(memory_space=pltpu.SMEM),
            pl.BlockSpec((tm, D), row),
            _mod(tables, 3)[1], _mod(tables, 4)[1],
            _resident((1, D)),
            _resident((D, _OK)),
            _resident((QK, D)),
            _resident((DG, 2 * DG)),
            _resident((1, VW)),
            pl.BlockSpec((tm, HALF), row), pl.BlockSpec((tm, HALF), row),
            pl.BlockSpec((HALF, tm), col), pl.BlockSpec((HALF, tm), col),
            _resident((H, DK, DV)),
        ] + cast_specs,
        out_specs=[
            pl.BlockSpec((tm, QK), row),
            pl.BlockSpec((QK, tm), col),
            pl.BlockSpec((tm, VW), row),
            pl.BlockSpec((tm, VW), row),
            pl.BlockSpec((2, tm, D), lambda i: (0, n - 1 - i, 0)),
            pl.BlockSpec((tm, D), row),
            pl.BlockSpec((tm, D), row),
            pl.BlockSpec((H, PT, DK, DV), lambda i: (0, n - 1 - i, 0, 0)),
        ] + cast_specs,
        out_shape=[
            jax.ShapeDtypeStruct((L, QK), BF16),
            jax.ShapeDtypeStruct((QK, L), BF16),
            jax.ShapeDtypeStruct((L, VW), BF16),
            jax.ShapeDtypeStruct((L, VW), BF16),
            jax.ShapeDtypeStruct((2, L, D), BF16),
            jax.ShapeDtypeStruct((L, D), BF16),
            jax.ShapeDtypeStruct((L, D), BF16),
            jax.ShapeDtypeStruct((H, L // C, DK, DV), BF16),
        ] + cast_shapes,
        scratch_shapes=[pltpu.VMEM((H, DK, DV), F32), pltpu.VMEM((H, DK, C), F32)],
        compiler_params=_cparams(1, PROJ_VMEM_LIMIT),
        name="proj",
    )(lg, x1, tables[1], tables[1], nw, wall, wkT, md, gnw, cq, sq, ck, sk, s_b0, *cast)


FFT_NT = 16
FFT_KB = 16


def _fft1_kernel(m_ref, g_ref, o_ref, r_ref):
    z = jnp.transpose(g_ref[...], (2, 0, 1, 3)).reshape(FFT_NT, 2 * N1, D)
    for t in range(FFT_NT):
        r_ref[t] = _dot(m_ref[...], z[t]).astype(BF16).reshape(2, N1, D)
    o_ref[...] = jnp.transpose(r_ref[...], (2, 1, 0, 3))


def _fft1(m1, g4):
    return pl.pallas_call(
        _fft1_kernel,
        grid=(N2 // FFT_NT,),
        in_specs=[_resident((2 * N1, 2 * N1)),
                  pl.BlockSpec((2, N1, FFT_NT, D), lambda j: (0, 0, j, 0))],
        out_specs=pl.BlockSpec((N1, 2, FFT_NT, D), lambda j: (0, 0, j, 0)),
        out_shape=jax.ShapeDtypeStruct((N1, 2, N2, D), BF16),
        scratch_shapes=[pltpu.VMEM((FFT_NT, 2, N1, D), BF16)],
        compiler_params=_cparams(1),
        name="fft1",
    )(m1, g4)


def _fft3_kernel(t_ref, pf_ref, a_ref, o_ref, xr_ref):
    for i in range(FFT_KB):
        xr_ref[i * N2:(i + 1) * N2, :] = _dot(t_ref[i], a_ref[i].reshape(2 * N2, D)).astype(BF16)
    fo = _dot(xr_ref[...], pf_ref[...]).astype(BF16).reshape(FFT_KB, N2, D)
    o_ref[...] = jnp.transpose(fo, (1, 0, 2))


def _fft3(t3, a4, pf):
    return pl.pallas_call(
        _fft3_kernel,
        grid=(N1 // FFT_KB,),
        in_specs=[pl.BlockSpec((FFT_KB, N2, 2 * N2), lambda j: (j, 0, 0)),
                  _resident((D, D)),
                  pl.BlockSpec((FFT_KB, 2, N2, D), lambda j: (j, 0, 0, 0))],
        out_specs=pl.BlockSpec((N2, FFT_KB, D), lambda j: (0, j, 0)),
        out_shape=jax.ShapeDtypeStruct((N2, N1, D), BF16),
        scratch_shapes=[pltpu.VMEM((FFT_KB * N2, D), BF16)],
        compiler_params=_cparams(1),
        name="fft3",
    )(t3, pf, a4)


TB_R = 512


def _retmix_kernel(lg_ref, q_ref, kT_ref, v_ref, s0_ref, sb_ref, sgw_ref, x_ref, g_ref, fo_ref, sga_ref, sgf_ref,
                   pr_ref, wo_ref, o_ref, S_ref, D_ref, XF_ref, XB_ref, ZE_ref, rin_ref):
    @pl.when(pl.program_id(0) == 0)
    def _():
        S_ref[...] = s0_ref[...]
        r = lax.broadcasted_iota(jnp.int32, (C, C), 0).astype(F32)
        c = lax.broadcasted_iota(jnp.int32, (C, C), 1).astype(F32)
        d = r - c
        for hd in range(H):
            lgf, lgb = lg_ref[0, hd], lg_ref[1, hd]
            D_ref[hd] = (jnp.where(d >= 0, jnp.exp(lgf * jnp.maximum(d, 0.0)), 0.0)
                         + jnp.where(d <= 0, jnp.exp(lgb * jnp.maximum(-d, 0.0)), 0.0))
            XF_ref[hd] = jnp.exp(lgf * (r + 1.0))
            XB_ref[hd] = jnp.exp(lgb * (C - r))
            ZE_ref[hd] = jnp.exp(lgf * (C - 1.0 - c))

    for ci in range(TB_R // C):
        for hd in range(H):
            g_chunk = jnp.exp(jnp.full((1, DV), lg_ref[0, hd] * C, F32))
            rows = slice(ci * C, (ci + 1) * C)
            qc = q_ref[rows, hd * DK:(hd + 1) * DK]
            kTc = kT_ref[hd * DK:(hd + 1) * DK, rows]
            vc = v_ref[rows, hd * DV:(hd + 1) * DV]
            S = S_ref[hd]
            kz = (kTc.astype(F32) * ZE_ref[hd]).astype(BF16)
            S_ref[hd] = S * g_chunk + _dot(kz, vc)
            p = (_dot(qc, kTc) * D_ref[hd]).astype(BF16)
            qf = (qc.astype(F32) * XF_ref[hd]).astype(BF16)
            qb = (qc.astype(F32) * XB_ref[hd]).astype(BF16)
            o = _dot(p, vc) + _dot(qf, S.astype(BF16)) + _dot(qb, sb_ref[hd, ci])
            mu = jnp.mean(o, axis=-1, keepdims=True)
            d = o - mu
            var = jnp.mean(d * d, axis=-1, keepdims=True)
            on = d * lax.rsqrt(var + EPS)
            rin_ref[rows, hd * DV:(hd + 1) * DV] = (sgw_ref[rows, hd * DV:(hd + 1) * DV].astype(F32) * on).astype(BF16)

    r = _dot(rin_ref[...], pr_ref[...])
    m = sga_ref[...].astype(F32) * r + sgf_ref[...].astype(F32) * fo_ref[...].astype(F32)
    y = _dot(m.astype(BF16), wo_ref[...])
    o_ref[...] = x_ref[...] + g_ref[0:1, :] * y


def _retmix(lg, q, kT, v, s0, sb, sgw, x1, tables, fo, sga, sgf, p_ret, w_out):
    tb = TB_R
    row = lambda i: (i, 0)
    return pl.pallas_call(
        _retmix_kernel,
        grid=(L // tb,),
        in_specs=[
            pl.BlockSpec(memory_space=pltpu.SMEM),
            pl.BlockSpec((tb, QK), row),
            pl.BlockSpec((QK, tb), lambda i: (0, i)),
            pl.BlockSpec((tb, VW), row),
            _resident((H, DK, DV)),
            pl.BlockSpec((H, tb // C, DK, DV), lambda i: (0, i, 0, 0)),
            pl.BlockSpec((tb, VW), row),
            pl.BlockSpec((tb, D), row),
            _mod(tables, 5)[1],
            pl.BlockSpec((tb, D), row), pl.BlockSpec((tb, D), row), pl.BlockSpec((tb, D), row),
            _resident((VW, D)), _resident((D, D)),
        ],
        out_specs=pl.BlockSpec((tb, D), row),
        out_shape=jax.ShapeDtypeStruct((L, D), F32),
        scratch_shapes=[pltpu.VMEM((H, DK, DV), F32)] + [pltpu.VMEM((H, C, C), F32)] * 4
                       + [pltpu.VMEM((tb, VW), BF16)],
        compiler_params=_cparams(1),
        name="retmix",
    )(lg, q, kT, v, s0, sb, sgw, x1, tables[1], fo, sga, sgf, p_ret, w_out)


def _rope_tables():
    rows = L // GRID_W
    row = np.repeat(np.arange(rows, dtype=np.float64), GRID_W)
    col = np.tile(np.arange(GRID_W, dtype=np.float64), rows)
    n_freq = DK // 4
    inv = ROPE_BASE ** (-np.arange(n_freq, dtype=np.float64) / n_freq)
    ang = np.concatenate([row[:, None] * inv, col[:, None] * inv], axis=-1)
    return np.cos(ang), np.sin(ang)


def _dft_tables():
    two_pi = 2.0 * np.pi
    th = (np.outer(np.arange(DG), np.arange(DG)) % DG) * (two_pi / DG)
    md = np.concatenate([np.cos(th), -np.sin(th)], axis=1)
    th = (np.outer(np.arange(N1), np.arange(N1)) % N1) * (two_pi / N1)
    c1, s1 = np.cos(th), np.sin(th)
    m1 = np.block([[c1, s1], [-s1, c1]])
    k = np.arange(N1)[:, None, None] + N1 * np.arange(N2)[None, :, None]
    th = ((k * np.arange(N2)[None, None, :]) % L) * (two_pi / L)
    t3 = np.concatenate([np.cos(th), np.sin(th)], axis=-1) / np.sqrt(float(L) * DG)
    f32 = lambda a: jnp.asarray(a.astype(np.float32))
    return f32(md).astype(BF16), f32(m1).astype(BF16), f32(t3).astype(BF16)


def kernel(x, c, ctx, c_ctx, w_ada, b_ada, norm_ffn1, w13_ffn1, w2_ffn1, norm_mix, w_in, decay_fwd, decay_bwd,
           ret_gn_w, p_ret, p_four, w_out, norm_ffn2, w13_ffn2, w2_ffn2, norm_final):
    assert x.shape == (1, L, D) and ctx.shape == (1, CTX, D) and w_ada.shape[0] == 1

    pm = np.zeros((DK, DK), np.float32)
    pm[np.concatenate([np.arange(0, DK, 2), np.arange(1, DK, 2)]), np.arange(DK)] = 1.0
    perm = jnp.asarray(pm).astype(BF16)
    eye = jnp.asarray(np.eye(DK, dtype=np.float32)).astype(BF16)

    lg = jnp.stack([jax.nn.log_sigmoid(decay_fwd[0].astype(F32)),
                    jax.nn.log_sigmoid(decay_bwd[0].astype(F32))])
    cos, sin = _rope_tables()
    qscale = DK ** -0.5
    f32c = lambda a: jnp.asarray(np.ascontiguousarray(a).astype(np.float32))
    cq, sq, ck, sk = f32c(cos * qscale), f32c(sin * qscale), f32c(cos.T), f32c(sin.T)
    md, m1, t3 = _dft_tables()

    cs = jnp.concatenate([c, c_ctx[None, :], jnp.zeros((6, D), F32)], axis=0).T
    mods_a = _mods(cs, w_ada[0], b_ada)

    x1, ctx1, w13b, w2b, wall, mods_b = _ffn1(x[0], ctx[0], mods_a, norm_ffn1, w13_ffn1[0], w2_ffn1[0],
                                              (w13_ffn2[0], w2_ffn2[0]), w_in[0], perm, w_ada[0], cs, b_ada)
    mods = (mods_a, mods_b)

    s_f, s_b, wkT = _ctx_state(lg, ctx1, mods, norm_mix, eye, wall)

    q, kT, v, sgw, G, sga, sgf, sb, pr, pf, wo = _proj(lg, x1, mods, norm_mix, wall, wkT, md, ret_gn_w, cq, sq, ck, sk,
                                                     s_b, (p_ret[0], p_four[0], w_out[0]))
    a = _fft1(m1, G.reshape(2, N1, N2, D))
    fo = _fft3(t3, a, pf).reshape(L, D)

    x2 = _retmix(lg, q, kT, v, s_f, sb, sgw, x1, mods, fo, sga, sgf, pr, wo)
    out = _ffn2(x2, mods, norm_ffn2, w13b, w2b, norm_final[None, :])
    return out[None]
```

```python
import functools

import numpy as np
import jax
import jax.numpy as jnp
from jax import lax
from jax.experimental import pallas as pl
from jax.experimental.pallas import tpu as pltpu

D = 1024
L = 16384
GRID_W = 64
CTX = 256
H = 4
DK = 256
DV = 512
QK = H * DK
VW = H * DV
NG = 4
DG = 256
DFF = 2816
ROPE_BASE = 10000.0
EPS = 1e-6
NMOD = 9

F32 = jnp.float32
BF16 = jnp.bfloat16

C = 256
N1 = 128
N2 = 128

V7X_VMEM_BYTES = 64 * 1024 * 1024
VMEM_LIMIT = V7X_VMEM_BYTES * 7 // 8


def _dot(a, b):
    return jnp.dot(a, b, preferred_element_type=F32)


def _cparams(n_axes, vmem_limit=VMEM_LIMIT):
    return pltpu.CompilerParams(
        dimension_semantics=("arbitrary",) * n_axes, vmem_limit_bytes=vmem_limit)


def _resident(shape):
    nd = len(shape)
    return pl.BlockSpec(shape, lambda *_: (0,) * nd, pipeline_mode=pl.Buffered(1))


def _mod_spec(j):
    return pl.BlockSpec((8, D), lambda *_: (0, j), pipeline_mode=pl.Buffered(1))


NMOD_A = 3


def _mod(tables, j):
    return (tables[0], _mod_spec(j)) if j < NMOD_A else (tables[1], _mod_spec(j - NMOD_A))


def _norm_mod(x, nw, sc, sh):
    y = x * lax.rsqrt(jnp.mean(x * x, axis=-1, keepdims=True) + EPS)
    return y * (nw * (1.0 + sc)) + sh


def _mods_block(w_ref, cs_ref, b_ref, o_ref):
    cs = cs_ref[...]
    s = cs * jax.nn.sigmoid(cs)
    w = w_ref[...]
    rows = [jnp.sum(w * s[:, r:r + 1], axis=0, keepdims=True) for r in range(2)]
    o_ref[...] = jnp.concatenate(rows + [jnp.zeros((6, w.shape[1]), F32)], axis=0) + b_ref[...]


def _mods_kernel(cs_ref, w_ref, b_ref, o_ref):
    _mods_block(w_ref, cs_ref, b_ref, o_ref)


def _mods(cs, w_ada, b_ada):
    tn = NMOD_A * D // 4
    return pl.pallas_call(
        _mods_kernel,
        grid=(4,),
        in_specs=[
            _resident((D, 8)),
            pl.BlockSpec((D, tn), lambda j: (0, j)),
            pl.BlockSpec((1, tn), lambda j: (0, j)),
        ],
        out_specs=pl.BlockSpec((8, tn), lambda j: (0, j)),
        out_shape=jax.ShapeDtypeStruct((8, NMOD_A * D), F32),
        compiler_params=_cparams(1),
        name="mods",
    )(cs, w_ada, b_ada)


def _cast_blocks(src_refs, dst_refs):
    for s, d in zip(src_refs, dst_refs):
        d[...] = s[...].astype(BF16)


def _cast_specs(arrays, steps):
    specs = [pl.BlockSpec((a.shape[0] // steps, a.shape[1]), lambda i: (i, 0)) for a in arrays]
    shapes = [jax.ShapeDtypeStruct(a.shape, BF16) for a in arrays]
    return specs, shapes


def _prep_block(win_ref, p_ref, wall_ref):
    w = win_ref[...].astype(BF16)
    rb = w.shape[0]
    heads = jnp.concatenate([w[:, hd * DK:(hd + 1) * DK] for hd in range(2 * H)], axis=0)
    perm = _dot(heads, p_ref[...]).astype(BF16)
    for hd in range(H):
        wall_ref[:, _OQ + hd * DK:_OQ + (hd + 1) * DK] = perm[hd * rb:(hd + 1) * rb]
        wall_ref[:, _OK + hd * DK:_OK + (hd + 1) * DK] = perm[(H + hd) * rb:(H + hd + 1) * rb]
    wall_ref[:, QK:_OK] = w[:, 2 * QK:]


MODS_TN = 256


FC = 256


def _swiglu_rows(x, sh, sc, g, nw, w13_ref, w2_ref, t_ref):
    h = _norm_mod(x, nw, sc, sh).astype(BF16)
    for j in range(DFF // FC):
        a = _dot(h, w13_ref[:, j * FC:(j + 1) * FC])
        b = _dot(h, w13_ref[:, DFF + j * FC:DFF + (j + 1) * FC])
        t_ref[:, j * FC:(j + 1) * FC] = (a * jax.nn.sigmoid(a) * b).astype(BF16)
    return x + (0.5 * g) * _dot(t_ref[...], w2_ref[...])


W13_CH = 512
W2_CH = 256


def _load_cast(src_hbm, dst_ref, stage_ref, sem_ref, axis, chunk):
    n = src_hbm.shape[axis] // chunk

    def window(c):
        return (slice(None), pl.ds(c * chunk, chunk)) if axis == 1 else (pl.ds(c * chunk, chunk), slice(None))

    def copy(c):
        return pltpu.make_async_copy(src_hbm.at[window(c)], stage_ref.at[c % 2], sem_ref.at[c % 2])

    copy(0).start()
    for c in range(n):
        if c + 1 < n:
            copy(c + 1).start()
        copy(c).wait()
        dst_ref[window(c)] = stage_ref[c % 2].astype(BF16)


def _ffn1_kernel(ncast, x_ref, ctx_ref, sh_ref, sc_ref, g_ref, nw_ref, w13_hbm, w2_hbm, *rest):
    cast_in, rest = rest[:ncast], rest[ncast:]
    (win_ref, p_ref, wada_ref, cs_ref, b_ref), rest = rest[:5], rest[5:]
    (o_ref, octx_ref), rest = rest[:2], rest[2:]
    cast_out, rest = rest[:ncast], rest[ncast:]
    wall_ref, mb_ref, t_ref, w13_ref, w2_ref, st13_ref, st2_ref, sem_ref = rest

    @pl.when(pl.program_id(0) == 0)
    def _():
        _load_cast(w13_hbm, w13_ref, st13_ref, sem_ref, 1, W13_CH)
        _load_cast(w2_hbm, w2_ref, st2_ref, sem_ref, 0, W2_CH)
        octx_ref[...] = _swiglu_rows(ctx_ref[...], sh_ref[1:2, :], sc_ref[1:2, :], g_ref[1:2, :], nw_ref[...],
                                     w13_ref, w2_ref, t_ref.at[pl.ds(0, CTX)])

    o_ref[...] = _swiglu_rows(x_ref[...], sh_ref[0:1, :], sc_ref[0:1, :], g_ref[0:1, :], nw_ref[...],
                              w13_ref, w2_ref, t_ref)
    _cast_blocks(cast_in, cast_out)
    _prep_block(win_ref, p_ref, wall_ref)
    _mods_block(wada_ref, cs_ref, b_ref, mb_ref)


def _ffn1(x, ctx, mods_a, nw, w13, w2, cast, w_in, perm, w_ada, cs, b_ada, tm=512):
    n = x.shape[0]
    steps = n // tm
    rb = D // steps
    cast_specs, cast_shapes = _cast_specs(cast, steps)
    nlate = (NMOD - NMOD_A) * D
    first, last = NMOD_A * D // MODS_TN, NMOD * D // MODS_TN - 1
    late_in = lambda i: (0, jnp.minimum(first + i, last))
    late_out = lambda i: (0, jnp.minimum(i, last - first))
    rows = lambda i: (i, 0)
    return pl.pallas_call(
        functools.partial(_ffn1_kernel, len(cast)),
        grid=(steps,),
        in_specs=[pl.BlockSpec((tm, D), rows), _resident((CTX, D)),
                  _mod_spec(0), _mod_spec(1), _mod_spec(2), _resident((1, D)),
                  pl.BlockSpec(memory_space=pl.ANY), pl.BlockSpec(memory_space=pl.ANY)]
                 + cast_specs
                 + [pl.BlockSpec((rb, _WIN), rows), _resident((DK, DK)),
                    pl.BlockSpec((D, MODS_TN), late_in), _resident((D, 8)), pl.BlockSpec((1, MODS_TN), late_in)],
        out_specs=[pl.BlockSpec((tm, D), rows), pl.BlockSpec((CTX, D), lambda i: (0, 0))]
                  + cast_specs
                  + [pl.BlockSpec((rb, _WIN), rows), pl.BlockSpec((8, MODS_TN), late_out)],
        out_shape=[jax.ShapeDtypeStruct((n, D), F32), jax.ShapeDtypeStruct((CTX, D), F32)]
                  + cast_shapes
                  + [jax.ShapeDtypeStruct((D, _WIN), BF16), jax.ShapeDtypeStruct((8, nlate), F32)],
        scratch_shapes=[pltpu.VMEM((tm, DFF), BF16), pltpu.VMEM((D, 2 * DFF), BF16), pltpu.VMEM((DFF, D), BF16),
                        pltpu.VMEM((2, D, W13_CH), F32), pltpu.VMEM((2, W2_CH, D), F32),
                        pltpu.SemaphoreType.DMA((2,))],
        compiler_params=_cparams(1),
        name="ffn1",
    )(x, ctx, mods_a, mods_a, mods_a, nw, w13, w2, *cast, w_in, perm, w_ada, cs, b_ada)


def _ffn2_kernel(x_ref, sh_ref, sc_ref, g_ref, nw_ref, w13_ref, w2_ref, nf_ref, o_ref, t_ref):
    out = _swiglu_rows(x_ref[...], sh_ref[0:1, :], sc_ref[0:1, :], g_ref[0:1, :], nw_ref[...], w13_ref, w2_ref, t_ref)
    o_ref[...] = out * lax.rsqrt(jnp.mean(out * out, axis=-1, keepdims=True) + EPS) * nf_ref[...]


def _ffn2(x, tables, nw, w13, w2, nf, tm=1024):
    n = x.shape[0]
    mod_arrays, mod_specs = zip(*[_mod(tables, 6 + k) for k in range(3)])
    return pl.pallas_call(
        _ffn2_kernel,
        grid=(n // tm,),
        in_specs=[pl.BlockSpec((tm, D), lambda i: (i, 0)), *mod_specs,
                  _resident((1, D)), _resident((D, 2 * DFF)), _resident((DFF, D)), _resident((1, D))],
        out_specs=pl.BlockSpec((tm, D), lambda i: (i, 0)),
        out_shape=jax.ShapeDtypeStruct((n, D), F32),
        scratch_shapes=[pltpu.VMEM((tm, DFF), BF16)],
        compiler_params=_cparams(1),
        name="ffn2",
    )(x, *mod_arrays, nw, w13, w2, nf)


def _ctx_state_kernel(lg_ref, x_ref, sh_ref, sc_ref, nw_ref, eye_ref, wk_ref, wv_ref, sf_ref, sb_ref, wkT_ref):
    hd = pl.program_id(0)
    wkT = lax.dot_general(eye_ref[...], wk_ref[...], (((1,), (1,)), ((), ())),
                          preferred_element_type=F32).astype(BF16)
    wkT_ref[...] = wkT
    h = _norm_mod(x_ref[...], nw_ref[...], sc_ref[1:2, :], sh_ref[1:2, :]).astype(BF16)
    kT = lax.dot_general(wkT, h, (((1,), (1,)), ((), ())), preferred_element_type=F32)
    v = _dot(h, wv_ref[...]).astype(BF16)
    j = lax.broadcasted_iota(jnp.int32, (DK, CTX), 1).astype(F32)
    wf = jnp.exp(lg_ref[0, hd] * (CTX - 1.0 - j))
    wb = jnp.exp(lg_ref[1, hd] * j)
    sf_ref[0] = _dot((kT * wf).astype(BF16), v)
    sb_ref[0] = _dot((kT * wb).astype(BF16), v)


def _ctx_state(lg, ctx1, tables, nw, eye, wall):
    return pl.pallas_call(
        _ctx_state_kernel,
        grid=(H,),
        in_specs=[
            pl.BlockSpec(memory_space=pltpu.SMEM),
            _resident((CTX, D)),
            _mod(tables, 3)[1], _mod(tables, 4)[1],
            _resident((1, D)),
            _resident((DK, DK)),
            pl.BlockSpec((D, DK), lambda h: (0, _OK // DK + h)),
            pl.BlockSpec((D, DV), lambda h: (0, _OV // DV + h)),
        ],
        out_specs=[pl.BlockSpec((1, DK, DV), lambda h: (h, 0, 0))] * 2 + [pl.BlockSpec((DK, D), lambda h: (h, 0))],
        out_shape=[jax.ShapeDtypeStruct((H, DK, DV), F32)] * 2 + [jax.ShapeDtypeStruct((QK, D), BF16)],
        compiler_params=_cparams(1),
        name="ctx_state",
    )(lg, ctx1, tables[1], tables[1], nw, eye, wall, wall)


_OQ, _OV, _OG = 0, QK, QK + VW
_OF, _OGA, _OGF = QK + 2 * VW, QK + 2 * VW + D, QK + 2 * VW + 2 * D
_WIN = 2 * QK + 2 * VW + 3 * D
_OK = _WIN - QK
HALF = DK // 2

PT = 2
PROJ_VMEM_LIMIT = V7X_VMEM_BYTES * 29 // 32


def _proj_chunk(rows, ci, lg_ref, x_ref, sh_ref, sc_ref, nw_ref, w_ref, wkT_ref, md_ref, gnw_ref, cq_ref, sq_ref,
                ck_ref, sk_ref, q_ref, kT_ref, v_ref, sg_ref, G_ref, sga_ref, sgf_ref, sb_ref, S_ref, ZE_ref):
    h = _norm_mod(x_ref[rows, :], nw_ref[...], sc_ref[0:1, :], sh_ref[0:1, :]).astype(BF16)

    sga_ref[rows, :] = jax.nn.sigmoid(_dot(h, w_ref[:, _OGA:_OGA + D])).astype(BF16)
    sgf_ref[rows, :] = jax.nn.sigmoid(_dot(h, w_ref[:, _OGF:_OGF + D])).astype(BF16)

    q = _dot(h, w_ref[:, _OQ:_OQ + QK])
    cq, sq = cq_ref[rows, :], sq_ref[rows, :]
    for hd in range(H):
        a1 = q[:, hd * DK:hd * DK + HALF]
        a2 = q[:, hd * DK + HALF:(hd + 1) * DK]
        q_ref[rows, hd * DK:hd * DK + HALF] = (a1 * cq - a2 * sq).astype(BF16)
        q_ref[rows, hd * DK + HALF:(hd + 1) * DK] = (a1 * sq + a2 * cq).astype(BF16)

    kT = lax.dot_general(wkT_ref[...], h, (((1,), (1,)), ((), ())), preferred_element_type=F32)
    ck, sk = ck_ref[:, rows], sk_ref[:, rows]
    for hd in range(H):
        a1 = kT[hd * DK:hd * DK + HALF, :]
        a2 = kT[hd * DK + HALF:(hd + 1) * DK, :]
        kT_ref[hd * DK:hd * DK + HALF, rows] = (a1 * ck - a2 * sk).astype(BF16)
        kT_ref[hd * DK + HALF:(hd + 1) * DK, rows] = (a1 * sk + a2 * ck).astype(BF16)

    v = _dot(h, w_ref[:, _OV:_OV + VW]).astype(BF16)
    v_ref[rows, :] = v
    for hd in range(H):
        S = S_ref[hd]
        sb_ref[hd, ci] = S.astype(BF16)
        kz = (kT_ref[hd * DK:(hd + 1) * DK, rows].astype(F32) * ZE_ref[hd]).astype(BF16)
        g_chunk = jnp.exp(jnp.full((1, DV), lg_ref[1, hd] * C, F32))
        S_ref[hd] = S * g_chunk + _dot(kz, v[:, hd * DV:(hd + 1) * DV])

    g = _dot(h, w_ref[:, _OG:_OG + VW])
    sg_ref[rows, :] = (g * jax.nn.sigmoid(g) * gnw_ref[...]).astype(BF16)

    f = _dot(h, w_ref[:, _OF:_OF + D]).astype(BF16)
    for gi in range(NG):
        z = _dot(f[:, gi * DG:(gi + 1) * DG], md_ref[...])
        G_ref[0, rows, gi * DG:(gi + 1) * DG] = z[:, :DG].astype(BF16)
        G_ref[1, rows, gi * DG:(gi + 1) * DG] = z[:, DG:].astype(BF16)


def _proj_kernel(lg_ref, x_ref, sh_ref, sc_ref, nw_ref, w_ref, wkT_ref, md_ref, gnw_ref, cq_ref, sq_ref, ck_ref,
                 sk_ref, s0_ref, pr_ref, pf_ref, wo_ref, q_ref, kT_ref, v_ref, sg_ref, G_ref, sga_ref, sgf_ref, sb_ref,
                 prb_ref, pfb_ref, wob_ref, S_ref, ZE_ref):
    _cast_blocks((pr_ref, pf_ref, wo_ref), (prb_ref, pfb_ref, wob_ref))

    @pl.when(pl.program_id(0) == 0)
    def _():
        S_ref[...] = s0_ref[...]
        c = lax.broadcasted_iota(jnp.int32, (DK, C), 1).astype(F32)
        for hd in range(H):
            ZE_ref[hd] = jnp.exp(lg_ref[1, hd] * c)

    for ci in range(PT - 1, -1, -1):
        _proj_chunk(slice(ci * C, (ci + 1) * C), ci, lg_ref, x_ref, sh_ref, sc_ref, nw_ref, w_ref, wkT_ref, md_ref,
                    gnw_ref, cq_ref, sq_ref, ck_ref, sk_ref, q_ref, kT_ref, v_ref, sg_ref, G_ref, sga_ref, sgf_ref,
                    sb_ref, S_ref, ZE_ref)


def _proj(lg, x1, tables, nw, wall, wkT, md, gnw, cq, sq, ck, sk, s_b0, cast):
    tm = PT * C
    n = L // tm
    cast_specs, cast_shapes = _cast_specs(cast, n)
    row = lambda i: (n - 1 - i, 0)
    col = lambda i: (0, n - 1 - i)
    return pl.pallas_call(
        _proj_kernel,
        grid=(n,),
        in_specs=[
            pl.BlockSpec(memory_space=pltpu.SMEM),
            pl.BlockSpec((tm, D), row),
            _mod(tables, 3)[1], _mod(tables, 4)[1],
            _resident((1, D)),
            _resident((D, _OK)),
            _resident((QK, D)),
            _resident((DG, 2 * DG)),
            _resident((1, VW)),
            pl.BlockSpec((tm, HALF), row), pl.BlockSpec((tm, HALF), row),
            pl.BlockSpec((HALF, tm), col), pl.BlockSpec((HALF, tm), col),
            _resident((H, DK, DV)),
        ] + cast_specs,
        out_specs=[
            pl.BlockSpec((tm, QK), row),
            pl.BlockSpec((QK, tm), col),
            pl.BlockSpec((tm, VW), row),
            pl.BlockSpec((tm, VW), row),
            pl.BlockSpec((2, tm, D), lambda i: (0, n - 1 - i, 0)),
            pl.BlockSpec((tm, D), row),
            pl.BlockSpec((tm, D), row),
            pl.BlockSpec((H, PT, DK, DV), lambda i: (0, n - 1 - i, 0, 0)),
        ] + cast_specs,
        out_shape=[
            jax.ShapeDtypeStruct((L, QK), BF16),
            jax.ShapeDtypeStruct((QK, L), BF16),
            jax.ShapeDtypeStruct((L, VW), BF16),
            jax.ShapeDtypeStruct((L, VW), BF16),
            jax.ShapeDtypeStruct((2, L, D), BF16),
            jax.ShapeDtypeStruct((L, D), BF16),
            jax.ShapeDtypeStruct((L, D), BF16),
            jax.ShapeDtypeStruct((H, L // C, DK, DV), BF16),
        ] + cast_shapes,
        scratch_shapes=[pltpu.VMEM((H, DK, DV), F32), pltpu.VMEM((H, DK, C), F32)],
        compiler_params=_cparams(1, PROJ_VMEM_LIMIT),
        name="proj",
    )(lg, x1, tables[1], tables[1], nw, wall, wkT, md, gnw, cq, sq, ck, sk, s_b0, *cast)


FFT_NT = 16
FFT_KB = 16


def _fft1_kernel(m_ref, g_ref, o_ref, r_ref):
    z = jnp.transpose(g_ref[...], (2, 0, 1, 3)).reshape(FFT_NT, 2 * N1, D)
    for t in range(FFT_NT):
        r_ref[t] = _dot(m_ref[...], z[t]).astype(BF16).reshape(2, N1, D)
    o_ref[...] = jnp.transpose(r_ref[...], (2, 1, 0, 3))


def _fft1(m1, g4):
    return pl.pallas_call(
        _fft1_kernel,
        grid=(N2 // FFT_NT,),
        in_specs=[_resident((2 * N1, 2 * N1)),
                  pl.BlockSpec((2, N1, FFT_NT, D), lambda j: (0, 0, j, 0))],
        out_specs=pl.BlockSpec((N1, 2, FFT_NT, D), lambda j: (0, 0, j, 0)),
        out_shape=jax.ShapeDtypeStruct((N1, 2, N2, D), BF16),
        scratch_shapes=[pltpu.VMEM((FFT_NT, 2, N1, D), BF16)],
        compiler_params=_cparams(1),
        name="fft1",
    )(m1, g4)


def _fft3_kernel(t_ref, pf_ref, a_ref, o_ref, xr_ref):
    for i in range(FFT_KB):
        xr_ref[i * N2:(i + 1) * N2, :] = _dot(t_ref[i], a_ref[i].reshape(2 * N2, D)).astype(BF16)
    fo = _dot(xr_ref[...], pf_ref[...]).astype(BF16).reshape(FFT_KB, N2, D)
    o_ref[...] = jnp.transpose(fo, (1, 0, 2))


def _fft3(t3, a4, pf):
    return pl.pallas_call(
        _fft3_kernel,
        grid=(N1 // FFT_KB,),
        in_specs=[pl.BlockSpec((FFT_KB, N2, 2 * N2), lambda j: (j, 0, 0)),
                  _resident((D, D)),
                  pl.BlockSpec((FFT_KB, 2, N2, D), lambda j: (j, 0, 0, 0))],
        out_specs=pl.BlockSpec((N2, FFT_KB, D), lambda j: (0, j, 0)),
        out_shape=jax.ShapeDtypeStruct((N2, N1, D), BF16),
        scratch_shapes=[pltpu.VMEM((FFT_KB * N2, D), BF16)],
        compiler_params=_cparams(1),
        name="fft3",
    )(t3, pf, a4)


TB_R = 512


def _retmix_kernel(lg_ref, q_ref, kT_ref, v_ref, s0_ref, sb_ref, sgw_ref, x_ref, g_ref, fo_ref, sga_ref, sgf_ref,
                   pr_ref, wo_ref, o_ref, S_ref, D_ref, XF_ref, XB_ref, ZE_ref, rin_ref):
    @pl.when(pl.program_id(0) == 0)
    def _():
        S_ref[...] = s0_ref[...]
        r = lax.broadcasted_iota(jnp.int32, (C, C), 0).astype(F32)
        c = lax.broadcasted_iota(jnp.int32, (C, C), 1).astype(F32)
        d = r - c
        for hd in range(H):
            lgf, lgb = lg_ref[0, hd], lg_ref[1, hd]
            D_ref[hd] = (jnp.where(d >= 0, jnp.exp(lgf * jnp.maximum(d, 0.0)), 0.0)
                         + jnp.where(d <= 0, jnp.exp(lgb * jnp.maximum(-d, 0.0)), 0.0))
            XF_ref[hd] = jnp.exp(lgf * (r + 1.0))
            XB_ref[hd] = jnp.exp(lgb * (C - r))
            ZE_ref[hd] = jnp.exp(lgf * (C - 1.0 - c))

    for ci in range(TB_R // C):
        for hd in range(H):
            g_chunk = jnp.exp(jnp.full((1, DV), lg_ref[0, hd] * C, F32))
            rows = slice(ci * C, (ci + 1) * C)
            qc = q_ref[rows, hd * DK:(hd + 1) * DK]
            kTc = kT_ref[hd * DK:(hd + 1) * DK, rows]
            vc = v_ref[rows, hd * DV:(hd + 1) * DV]
            S = S_ref[hd]
            kz = (kTc.astype(F32) * ZE_ref[hd]).astype(BF16)
            S_ref[hd] = S * g_chunk + _dot(kz, vc)
            p = (_dot(qc, kTc) * D_ref[hd]).astype(BF16)
            qf = (qc.astype(F32) * XF_ref[hd]).astype(BF16)
            qb = (qc.astype(F32) * XB_ref[hd]).astype(BF16)
            o = _dot(p, vc) + _dot(qf, S.astype(BF16)) + _dot(qb, sb_ref[hd, ci])
            mu = jnp.mean(o, axis=-1, keepdims=True)
            d = o - mu
            var = jnp.mean(d * d, axis=-1, keepdims=True)
            on = d * lax.rsqrt(var + EPS)
            rin_ref[rows, hd * DV:(hd + 1) * DV] = (sgw_ref[rows, hd * DV:(hd + 1) * DV].astype(F32) * on).astype(BF16)

    r = _dot(rin_ref[...], pr_ref[...])
    m = sga_ref[...].astype(F32) * r + sgf_ref[...].astype(F32) * fo_ref[...].astype(F32)
    y = _dot(m.astype(BF16), wo_ref[...])
    o_ref[...] = x_ref[...] + g_ref[0:1, :] * y


def _retmix(lg, q, kT, v, s0, sb, sgw, x1, tables, fo, sga, sgf, p_ret, w_out):
    tb = TB_R
    row = lambda i: (i, 0)
    return pl.pallas_call(
        _retmix_kernel,
        grid=(L // tb,),
        in_specs=[
            pl.BlockSpec(memory_space=pltpu.SMEM),
            pl.BlockSpec((tb, QK), row),
            pl.BlockSpec((QK, tb), lambda i: (0, i)),
            pl.BlockSpec((tb, VW), row),
            _resident((H, DK, DV)),
            pl.BlockSpec((H, tb // C, DK, DV), lambda i: (0, i, 0, 0)),
            pl.BlockSpec((tb, VW), row),
            pl.BlockSpec((tb, D), row),
            _mod(tables, 5)[1],
            pl.BlockSpec((tb, D), row), pl.BlockSpec((tb, D), row), pl.BlockSpec((tb, D), row),
            _resident((VW, D)), _resident((D, D)),
        ],
        out_specs=pl.BlockSpec((tb, D), row),
        out_shape=jax.ShapeDtypeStruct((L, D), F32),
        scratch_shapes=[pltpu.VMEM((H, DK, DV), F32)] + [pltpu.VMEM((H, C, C), F32)] * 4
                       + [pltpu.VMEM((tb, VW), BF16)],
        compiler_params=_cparams(1),
        name="retmix",
    )(lg, q, kT, v, s0, sb, sgw, x1, tables[1], fo, sga, sgf, p_ret, w_out)


def _rope_tables():
    rows = L // GRID_W
    row = np.repeat(np.arange(rows, dtype=np.float64), GRID_W)
    col = np.tile(np.arange(GRID_W, dtype=np.float64), rows)
    n_freq = DK // 4
    inv = ROPE_BASE ** (-np.arange(n_freq, dtype=np.float64) / n_freq)
    ang = np.concatenate([row[:, None] * inv, col[:, None] * inv], axis=-1)
    return np.cos(ang), np.sin(ang)


def _dft_tables():
    two_pi = 2.0 * np.pi
    th = (np.outer(np.arange(DG), np.arange(DG)) % DG) * (two_pi / DG)
    md = np.concatenate([np.cos(th), -np.sin(th)], axis=1)
    th = (np.outer(np.arange(N1), np.arange(N1)) % N1) * (two_pi / N1)
    c1, s1 = np.cos(th), np.sin(th)
    m1 = np.block([[c1, s1], [-s1, c1]])
    k = np.arange(N1)[:, None, None] + N1 * np.arange(N2)[None, :, None]
    th = ((k * np.arange(N2)[None, None, :]) % L) * (two_pi / L)
    t3 = np.concatenate([np.cos(th), np.sin(th)], axis=-1) / np.sqrt(float(L) * DG)
    f32 = lambda a: jnp.asarray(a.astype(np.float32))
    return f32(md).astype(BF16), f32(m1).astype(BF16), f32(t3).astype(BF16)


def kernel(x, c, ctx, c_ctx, w_ada, b_ada, norm_ffn1, w13_ffn1, w2_ffn1, norm_mix, w_in, decay_fwd, decay_bwd,
           ret_gn_w, p_ret, p_four, w_out, norm_ffn2, w13_ffn2, w2_ffn2, norm_final):
    assert x.shape == (1, L, D) and ctx.shape == (1, CTX, D) and w_ada.shape[0] == 1

    pm = np.zeros((DK, DK), np.float32)
    pm[np.concatenate([np.arange(0, DK, 2), np.arange(1, DK, 2)]), np.arange(DK)] = 1.0
    perm = jnp.asarray(pm).astype(BF16)
    eye = jnp.asarray(np.eye(DK, dtype=np.float32)).astype(BF16)

    lg = jnp.stack([jax.nn.log_sigmoid(decay_fwd[0].astype(F32)),
                    jax.nn.log_sigmoid(decay_bwd[0].astype(F32))])
    cos, sin = _rope_tables()
    qscale = DK ** -0.5
    f32c = lambda a: jnp.asarray(np.ascontiguousarray(a).astype(np.float32))
    cq, sq, ck, sk = f32c(cos * qscale), f32c(sin * qscale), f32c(cos.T), f32c(sin.T)
    md, m1, t3 = _dft_tables()

    cs = jnp.concatenate([c, c_ctx[None, :], jnp.zeros((6, D), F32)], axis=0).T
    mods_a = _mods(cs, w_ada[0], b_ada)

    x1, ctx1, w13b, w2b, wall, mods_b = _ffn1(x[0], ctx[0], mods_a, norm_ffn1, w13_ffn1[0], w2_ffn1[0],
                                              (w13_ffn2[0], w2_ffn2[0]), w_in[0], perm, w_ada[0], cs, b_ada)
    mods = (mods_a, mods_b)

    s_f, s_b, wkT = _ctx_state(lg, ctx1, mods, norm_mix, eye, wall)

    q, kT, v, sgw, G, sga, sgf, sb, pr, pf, wo = _proj(lg, x1, mods, norm_mix, wall, wkT, md, ret_gn_w, cq, sq, ck, sk,
                                                     s_b, (p_ret[0], p_four[0], w_out[0]))
    a = _fft1(m1, G.reshape(2, N1, N2, D))
    fo = _fft3(t3, a, pf).reshape(L, D)

    x2 = _retmix(lg, q, kT, v, s_f, sb, sgw, x1, mods, fo, sga, sgf, pr, wo)
    out = _ffn2(x2, mods, norm_ffn2, w13b, w2b, norm_final[None, :])
    return out[None]
```
